```python
import jax, jax.numpy as jnp
from jax import lax
import numpy as np

D_MODEL = 1024
BATCH = 2
SEQ = 8192
DEPTH = 2

N_MIXERS = 2
N_HEADS = 16
HEAD_DIM = D_MODEL // N_HEADS
N_KV_GROUPS = 2
HEADS_PER_GROUP = N_HEADS // N_KV_GROUPS
KV_DIM = N_KV_GROUPS * HEAD_DIM
N_BRANCH = 3
CMP_STRIDE = 16
CMP_BLOCK = 2 * CMP_STRIDE
CMP_HIDDEN = 256
SEL_BLOCK = 64
SEL_TOPK = 16
WINDOW = 512
Q_BLOCK = 128
ROPE_THETA = 10000.0
IN_SIZES = (D_MODEL,) + (KV_DIM,) * 6 + (N_BRANCH * N_HEADS,)
IN_COLS = sum(IN_SIZES)
POOL_WINDOWS = (2, 4, 8, 16)
N_POOL_GROUPS = len(POOL_WINDOWS)
POOL_GROUP_DIM = D_MODEL // N_POOL_GROUPS
D_FF = 2816
CONV_WIDTH = 3
NORM_EPS = 1e-6
NEG_INF = -1e30
FORCE_BONUS = 1e4

kernel_name = 'hybrid_nsa_pool_convffn'


def rmsnorm(x, g):
    xf = x.astype(jnp.float32)
    y = xf * lax.rsqrt(jnp.mean(xf * xf, axis=-1, keepdims=True) + NORM_EPS)
    return (y * g.astype(jnp.float32)).astype(x.dtype)


def rope_tables(pos):
    inv = 1.0 / (ROPE_THETA ** (jnp.arange(0, HEAD_DIM, 2, dtype=jnp.float32) / HEAD_DIM))
    ang = pos.astype(jnp.float32)[:, None] * inv[None, :]
    return jnp.cos(ang), jnp.sin(ang)


def apply_rope(a, cos, sin):
    a1, a2 = jnp.split(a.astype(jnp.float32), 2, axis=-1)
    c, s = cos[:, None, :], sin[:, None, :]
    return jnp.concatenate([a1 * c - a2 * s, a1 * s + a2 * c], axis=-1).astype(a.dtype)


def masked_softmax(s, mask):
    s = jnp.where(mask, s.astype(jnp.float32), NEG_INF)
    return jax.nn.softmax(s, axis=-1) * jnp.any(mask, axis=-1, keepdims=True)


def compress_blocks(a, pos_emb, w1, b1, w2, b2):
    B, T, G, dh = a.shape
    c = a.reshape(B, T // CMP_STRIDE, CMP_STRIDE, G, dh)
    blocks = jnp.concatenate([c[:, :-1], c[:, 1:]], axis=2) + pos_emb[:, None, :]
    flat = blocks.transpose(0, 1, 3, 2, 4).reshape(B, T // CMP_STRIDE - 1, G, CMP_BLOCK * dh)
    return jax.nn.gelu(flat @ w1 + b1) @ w2 + b2


def cmp_sel_overlap(n_cmp, n_blk):
    j = np.arange(n_cmp)[:, None]
    s = np.arange(n_blk)[None, :]
    lo = np.maximum(j * CMP_STRIDE, s * SEL_BLOCK)
    hi = np.minimum(j * CMP_STRIDE + CMP_BLOCK, (s + 1) * SEL_BLOCK)
    return jnp.asarray(np.clip(hi - lo, 0, None) / CMP_BLOCK, dtype=jnp.float32)


def gather_blocks(blocks, sel):
    return jax.vmap(jax.vmap(lambda bl, idx: bl[idx]))(blocks, sel)


def nsa_mixer(h, w_in, ck_pos, ck_w1, ck_b1, ck_w2, ck_b2,
              cv_pos, cv_w1, cv_b1, cv_w2, cv_b2, w_out):
    B, T, _ = h.shape
    G, HG, dh = N_KV_GROUPS, HEADS_PER_GROUP, HEAD_DIM
    n_cmp = T // CMP_STRIDE - 1
    n_blk = T // SEL_BLOCK
    n_sel = min(SEL_TOPK, n_blk)
    n_chunks = T // Q_BLOCK
    scale = HEAD_DIM ** -0.5
    split_at = [int(v) for v in np.cumsum(IN_SIZES)[:-1]]
    q, k_c, v_c, k_s, v_s, k_w, v_w, g_logit = jnp.split(h @ w_in, split_at, axis=-1)
    q = q.reshape(B, T, N_HEADS, dh)
    k_c, v_c, k_s, v_s, k_w, v_w = [a.reshape(B, T, G, dh) for a in (k_c, v_c, k_s, v_s, k_w, v_w)]
    cos, sin = rope_tables(jnp.arange(T))
    q = apply_rope(q, cos, sin)
    k_s = apply_rope(k_s, cos, sin)
    k_w = apply_rope(k_w, cos, sin)
    kc = compress_blocks(k_c, ck_pos, ck_w1, ck_b1, ck_w2, ck_b2)
    vc = compress_blocks(v_c, cv_pos, cv_w1, cv_b1, cv_w2, cv_b2)
    cmp_end = jnp.arange(n_cmp) * CMP_STRIDE + (CMP_BLOCK - 1)
    ccos, csin = rope_tables(cmp_end)
    kc = apply_rope(kc, ccos, csin)
    qg = q.reshape(B, T, G, HG, dh).transpose(0, 2, 3, 1, 4)
    kc = kc.transpose(0, 2, 1, 3)
    vc = vc.transpose(0, 2, 1, 3)
    kb = k_s.transpose(0, 2, 1, 3).reshape(B, G, n_blk, SEL_BLOCK, dh)
    vb = v_s.transpose(0, 2, 1, 3).reshape(B, G, n_blk, SEL_BLOCK, dh)
    pad_w = ((0, 0), (0, 0), (WINDOW, 0), (0, 0))
    kw = jnp.pad(k_w.transpose(0, 2, 1, 3), pad_w)
    vw = jnp.pad(v_w.transpose(0, 2, 1, 3), pad_w)
    gates = jax.nn.sigmoid(g_logit).reshape(B, T, G, HG, N_BRANCH).transpose(0, 2, 3, 1, 4)
    overlap = cmp_sel_overlap(n_cmp, n_blk)
    blk_ids = jnp.arange(n_blk)
    win_off = jnp.arange(Q_BLOCK + WINDOW)
    sel_off = jnp.arange(SEL_BLOCK)

    def chunk(i):
        s0 = i * Q_BLOCK
        t = s0 + jnp.arange(Q_BLOCK)
        qb = lax.dynamic_slice_in_dim(qg, s0, Q_BLOCK, axis=3)
        m_c = cmp_end[None, :] <= t[:, None]
        p_c = masked_softmax(jnp.einsum('bghqd,bgnd->bghqn', qb, kc) * scale, m_c)
        o_c = jnp.einsum('bghqn,bgnd->bghqd', p_c.astype(vc.dtype), vc)
        imp = jnp.einsum('bghqn,ns->bgqs', p_c, overlap)
        cur = t // SEL_BLOCK
        forced = (blk_ids == 0) | (blk_ids == cur[:, None]) | (blk_ids == cur[:, None] - 1)
        score = jnp.where(blk_ids <= cur[:, None], imp + FORCE_BONUS * forced, NEG_INF)
        _, sel = lax.top_k(score, n_sel)
        kg = gather_blocks(kb, sel)
        vg = gather_blocks(vb, sel)
        tok = sel[..., None] * SEL_BLOCK + sel_off
        m_s = (tok <= t[:, None, None]).reshape(B, G, 1, Q_BLOCK, n_sel * SEL_BLOCK)
        s_s = jnp.einsum('bghqd,bgqnkd->bghqnk', qb, kg).reshape(B, G, HG, Q_BLOCK, n_sel * SEL_BLOCK)
        p_s = masked_softmax(s_s * scale, m_s).reshape(B, G, HG, Q_BLOCK, n_sel, SEL_BLOCK)
        o_s = jnp.einsum('bghqnk,bgqnkd->bghqd', p_s.astype(vg.dtype), vg)
        kwb = lax.dynamic_slice_in_dim(kw, s0, Q_BLOCK + WINDOW, axis=2)
        vwb = lax.dynamic_slice_in_dim(vw, s0, Q_BLOCK + WINDOW, axis=2)
        kpos = s0 - WINDOW + win_off
        m_w = (kpos[None, :] <= t[:, None]) & (kpos[None, :] > t[:, None] - WINDOW) & (kpos[None, :] >= 0)
        p_w = masked_softmax(jnp.einsum('bghqd,bgkd->bghqk', qb, kwb) * scale, m_w)
        o_w = jnp.einsum('bghqk,bgkd->bghqd', p_w.astype(vwb.dtype), vwb)
        gb = lax.dynamic_slice_in_dim(gates, s0, Q_BLOCK, axis=3)
        return gb[..., 0:1] * o_c + gb[..., 1:2] * o_s + gb[..., 2:3] * o_w

    o = lax.map(chunk, jnp.arange(n_chunks))
    o = o.transpose(1, 0, 4, 2, 3, 5).reshape(B, T, D_MODEL)
    return o @ w_out


def pool_mixer(h, pool_w, pool_b, pool_scale):
    B, T, _ = h.shape
    hf = h.astype(jnp.float32)
    csum = jnp.pad(jnp.cumsum(hf, axis=1), ((0, 0), (1, 0), (0, 0)))
    count_base = jnp.arange(1, T + 1, dtype=jnp.float32)[:, None]
    outs = []
    for g, w in enumerate(POOL_WINDOWS):
        sl = slice(g * POOL_GROUP_DIM, (g + 1) * POOL_GROUP_DIM)
        c = csum[..., sl]
        lag = jnp.pad(c, ((0, 0), (w - 1, 0), (0, 0)))[:, :T]
        count = jnp.minimum(count_base, float(w))
        outs.append((c[:, 1:] - lag) / count - hf[..., sl])
    pooled = jnp.stack(outs, axis=2)
    y = jnp.einsum('btgc,gcd->btgd', pooled, pool_w.astype(jnp.float32)) + pool_b
    return (y.reshape(B, T, D_MODEL) * pool_scale).astype(h.dtype)


def conv_ffn(h, w_up, conv_w, conv_b, w_down):
    u = h @ w_up
    T = u.shape[1]
    up = jnp.pad(u, ((0, 0), (CONV_WIDTH - 1, 0), (0, 0)))
    conv = sum(conv_w[k] * up[:, k:k + T] for k in range(CONV_WIDTH)) + conv_b
    gate, val = jnp.split(conv, 2, axis=-1)
    return (jax.nn.silu(gate) * val) @ w_down


def setup_inputs(seed: int = 0) -> dict:
    key = jax.random.key(seed)
    ks = iter(jax.random.split(key, 40))
    f32 = jnp.float32

    def w(shape, fan_in):
        return jax.random.normal(next(ks), shape, f32) * (fan_in ** -0.5)

    def gain(n):
        return 1.0 + 0.02 * jax.random.normal(next(ks), (n,), f32)

    def bias(shape):
        return 0.01 * jax.random.normal(next(ks), shape, f32)

    def ffn_params():
        return (gain(D_MODEL), w((D_MODEL, 2 * D_FF), D_MODEL),
                w((CONV_WIDTH, 2 * D_FF), CONV_WIDTH), bias((2 * D_FF,)),
                w((D_FF, D_MODEL), D_FF))

    x = jax.random.normal(next(ks), (BATCH, SEQ, D_MODEL), f32)
    norm_mix_0 = gain(D_MODEL)
    nsa_w_in = w((D_MODEL, IN_COLS), D_MODEL)
    cmp_k_pos = 0.02 * jax.random.normal(next(ks), (CMP_BLOCK, HEAD_DIM), f32)
    cmp_k_w1 = w((CMP_BLOCK * HEAD_DIM, CMP_HIDDEN), CMP_BLOCK * HEAD_DIM)
    cmp_k_b1 = bias((CMP_HIDDEN,))
    cmp_k_w2 = w((CMP_HIDDEN, HEAD_DIM), CMP_HIDDEN)
    cmp_k_b2 = bias((HEAD_DIM,))
    cmp_v_pos = 0.02 * jax.random.normal(next(ks), (CMP_BLOCK, HEAD_DIM), f32)
    cmp_v_w1 = w((CMP_BLOCK * HEAD_DIM, CMP_HIDDEN), CMP_BLOCK * HEAD_DIM)
    cmp_v_b1 = bias((CMP_HIDDEN,))
    cmp_v_w2 = w((CMP_HIDDEN, HEAD_DIM), CMP_HIDDEN)
    cmp_v_b2 = bias((HEAD_DIM,))
    nsa_w_out = w((D_MODEL, D_MODEL), D_MODEL)
    norm_ffn_0, ffn_up_0, ffn_conv_w_0, ffn_conv_b_0, ffn_down_0 = ffn_params()
    norm_mix_1 = gain(D_MODEL)
    pool_w = w((N_POOL_GROUPS, POOL_GROUP_DIM, POOL_GROUP_DIM), POOL_GROUP_DIM)
    pool_b = bias((N_POOL_GROUPS, POOL_GROUP_DIM))
    pool_scale = 1.0 + 0.1 * jax.random.normal(next(ks), (D_MODEL,), f32)
    norm_ffn_1, ffn_up_1, ffn_conv_w_1, ffn_conv_b_1, ffn_down_1 = ffn_params()
    norm_final = gain(D_MODEL)
    return {
        'x': x, 'norm_mix_0': norm_mix_0, 'nsa_w_in': nsa_w_in,
        'cmp_k_pos': cmp_k_pos, 'cmp_k_w1': cmp_k_w1, 'cmp_k_b1': cmp_k_b1,
        'cmp_k_w2': cmp_k_w2, 'cmp_k_b2': cmp_k_b2,
        'cmp_v_pos': cmp_v_pos, 'cmp_v_w1': cmp_v_w1, 'cmp_v_b1': cmp_v_b1,
        'cmp_v_w2': cmp_v_w2, 'cmp_v_b2': cmp_v_b2, 'nsa_w_out': nsa_w_out,
        'norm_ffn_0': norm_ffn_0, 'ffn_up_0': ffn_up_0, 'ffn_conv_w_0': ffn_conv_w_0,
        'ffn_conv_b_0': ffn_conv_b_0, 'ffn_down_0': ffn_down_0,
        'norm_mix_1': norm_mix_1, 'pool_w': pool_w, 'pool_b': pool_b, 'pool_scale': pool_scale,
        'norm_ffn_1': norm_ffn_1, 'ffn_up_1': ffn_up_1, 'ffn_conv_w_1': ffn_conv_w_1,
        'ffn_conv_b_1': ffn_conv_b_1, 'ffn_down_1': ffn_down_1, 'norm_final': norm_final,
    }


def reference(x, norm_mix_0, nsa_w_in, cmp_k_pos, cmp_k_w1, cmp_k_b1, cmp_k_w2, cmp_k_b2,
              cmp_v_pos, cmp_v_w1, cmp_v_b1, cmp_v_w2, cmp_v_b2, nsa_w_out,
              norm_ffn_0, ffn_up_0, ffn_conv_w_0, ffn_conv_b_0, ffn_down_0,
              norm_mix_1, pool_w, pool_b, pool_scale,
              norm_ffn_1, ffn_up_1, ffn_conv_w_1, ffn_conv_b_1, ffn_down_1, norm_final):
    mixers = [
        lambda h: nsa_mixer(h, nsa_w_in, cmp_k_pos, cmp_k_w1, cmp_k_b1, cmp_k_w2, cmp_k_b2,
                            cmp_v_pos, cmp_v_w1, cmp_v_b1, cmp_v_w2, cmp_v_b2, nsa_w_out),
        lambda h: pool_mixer(h, pool_w, pool_b, pool_scale),
    ]
    mix_norms = [norm_mix_0, norm_mix_1]
    ffns = [(norm_ffn_0, ffn_up_0, ffn_conv_w_0, ffn_conv_b_0, ffn_down_0),
            (norm_ffn_1, ffn_up_1, ffn_conv_w_1, ffn_conv_b_1, ffn_down_1)]
    for i in range(DEPTH):
        x = x + mixers[i % N_MIXERS](rmsnorm(x, mix_norms[i]))
        g, up, cw, cb, down = ffns[i]
        x = x + conv_ffn(rmsnorm(x, g), up, cw, cb, down)
    return rmsnorm(x, norm_final)
```

```python
import functools

import numpy as np
import jax
import jax.numpy as jnp
from jax import lax
from jax.experimental import pallas as pl
from jax.experimental.pallas import tpu as pltpu

F32 = jnp.float32
BF16 = jnp.bfloat16

N_HEADS = 16
HEAD_DIM = 64
N_GROUPS = 2
HEADS_PER_GROUP = N_HEADS // N_GROUPS
N_BRANCH = 3
CMP_STRIDE = 16
CMP_BLOCK = 32
SEL_BLOCK = 64
SEL_TOPK = 16
WINDOW = 512
ROPE_THETA = 10000.0
POOL_WINDOWS = (2, 4, 8, 16)
CONV_WIDTH = 3
NORM_EPS = 1e-6
NEG_INF = -1e30
FORCE_BONUS = 1e4

LANES = 128
SUBLANES = 8
VMEM_LIMIT = 56 * 1024 * 1024

TM_PROJ = 512
TQ = 128
TK_SEL = 512
TM_FFN = 1024
TF_FFN = 256
TM_POOL = 512
HALO_FFN = 8
HALO_POOL = 16


def _rmsnorm(x, g):
    return x * lax.rsqrt(jnp.mean(x * x, axis=-1, keepdims=True) + NORM_EPS) * g


def _nt_dot(a, b):
    return lax.dot_general(a, b, (((1,), (1,)), ((), ())), preferred_element_type=F32)


def _dot(a, b):
    return jnp.dot(a, b, preferred_element_type=F32)


def _inproj_body(x_ref, g_ref, w_ref, cos_ref, sin_ref,
                 q_ref, qsw_ref, ks_ref, kw_ref, kc_ref, vc_ref, vs_ref, vw_ref, gate_ref, *, tm):
    i = pl.program_id(1)
    h = _rmsnorm(x_ref[...], g_ref[...]).astype(BF16)
    res = _dot(h, w_ref[...])
    cos = cos_ref[...]
    sin = sin_ref[...]
    lane = lax.broadcasted_iota(jnp.int32, (tm, LANES), 1)
    first_half = (lane & (HEAD_DIM - 1)) < (HEAD_DIM // 2)

    def rope(xc):
        sw = jnp.where(first_half, pltpu.roll(xc, LANES - HEAD_DIM // 2, 1),
                       pltpu.roll(xc, HEAD_DIM // 2, 1))
        return xc * cos + sw * sin

    scale = HEAD_DIM ** -0.5
    n_pairs = N_HEADS // 2
    for p in range(n_pairs):
        qc = rope(res[:, p * LANES:(p + 1) * LANES]) * scale
        q_ref[:, p * LANES:(p + 1) * LANES] = qc.astype(BF16)
        qsw_ref[:, p * LANES:(p + 1) * LANES] = pltpu.roll(qc, HEAD_DIM, 1).astype(BF16)
    off = n_pairs * LANES
    row_t = i * tm + lax.broadcasted_iota(jnp.int32, (tm, LANES), 0)
    onehot = jnp.where(lane == (row_t >> 6), 1.0, 0.0).astype(BF16)
    for g in range(N_GROUPS):
        ks_ref[g, :, 0:LANES] = onehot
        ks_ref[g, :, LANES:2 * LANES] = rope(res[:, off + g * LANES: off + (g + 1) * LANES]).astype(BF16)
    off += N_GROUPS * LANES
    for g in range(N_GROUPS):
        kw_ref[g] = rope(res[:, off + g * LANES: off + (g + 1) * LANES]).astype(BF16)
    off += N_GROUPS * LANES
    kc_ref[...] = res[:, off:off + LANES]
    vc_ref[...] = res[:, off + LANES:off + 2 * LANES]
    vs_ref[...] = res[:, off + 2 * LANES:off + 3 * LANES].astype(BF16)
    vw_ref[...] = res[:, off + 3 * LANES:off + 4 * LANES].astype(BF16)
    off += 4 * LANES
    for g in range(N_GROUPS):
        z = res[:, off + g * LANES: off + (g + 1) * LANES]
        gate_ref[g] = 1.0 / (1.0 + jnp.exp(-z))


def _in_proj(x, g, wp, cos4, sin4):
    B, T, D = x.shape
    tm = TM_PROJ
    ncols = wp.shape[1]
    grid = (B, T // tm)
    tok = lambda last: pl.BlockSpec((None, tm, last), lambda b, i: (b, i, 0))
    grp = lambda last: pl.BlockSpec((None, N_GROUPS, tm, last), lambda b, i: (b, 0, i, 0))
    out_shape = (
        jax.ShapeDtypeStruct((B, T, N_HEADS * HEAD_DIM), BF16),
        jax.ShapeDtypeStruct((B, T, N_HEADS * HEAD_DIM), BF16),
        jax.ShapeDtypeStruct((B, N_GROUPS, T, 2 * LANES), BF16),
        jax.ShapeDtypeStruct((B, N_GROUPS, T, LANES), BF16),
        jax.ShapeDtypeStruct((B, T, LANES), F32),
        jax.ShapeDtypeStruct((B, T, LANES), F32),
        jax.ShapeDtypeStruct((B, T, LANES), BF16),
        jax.ShapeDtypeStruct((B, T, LANES), BF16),
        jax.ShapeDtypeStruct((B, N_GROUPS, T, LANES), F32),
    )
    return pl.pallas_call(
        functools.partial(_inproj_body, tm=tm),
        grid=grid,
        in_specs=[tok(D),
                  pl.BlockSpec((1, D), lambda b, i: (0, 0)),
                  pl.BlockSpec((D, ncols), lambda b, i: (0, 0)),
                  pl.BlockSpec((tm, LANES), lambda b, i: (i, 0)),
                  pl.BlockSpec((tm, LANES), lambda b, i: (i, 0))],
        out_specs=(tok(N_HEADS * HEAD_DIM), tok(N_HEADS * HEAD_DIM), grp(2 * LANES), grp(LANES),
                   tok(LANES), tok(LANES), tok(LANES), tok(LANES), grp(LANES)),
        out_shape=out_shape,
        compiler_params=pltpu.CompilerParams(dimension_semantics=("parallel", "parallel"),
                                             vmem_limit_bytes=VMEM_LIMIT),
        name="in_proj",
    )(x, g, wp, cos4, sin4)


def _gelu_tanh(x):
    return 0.5 * x * (1.0 + jnp.tanh(np.sqrt(2.0 / np.pi).astype(np.float32) * (x + 0.044715 * (x * x * x))))


def _compress_body(kc_ref, vc_ref,
                   kw1e_ref, kw1_ref, kpos_ref, kb1_ref, kw2_ref, kb2_ref,
                   vw1e_ref, vw1_ref, vpos_ref, vb1_ref, vw2a_ref, vw2b_ref, vb2_ref,
                   cos_ref, sin_ref, kco_ref, vco_ref, *, n_rows, hidden):
    row = lax.broadcasted_iota(jnp.int32, (n_rows, LANES), 0)
    keep = row < (n_rows - 1)

    def hidden_act(c_ref, w1e_ref, w1_ref, pos_ref, b1_ref):
        c2 = c_ref[...].astype(BF16)
        ab = _dot(c2, w1e_ref[...])
        posb = _dot(pos_ref[...], w1_ref[...])[0:1, :] + b1_ref[...]
        outs = []
        for g in range(N_GROUPS):
            a = ab[:, g * hidden:(g + 1) * hidden]
            b = ab[:, (N_GROUPS + g) * hidden:(N_GROUPS + g + 1) * hidden]
            pre = a + pltpu.roll(b, n_rows - 1, 0) + posb
            outs.append(_gelu_tanh(pre).astype(BF16))
        return outs

    hk = hidden_act(kc_ref, kw1e_ref, kw1_ref, kpos_ref, kb1_ref)
    cos = cos_ref[...]
    sin = sin_ref[...]
    for g in range(N_GROUPS):
        r = _dot(hk[g], kw2_ref[...]) + kb2_ref[...]
        kco_ref[g] = jnp.where(keep, r * cos + pltpu.roll(r, HEAD_DIM, 1) * sin, 0.0).astype(BF16)
    hv = hidden_act(vc_ref, vw1e_ref, vw1_ref, vpos_ref, vb1_ref)
    v = _dot(hv[0], vw2a_ref[...]) + _dot(hv[1], vw2b_ref[...]) + vb2_ref[...]
    vco_ref[...] = jnp.where(keep, v, 0.0).astype(BF16)


def _compress(kc2, vc2, kparams, vparams, ccos, csin):
    B, n_rows, width = kc2.shape
    hidden = kparams[3].shape[1]
    full = lambda a: pl.BlockSpec(a.shape, lambda b: (0,) * a.ndim)
    blk = pl.BlockSpec((None, n_rows, width), lambda b: (b, 0, 0))
    weights = list(kparams) + list(vparams) + [ccos, csin]
    return pl.pallas_call(
        functools.partial(_compress_body, n_rows=n_rows, hidden=hidden),
        grid=(B,),
        in_specs=[blk, blk] + [full(a) for a in weights],
        out_specs=(pl.BlockSpec((None, N_GROUPS, n_rows, LANES), lambda b: (b, 0, 0, 0)),
                   pl.BlockSpec((None, n_rows, LANES), lambda b: (b, 0, 0))),
        out_shape=(jax.ShapeDtypeStruct((B, N_GROUPS, n_rows, LANES), BF16),
                   jax.ShapeDtypeStruct((B, n_rows, LANES), BF16)),
        compiler_params=pltpu.CompilerParams(dimension_semantics=("parallel",),
                                             vmem_limit_bytes=VMEM_LIMIT),
        name="compress",
    )(kc2, vc2, *weights)


def _softmax_rows(s):
    m = jnp.max(s, axis=1, keepdims=True)
    e = jnp.exp(s - m)
    return e, jnp.sum(e, axis=1, keepdims=True)


def _attn_body(q_ref, qsw_ref, ks_ref, kw_ref, vs_ref, vw_ref, kc_ref, vc_ref, gate_ref, ov_ref,
               o_ref, qaug_ref, m_ref, l_ref, acc_ref, *, tq, tk, n_cmp_pad, n_blk):
    hg = HEADS_PER_GROUP
    g = pl.program_id(1)
    i = pl.program_id(2)
    s0 = i * tq
    t_col = s0 + lax.broadcasted_iota(jnp.int32, (tq, 1), 0)

    def qpair(h):
        src = q_ref if h % 2 == 0 else qsw_ref
        return src[:, (h // 2) * LANES:(h // 2 + 1) * LANES]

    q_all = jnp.concatenate([qpair(h) for h in range(hg)], axis=0)

    sc = _nt_dot(q_all, kc_ref[...])
    j_row = lax.broadcasted_iota(jnp.int32, (1, n_cmp_pad), 1)
    bias_c = jnp.where(j_row * CMP_STRIDE + (CMP_BLOCK - 1) <= t_col, 0.0, NEG_INF)
    vis = jnp.where(t_col >= CMP_BLOCK - 1, 1.0, 0.0)
    p_parts = []
    psum = jnp.zeros((tq, n_cmp_pad), F32)
    for h in range(hg):
        e, l = _softmax_rows(sc[h * tq:(h + 1) * tq] + bias_c)
        p = e * (vis / l)
        psum = psum + p
        p_parts.append(p.astype(BF16))
    oc = _dot(jnp.concatenate(p_parts, axis=0), vc_ref[...])

    hi = psum.astype(BF16)
    r1 = psum - hi.astype(F32)
    mid = r1.astype(BF16)
    lo = (r1 - mid.astype(F32)).astype(BF16)
    ov = ov_ref[...]
    imp = _dot(hi, ov) + _dot(mid, ov) + _dot(lo, ov)
    blk = lax.broadcasted_iota(jnp.int32, (tq, LANES), 1)
    cur = t_col >> 6
    forced = (blk == 0) | (blk == cur) | (blk == cur - 1)
    score = jnp.where(blk <= cur, imp + jnp.where(forced, FORCE_BONUS, 0.0), NEG_INF)
    st = score.T
    blk_t = lax.broadcasted_iota(jnp.int32, (LANES, tq), 0).astype(F32)
    sel_t = jnp.zeros((LANES, tq), F32)
    for _ in range(min(SEL_TOPK, n_blk)):
        mx = jnp.max(st, axis=0, keepdims=True)
        idx = jnp.min(jnp.where(st == mx, blk_t, float(LANES)), axis=0, keepdims=True)
        hit = blk_t == idx
        sel_t = jnp.where(hit, 1.0, sel_t)
        st = jnp.where(hit, -jnp.inf, st)
    sel = sel_t.T
    bias_s = jnp.where((sel > 0.5) & (blk <= cur), 0.0, NEG_INF).astype(BF16)

    for h in range(hg):
        qaug_ref[h * tq:(h + 1) * tq, 0:LANES] = bias_s
        qaug_ref[h * tq:(h + 1) * tq, LANES:2 * LANES] = qpair(h)
    m_ref[...] = jnp.full(m_ref.shape, NEG_INF, F32)
    l_ref[...] = jnp.zeros(l_ref.shape, F32)
    acc_ref[...] = jnp.zeros(acc_ref.shape, F32)

    def sel_step(kt, causal):
        start = pl.multiple_of(kt * tk, tk)
        s = _nt_dot(qaug_ref[...], ks_ref[pl.ds(start, tk), :])
        if causal:
            kpos = start + lax.broadcasted_iota(jnp.int32, (1, tk), 1)
            bias_d = jnp.where(kpos <= t_col, 0.0, NEG_INF)
            s = jnp.concatenate([s[h * tq:(h + 1) * tq] + bias_d for h in range(hg)], axis=0)
        m_old = m_ref[...]
        m_new = jnp.maximum(m_old, jnp.max(s, axis=1, keepdims=True))
        alpha = jnp.exp(m_old - m_new)
        p = jnp.exp(s - m_new)
        l_ref[...] = alpha * l_ref[...] + jnp.sum(p, axis=1, keepdims=True)
        acc_ref[...] = alpha * acc_ref[...] + _dot(p.astype(BF16), vs_ref[pl.ds(start, tk), :])
        m_ref[...] = m_new

    k_diag = s0 // tk

    def full_tile(kt, carry):
        sel_step(kt, False)
        return carry

    lax.fori_loop(0, k_diag, full_tile, 0)
    sel_step(k_diag, True)
    o_sel = acc_ref[...] / l_ref[...]

    n_win = WINDOW + tq
    kstart = pl.multiple_of(jnp.maximum(s0 - WINDOW, 0), tq)
    sw = _nt_dot(q_all, kw_ref[pl.ds(kstart, n_win), :])
    kpos_w = kstart + lax.broadcasted_iota(jnp.int32, (1, n_win), 1)
    bias_w = jnp.where((kpos_w <= t_col) & (kpos_w > t_col - WINDOW), 0.0, NEG_INF)
    pw_parts = []
    linv_w = []
    for h in range(hg):
        e, l = _softmax_rows(sw[h * tq:(h + 1) * tq] + bias_w)
        pw_parts.append(e.astype(BF16))
        linv_w.append(1.0 / l)
    o_win = _dot(jnp.concatenate(pw_parts, axis=0), vw_ref[pl.ds(kstart, n_win), :])

    gates = gate_ref[...]
    outs = []
    for h in range(hg):
        rows = slice(h * tq, (h + 1) * tq)
        gc = gates[:, N_BRANCH * h + 0:N_BRANCH * h + 1]
        gs = gates[:, N_BRANCH * h + 1:N_BRANCH * h + 2]
        gw = gates[:, N_BRANCH * h + 2:N_BRANCH * h + 3]
        outs.append(gc * oc[rows] + gs * o_sel[rows] + (gw * linv_w[h]) * o_win[rows])
    lane = lax.broadcasted_iota(jnp.int32, (tq, LANES), 1)
    low = lane < HEAD_DIM
    is_g0 = g == 0
    for p in range(hg // 2):
        a = outs[2 * p]
        b = outs[2 * p + 1]
        xa = jnp.where(is_g0, a, pltpu.roll(a, HEAD_DIM, 1))
        xb = jnp.where(is_g0, pltpu.roll(b, HEAD_DIM, 1), b)
        o_ref[:, p * LANES:(p + 1) * LANES] = jnp.where(low, xa, xb).astype(BF16)


def _nsa_attn(q, qsw, ks, kw, vs, vw, kco, vco, gates, overlap):
    B, T, _ = q.shape
    tq, tk = TQ, TK_SEL
    n_cmp_pad = kco.shape[2]
    n_blk = T // SEL_BLOCK
    gw = HEADS_PER_GROUP * HEAD_DIM
    rows = HEADS_PER_GROUP * tq
    grid = (B, N_GROUPS, T // tq)
    return pl.pallas_call(
        functools.partial(_attn_body, tq=tq, tk=tk, n_cmp_pad=n_cmp_pad, n_blk=n_blk),
        grid=grid,
        in_specs=[
            pl.BlockSpec((None, tq, gw), lambda b, g, i: (b, i, g)),
            pl.BlockSpec((None, tq, gw), lambda b, g, i: (b, i, g)),
            pl.BlockSpec((None, None, T, 2 * LANES), lambda b, g, i: (b, g, 0, 0)),
            pl.BlockSpec((None, None, T, LANES), lambda b, g, i: (b, g, 0, 0)),
            pl.BlockSpec((None, T, LANES), lambda b, g, i: (b, 0, 0)),
            pl.BlockSpec((None, T, LANES), lambda b, g, i: (b, 0, 0)),
            pl.BlockSpec((None, None, n_cmp_pad, LANES), lambda b, g, i: (b, g, 0, 0)),
            pl.BlockSpec((None, n_cmp_pad, LANES), lambda b, g, i: (b, 0, 0)),
            pl.BlockSpec((None, None, tq, LANES), lambda b, g, i: (b, g, i, 0)),
            pl.BlockSpec(overlap.shape, lambda b, g, i: (0, 0)),
        ],
        out_specs=pl.BlockSpec((None, tq, gw), lambda b, g, i: (b, i, g)),
        out_shape=jax.ShapeDtypeStruct((B, T, N_HEADS * HEAD_DIM), BF16),
        scratch_shapes=[pltpu.VMEM((rows, 2 * LANES), BF16),
                        pltpu.VMEM((rows, 1), F32),
                        pltpu.VMEM((rows, 1), F32),
                        pltpu.VMEM((rows, LANES), F32)],
        compiler_params=pltpu.CompilerParams(
            dimension_semantics=("parallel", "parallel", "arbitrary"),
            vmem_limit_bytes=VMEM_LIMIT),
        name="nsa_attn",
    )(q, qsw, ks, kw, vs, vw, kco, vco, gates, overlap)


def _outproj_body(x_ref, o_ref, w_ref, y_ref):
    y_ref[...] = x_ref[...] + _dot(o_ref[...], w_ref[...])


def _out_proj(x2, o2, w):
    N, D = x2.shape
    tm = TM_PROJ
    return pl.pallas_call(
        _outproj_body,
        grid=(N // tm,),
        in_specs=[pl.BlockSpec((tm, D), lambda i: (i, 0)),
                  pl.BlockSpec((tm, D), lambda i: (i, 0)),
                  pl.BlockSpec(w.shape, lambda i: (0, 0))],
        out_specs=pl.BlockSpec((tm, D), lambda i: (i, 0)),
        out_shape=jax.ShapeDtypeStruct((N, D), F32),
        compiler_params=pltpu.CompilerParams(dimension_semantics=("parallel",),
                                             vmem_limit_bytes=VMEM_LIMIT),
        name="out_proj",
    )(x2, o2, w)


def _ffn_body(x_ref, xh_ref, g_ref, wg_ref, wv_ref, cwg_ref, cwv_ref, cbg_ref, cbv_ref, wd_ref,
              gf_ref, o_ref, h_ref, acc_ref, *, tm, tiles_per_seq, final_norm):
    i = pl.program_id(0)
    j = pl.program_id(1)
    halo = HALO_FFN

    @pl.when(j == 0)
    def _():
        g = g_ref[...]
        hh = _rmsnorm(xh_ref[...], g)
        seq_start = (i % tiles_per_seq) == 0
        h_ref[0:halo, :] = jnp.where(seq_start, 0.0, hh).astype(BF16)
        h_ref[halo:halo + tm, :] = _rmsnorm(x_ref[...], g).astype(BF16)
        acc_ref[...] = jnp.zeros(acc_ref.shape, F32)

    hfull = h_ref[...]

    def conv(w_ref, cw_ref, cb_ref):
        u = _dot(hfull, w_ref[...])
        n = halo + tm
        u1 = pltpu.roll(u, 1, 0)
        u2 = pltpu.roll(u, 2, 0)
        cw = cw_ref[...]
        c = cw[0:1, :] * u2 + cw[1:2, :] * u1 + cw[2:3, :] * u + cb_ref[...]
        return c[halo:n]

    cg = conv(wg_ref, cwg_ref, cbg_ref)
    cv = conv(wv_ref, cwv_ref, cbv_ref)
    a = (cg * (1.0 / (1.0 + jnp.exp(-cg)))) * cv
    acc_ref[...] += _dot(a.astype(BF16), wd_ref[...])

    @pl.when(j == pl.num_programs(1) - 1)
    def _():
        y = x_ref[...] + acc_ref[...]
        if final_norm:
            y = _rmsnorm(y, gf_ref[...])
        o_ref[...] = y


def _conv_ffn(x2, seq_len, g, w_up, conv_w, conv_b, w_down, g_final, final_norm):
    N, D = x2.shape
    dff = w_down.shape[0]
    tm, tf = TM_FFN, TF_FFN
    nj = dff // tf
    halo_blocks = tm // HALO_FFN
    return pl.pallas_call(
        functools.partial(_ffn_body, tm=tm, tiles_per_seq=seq_len // tm, final_norm=final_norm),
        grid=(N // tm, nj),
        in_specs=[
            pl.BlockSpec((tm, D), lambda i, j: (i, 0)),
            pl.BlockSpec((HALO_FFN, D), lambda i, j: (jnp.maximum(i * halo_blocks - 1, 0), 0)),
            pl.BlockSpec((1, D), lambda i, j: (0, 0)),
            pl.BlockSpec((D, tf), lambda i, j: (0, j)),
            pl.BlockSpec((D, tf), lambda i, j: (0, nj + j)),
            pl.BlockSpec((CONV_WIDTH, tf), lambda i, j: (0, j)),
            pl.BlockSpec((CONV_WIDTH, tf), lambda i, j: (0, nj + j)),
            pl.BlockSpec((1, tf), lambda i, j: (0, j)),
            pl.BlockSpec((1, tf), lambda i, j: (0, nj + j)),
            pl.BlockSpec((tf, D), lambda i, j: (j, 0)),
            pl.BlockSpec((1, D), lambda i, j: (0, 0)),
        ],
        out_specs=pl.BlockSpec((tm, D), lambda i, j: (i, 0)),
        out_shape=jax.ShapeDtypeStruct((N, D), F32),
        scratch_shapes=[pltpu.VMEM((HALO_FFN + tm, D), BF16),
                        pltpu.VMEM((tm, D), F32)],
        compiler_params=pltpu.CompilerParams(dimension_semantics=("parallel", "arbitrary"),
                                             vmem_limit_bytes=VMEM_LIMIT),
        name="conv_ffn",
    )(x2, x2, g, w_up, w_up, conv_w, conv_w, conv_b, conv_b, w_down, g_final)


def _pool_body(x_ref, xh_ref, g_ref, w_ref, b_ref, s_ref, o_ref, *, tm, tiles_per_seq, gdim):
    i = pl.program_id(0)
    halo = HALO_POOL
    g = g_ref[...]
    x = x_ref[...]
    h = _rmsnorm(x, g)
    seq_tile = i % tiles_per_seq
    hh = jnp.where(seq_tile == 0, 0.0, _rmsnorm(xh_ref[...], g))
    hext = jnp.concatenate([hh, h], axis=0)
    t_seq = seq_tile * tm + lax.broadcasted_iota(jnp.int32, (tm, 1), 0)
    ys = []
    for gi, w in enumerate(POOL_WINDOWS):
        cols = slice(gi * gdim, (gi + 1) * gdim)
        s = hext[:, cols]
        shift = 1
        while shift < w:
            s = s + pltpu.roll(s, shift, 0)
            shift *= 2
        cnt = jnp.minimum(t_seq + 1, w).astype(F32)
        pooled = s[halo:halo + tm] / cnt - h[:, cols]
        ys.append(_dot(pooled.astype(BF16), w_ref[gi]))
    y = jnp.concatenate(ys, axis=1) + b_ref[...]
    o_ref[...] = x + y * s_ref[...]


def _pool_mix(x2, seq_len, g, pool_w, pool_b, pool_scale):
    N, D = x2.shape
    tm = TM_POOL
    gdim = D // len(POOL_WINDOWS)
    halo_blocks = tm // HALO_POOL
    return pl.pallas_call(
        functools.partial(_pool_body, tm=tm, tiles_per_seq=seq_len // tm, gdim=gdim),
        grid=(N // tm,),
        in_specs=[
            pl.BlockSpec((tm, D), lambda i: (i, 0)),
            pl.BlockSpec((HALO_POOL, D), lambda i: (jnp.maximum(i * halo_blocks - 1, 0), 0)),
            pl.BlockSpec((1, D), lambda i: (0, 0)),
            pl.BlockSpec(pool_w.shape, lambda i: (0, 0, 0)),
            pl.BlockSpec((1, D), lambda i: (0, 0)),
            pl.BlockSpec((1, D), lambda i: (0, 0)),
        ],
        out_specs=pl.BlockSpec((tm, D), lambda i: (i, 0)),
        out_shape=jax.ShapeDtypeStruct((N, D), F32),
        compiler_params=pltpu.CompilerParams(dimension_semantics=("parallel",),
                                             vmem_limit_bytes=VMEM_LIMIT),
        name="pool_mix",
    )(x2, x2, g, pool_w, pool_b, pool_scale)


def _pad_cols(w, width):
    return jnp.pad(w, ((0, 0), (0, width - w.shape[1])))


def _inproj_weight(w_in):
    D = w_in.shape[0]
    kv = N_GROUPS * HEAD_DIM
    sizes = [N_HEADS * HEAD_DIM] + [kv] * 6 + [N_BRANCH * N_HEADS]
    offs = np.concatenate([[0], np.cumsum(sizes)])
    q, k_c, v_c, k_s, v_s, k_w, v_w, gt = [w_in[:, offs[n]:offs[n + 1]] for n in range(8)]
    per_group = lambda w, n: [_pad_cols(w[:, g * n:(g + 1) * n], LANES) for g in range(N_GROUPS)]
    cols = ([q] + per_group(k_s, HEAD_DIM) + per_group(k_w, HEAD_DIM) + [k_c, v_c, v_s, v_w]
            + per_group(gt, N_BRANCH * HEADS_PER_GROUP))
    return jnp.concatenate(cols, axis=1).astype(BF16)


def _rope_tables(pos):
    inv = 1.0 / (ROPE_THETA ** (jnp.arange(0, HEAD_DIM, 2, dtype=F32) / HEAD_DIM))
    ang = pos.astype(F32)[:, None] * inv[None, :]
    return jnp.cos(ang), jnp.sin(ang)


def _expand_w1(w1):
    hdim = w1.shape[1]
    halves = w1.reshape(2, CMP_STRIDE, HEAD_DIM, hdim)
    zeros = jnp.zeros((CMP_STRIDE, HEAD_DIM, hdim), w1.dtype)
    cols = []
    for half in range(2):
        for g in range(N_GROUPS):
            parts = [halves[half] if gg == g else zeros for gg in range(N_GROUPS)]
            cols.append(jnp.concatenate(parts, axis=1).reshape(CMP_STRIDE * N_GROUPS * HEAD_DIM, hdim))
    return jnp.concatenate(cols, axis=1).astype(BF16)


def _rot_half_cols(w):
    half = HEAD_DIM // 2
    return jnp.concatenate([-w[..., half:], w[..., :half]], axis=-1)


def _overlap_matrix(n_cmp_pad, n_blk):
    j = np.arange(n_cmp_pad)[:, None]
    s = np.arange(LANES)[None, :]
    lo = np.maximum(j * CMP_STRIDE, s * SEL_BLOCK)
    hi = np.minimum(j * CMP_STRIDE + CMP_BLOCK, (s + 1) * SEL_BLOCK)
    return jnp.asarray(np.clip(hi - lo, 0, None) / CMP_BLOCK, dtype=BF16)


def kernel(x, norm_mix_0, nsa_w_in, cmp_k_pos, cmp_k_w1, cmp_k_b1, cmp_k_w2, cmp_k_b2, cmp_v_pos, cmp_v_w1, cmp_v_b1, cmp_v_w2, cmp_v_b2, nsa_w_out, norm_ffn_0, ffn_up_0, ffn_conv_w_0, ffn_conv_b_0, ffn_down_0, norm_mix_1, pool_w, pool_b, pool_scale, norm_ffn_1, ffn_up_1, ffn_conv_w_1, ffn_conv_b_1, ffn_down_1, norm_final):
    B, T, D = x.shape
    assert D == N_HEADS * HEAD_DIM and SEL_BLOCK == 64
    assert T % TM_FFN == 0 and T % TK_SEL == 0 and TK_SEL % TQ == 0 and T >= WINDOW + TQ
    n_cmp_pad = T // CMP_STRIDE
    n_blk = T // SEL_BLOCK
    assert n_blk <= LANES
    row = lambda v: v.reshape(1, -1)

    cos, sin = _rope_tables(jnp.arange(T))
    cos4 = jnp.tile(cos, (1, 4))
    sin4 = jnp.tile(jnp.concatenate([-sin, sin], axis=1), (1, 2))
    q, qsw, ks, kw, kc, vc, vs, vw, gates = _in_proj(
        x, row(norm_mix_0), _inproj_weight(nsa_w_in), cos4, sin4)

    ccos, csin = _rope_tables(jnp.arange(n_cmp_pad) * CMP_STRIDE + (CMP_BLOCK - 1))
    zeros64 = jnp.zeros((n_cmp_pad, HEAD_DIM), F32)
    ccos2 = jnp.concatenate([ccos, ccos, zeros64], axis=1)
    csin2 = jnp.concatenate([csin, csin, zeros64], axis=1)
    pos_rows = lambda p: jnp.broadcast_to(p.reshape(1, -1), (SUBLANES, p.size)).astype(BF16)
    zero_w2 = jnp.zeros_like(cmp_v_w2)
    kparams = (_expand_w1(cmp_k_w1), cmp_k_w1.astype(BF16), pos_rows(cmp_k_pos), row(cmp_k_b1),
               jnp.concatenate([cmp_k_w2, _rot_half_cols(cmp_k_w2)], axis=1).astype(BF16),
               row(jnp.concatenate([cmp_k_b2, _rot_half_cols(cmp_k_b2)])))
    vparams = (_expand_w1(cmp_v_w1), cmp_v_w1.astype(BF16), pos_rows(cmp_v_pos), row(cmp_v_b1),
               jnp.concatenate([cmp_v_w2, zero_w2], axis=1).astype(BF16),
               jnp.concatenate([zero_w2, cmp_v_w2], axis=1).astype(BF16),
               row(jnp.concatenate([cmp_v_b2, cmp_v_b2])))
    width = CMP_STRIDE * N_GROUPS * HEAD_DIM
    kco, vco = _compress(kc.reshape(B, n_cmp_pad, width), vc.reshape(B, n_cmp_pad, width),
                         kparams, vparams, ccos2, csin2)

    o = _nsa_attn(q, qsw, ks, kw, vs, vw, kco, vco, gates, _overlap_matrix(n_cmp_pad, n_blk))
    x2 = _out_proj(x.reshape(B * T, D), o.reshape(B * T, D), nsa_w_out.astype(BF16))

    x2 = _conv_ffn(x2, T, row(norm_ffn_0), ffn_up_0.astype(BF16), ffn_conv_w_0, row(ffn_conv_b_0),
                   ffn_down_0.astype(BF16), row(norm_final), False)
    x2 = _pool_mix(x2, T, row(norm_mix_1), pool_w.astype(BF16), row(pool_b.reshape(-1)), row(pool_scale))
    x2 = _conv_ffn(x2, T, row(norm_ffn_1), ffn_up_1.astype(BF16), ffn_conv_w_1, row(ffn_conv_b_1),
                   ffn_down_1.astype(BF16), row(norm_final), True)
    return x2.reshape(B, T, D)
```

```python
import functools

import numpy as np
import jax
import jax.numpy as jnp
from jax import lax
from jax.experimental import pallas as pl
from jax.experimental.pallas import tpu as pltpu

F32 = jnp.float32
BF16 = jnp.bfloat16

N_HEADS = 16
HEAD_DIM = 64
N_GROUPS = 2
HEADS_PER_GROUP = N_HEADS // N_GROUPS
N_BRANCH = 3
CMP_STRIDE = 16
CMP_BLOCK = 32
SEL_BLOCK = 64
SEL_TOPK = 16
WINDOW = 512
ROPE_THETA = 10000.0
POOL_WINDOWS = (2, 4, 8, 16)
CONV_WIDTH = 3
NORM_EPS = 1e-6
NEG_INF = -1e30
FORCE_BONUS = 1e4

LANES = 128
SUBLANES = 8
VMEM_LIMIT = 56 * 1024 * 1024

TM_PROJ = 512
TQ = 128
TK_SEL = 512
TM_FFN = 1024
TF_FFN = 256
TM_POOL = 512
HALO_FFN = 8
HALO_POOL = 16


def _rmsnorm(x, g):
    return x * lax.rsqrt(jnp.mean(x * x, axis=-1, keepdims=True) + NORM_EPS) * g


def _nt_dot(a, b):
    return lax.dot_general(a, b, (((1,), (1,)), ((), ())), preferred_element_type=F32)


def _dot(a, b):
    return jnp.dot(a, b, preferred_element_type=F32)


def _inproj_body(x_ref, g_ref, w_ref, cos_ref, sin_ref,
                 q_ref, qsw_ref, ks_ref, kw_ref, kc_ref, vc_ref, vs_ref, vw_ref, gate_ref, *, tm):
    i = pl.program_id(1)
    h = _rmsnorm(x_ref[...], g_ref[...]).astype(BF16)
    res = _dot(h, w_ref[...])
    cos = cos_ref[...]
    sin = sin_ref[...]
    lane = lax.broadcasted_iota(jnp.int32, (tm, LANES), 1)
    first_half = (lane & (HEAD_DIM - 1)) < (HEAD_DIM // 2)

    def rope(xc):
        sw = jnp.where(first_half, pltpu.roll(xc, LANES - HEAD_DIM // 2, 1),
                       pltpu.roll(xc, HEAD_DIM // 2, 1))
        return xc * cos + sw * sin

    scale = HEAD_DIM ** -0.5
    n_pairs = N_HEADS // 2
    for p in range(n_pairs):
        qc = rope(res[:, p * LANES:(p + 1) * LANES]) * scale
        q_ref[:, p * LANES:(p + 1) * LANES] = qc.astype(BF16)
        qsw_ref[:, p * LANES:(p + 1) * LANES] = pltpu.roll(qc, HEAD_DIM, 1).astype(BF16)
    off = n_pairs * LANES
    row_t = i * tm + lax.broadcasted_iota(jnp.int32, (tm, LANES), 0)
    onehot = jnp.where(lane == (row_t >> 6), 1.0, 0.0).astype(BF16)
    for g in range(N_GROUPS):
        ks_ref[g, :, 0:LANES] = onehot
        ks_ref[g, :, LANES:2 * LANES] = rope(res[:, off + g * LANES: off + (g + 1) * LANES]).astype(BF16)
    off += N_GROUPS * LANES
    for g in range(N_GROUPS):
        kw_ref[g] = rope(res[:, off + g * LANES: off + (g + 1) * LANES]).astype(BF16)
    off += N_GROUPS * LANES
    kc_ref[...] = res[:, off:off + LANES]
    vc_ref[...] = res[:, off + LANES:off + 2 * LANES]
    vs_ref[...] = res[:, off + 2 * LANES:off + 3 * LANES].astype(BF16)
    vw_ref[...] = res[:, off + 3 * LANES:off + 4 * LANES].astype(BF16)
    off += 4 * LANES
    for g in range(N_GROUPS):
        z = res[:, off + g * LANES: off + (g + 1) * LANES]
        gate_ref[g] = 1.0 / (1.0 + jnp.exp(-z))


def _in_proj(x, g, wp, cos4, sin4):
    B, T, D = x.shape
    tm = TM_PROJ
    ncols = wp.shape[1]
    grid = (B, T // tm)
    tok = lambda last: pl.BlockSpec((None, tm, last), lambda b, i: (b, i, 0))
    grp = lambda last: pl.BlockSpec((None, N_GROUPS, tm, last), lambda b, i: (b, 0, i, 0))
    out_shape = (
        jax.ShapeDtypeStruct((B, T, N_HEADS * HEAD_DIM), BF16),
        jax.ShapeDtypeStruct((B, T, N_HEADS * HEAD_DIM), BF16),
        jax.ShapeDtypeStruct((B, N_GROUPS, T, 2 * LANES), BF16),
        jax.ShapeDtypeStruct((B, N_GROUPS, T, LANES), BF16),
        jax.ShapeDtypeStruct((B, T, LANES), F32),
        jax.ShapeDtypeStruct((B, T, LANES), F32),
        jax.ShapeDtypeStruct((B, T, LANES), BF16),
        jax.ShapeDtypeStruct((B, T, LANES), BF16),
        jax.ShapeDtypeStruct((B, N_GROUPS, T, LANES), F32),
    )
    return pl.pallas_call(
        functools.partial(_inproj_body, tm=tm),
        grid=grid,
        in_specs=[tok(D),
                  pl.BlockSpec((1, D), lambda b, i: (0, 0)),
                  pl.BlockSpec((D, ncols), lambda b, i: (0, 0)),
                  pl.BlockSpec((tm, LANES), lambda b, i: (i, 0)),
                  pl.BlockSpec((tm, LANES), lambda b, i: (i, 0))],
        out_specs=(tok(N_HEADS * HEAD_DIM), tok(N_HEADS * HEAD_DIM), grp(2 * LANES), grp(LANES),
                   tok(LANES), tok(LANES), tok(LANES), tok(LANES), grp(LANES)),
        out_shape=out_shape,
        compiler_params=pltpu.CompilerParams(dimension_semantics=("parallel", "parallel"),
                                             vmem_limit_bytes=VMEM_LIMIT),
        name="in_proj",
    )(x, g, wp, cos4, sin4)


def _gelu_tanh(x):
    return 0.5 * x * (1.0 + jnp.tanh(np.sqrt(2.0 / np.pi).astype(np.float32) * (x + 0.044715 * (x * x * x))))


def _compress_body(kc_ref, vc_ref,
                   kw1e_ref, kw1_ref, kpos_ref, kb1_ref, kw2_ref, kb2_ref,
                   vw1e_ref, vw1_ref, vpos_ref, vb1_ref, vw2a_ref, vw2b_ref, vb2_ref,
                   cos_ref, sin_ref, kco_ref, vco_ref, *, n_rows, hidden):
    row = lax.broadcasted_iota(jnp.int32, (n_rows, LANES), 0)
    keep = row < (n_rows - 1)

    def hidden_act(c_ref, w1e_ref, w1_ref, pos_ref, b1_ref):
        c2 = c_ref[...].astype(BF16)
        ab = _dot(c2, w1e_ref[...])
        posb = _dot(pos_ref[...], w1_ref[...])[0:1, :] + b1_ref[...]
        outs = []
        for g in range(N_GROUPS):
            a = ab[:, g * hidden:(g + 1) * hidden]
            b = ab[:, (N_GROUPS + g) * hidden:(N_GROUPS + g + 1) * hidden]
            pre = a + pltpu.roll(b, n_rows - 1, 0) + posb
            outs.append(_gelu_tanh(pre).astype(BF16))
        return outs

    hk = hidden_act(kc_ref, kw1e_ref, kw1_ref, kpos_ref, kb1_ref)
    cos = cos_ref[...]
    sin = sin_ref[...]
    for g in range(N_GROUPS):
        r = _dot(hk[g], kw2_ref[...]) + kb2_ref[...]
        kco_ref[g] = jnp.where(keep, r * cos + pltpu.roll(r, HEAD_DIM, 1) * sin, 0.0).astype(BF16)
    hv = hidden_act(vc_ref, vw1e_ref, vw1_ref, vpos_ref, vb1_ref)
    v = _dot(hv[0], vw2a_ref[...]) + _dot(hv[1], vw2b_ref[...]) + vb2_ref[...]
    vco_ref[...] = jnp.where(keep, v, 0.0).astype(BF16)


def _compress(kc2, vc2, kparams, vparams, ccos, csin):
    B, n_rows, width = kc2.shape
    hidden = kparams[3].shape[1]
    full = lambda a: pl.BlockSpec(a.shape, lambda b: (0,) * a.ndim)
    blk = pl.BlockSpec((None, n_rows, width), lambda b: (b, 0, 0))
    weights = list(kparams) + list(vparams) + [ccos, csin]
    return pl.pallas_call(
        functools.partial(_compress_body, n_rows=n_rows, hidden=hidden),
        grid=(B,),
        in_specs=[blk, blk] + [full(a) for a in weights],
        out_specs=(pl.BlockSpec((None, N_GROUPS, n_rows, LANES), lambda b: (b, 0, 0, 0)),
                   pl.BlockSpec((None, n_rows, LANES), lambda b: (b, 0, 0))),
        out_shape=(jax.ShapeDtypeStruct((B, N_GROUPS, n_rows, LANES), BF16),
                   jax.ShapeDtypeStruct((B, n_rows, LANES), BF16)),
        compiler_params=pltpu.CompilerParams(dimension_semantics=("parallel",),
                                             vmem_limit_bytes=VMEM_LIMIT),
        name="compress",
    )(kc2, vc2, *weights)


def _softmax_rows(s):
    m = jnp.max(s, axis=1, keepdims=True)
    e = jnp.exp(s - m)
    return e, jnp.sum(e, axis=1, keepdims=True)


def _attn_body(q_ref, qsw_ref, ks_ref, kw_ref, vs_ref, vw_ref, kc_ref, vc_ref, gate_ref, ov_ref,
               o_ref, qaug_ref, s_ref, m_ref, l_ref, acc_ref, *, tq, tk, n_cmp_pad, n_blk):
    hg = HEADS_PER_GROUP
    g = pl.program_id(1)
    i = pl.program_id(2)
    s0 = i * tq
    t_col = s0 + lax.broadcasted_iota(jnp.int32, (tq, 1), 0)

    n_pairs = hg // 2
    rc = 2 * tq

    def qpair(h):
        src = q_ref if h % 2 == 0 else qsw_ref
        return src[:, (h // 2) * LANES:(h // 2 + 1) * LANES]

    def q_rows(p):
        return jnp.concatenate([qpair(2 * p), qpair(2 * p + 1)], axis=0)

    j_row = lax.broadcasted_iota(jnp.int32, (1, n_cmp_pad), 1)
    bias_c = jnp.where(j_row * CMP_STRIDE + (CMP_BLOCK - 1) <= t_col, 0.0, NEG_INF)
    vis = jnp.where(t_col >= CMP_BLOCK - 1, 1.0, 0.0)
    psum = jnp.zeros((tq, n_cmp_pad), F32)
    oc = []
    for p in range(n_pairs):
        sc = _nt_dot(q_rows(p), kc_ref[...])
        parts = []
        for hh in range(2):
            e, l = _softmax_rows(sc[hh * tq:(hh + 1) * tq] + bias_c)
            pn = e * (vis / l)
            psum = psum + pn
            parts.append(pn.astype(BF16))
        oc.append(_dot(jnp.concatenate(parts, axis=0), vc_ref[...]))

    hi = psum.astype(BF16)
    r1 = psum - hi.astype(F32)
    mid = r1.astype(BF16)
    lo = (r1 - mid.astype(F32)).astype(BF16)
    ov = ov_ref[...]
    imp = _dot(hi, ov) + _dot(mid, ov) + _dot(lo, ov)
    blk = lax.broadcasted_iota(jnp.int32, (tq, LANES), 1)
    cur = t_col >> 6
    forced = (blk == 0) | (blk == cur) | (blk == cur - 1)
    score = jnp.where(blk <= cur, imp + jnp.where(forced, FORCE_BONUS, 0.0), NEG_INF)
    st = score.T
    blk_t = lax.broadcasted_iota(jnp.int32, (LANES, tq), 0).astype(F32)
    sel_t = jnp.zeros((LANES, tq), F32)
    for _ in range(min(SEL_TOPK, n_blk)):
        mx = jnp.max(st, axis=0, keepdims=True)
        idx = jnp.min(jnp.where(st == mx, blk_t, float(LANES)), axis=0, keepdims=True)
        hit = blk_t == idx
        sel_t = jnp.where(hit, 1.0, sel_t)
        st = jnp.where(hit, -jnp.inf, st)
    sel = sel_t.T
    bias_s = jnp.where((sel > 0.5) & (blk <= cur), 0.0, NEG_INF).astype(BF16)

    for h in range(hg):
        qaug_ref[h * tq:(h + 1) * tq, 0:LANES] = bias_s
        qaug_ref[h * tq:(h + 1) * tq, LANES:2 * LANES] = qpair(h)
    m_ref[...] = jnp.full(m_ref.shape, NEG_INF, F32)
    l_ref[...] = jnp.zeros(l_ref.shape, F32)
    acc_ref[...] = jnp.zeros(acc_ref.shape, F32)

    n_lc = tk // LANES

    def scores(kt, rows):
        start = pl.multiple_of(kt * tk, tk)
        return _nt_dot(qaug_ref[rows, :], ks_ref[pl.ds(start, tk), :])

    def softmax_pv(kt, causal):
        start = pl.multiple_of(kt * tk, tk)
        if causal:
            kpos = start + lax.broadcasted_iota(jnp.int32, (1, tk), 1)
            bias_d = jnp.where(kpos <= t_col, 0.0, NEG_INF)
        for p in range(n_pairs):
            rows = slice(p * rc, (p + 1) * rc)
            s = s_ref[rows, :]
            if causal:
                s = jnp.concatenate([s[hh * tq:(hh + 1) * tq] + bias_d for hh in range(2)], axis=0)
            chunks = [s[:, c * LANES:(c + 1) * LANES] for c in range(n_lc)]
            mx = functools.reduce(jnp.maximum, chunks)
            m_old = m_ref[rows, :]
            m_new = jnp.maximum(m_old, jnp.max(mx, axis=1, keepdims=True))
            alpha = jnp.exp(m_old - m_new)
            ps = [jnp.exp(c - m_new) for c in chunks]
            l_ref[rows, :] = alpha * l_ref[rows, :] + functools.reduce(jnp.add, ps)
            pv = _dot(jnp.concatenate(ps, axis=1).astype(BF16), vs_ref[pl.ds(start, tk), :])
            acc_ref[rows, :] = alpha * acc_ref[rows, :] + pv
            m_ref[rows, :] = m_new
            if not causal:
                s_ref[rows, :] = scores(kt + 1, rows)

    k_diag = s0 // tk
    s_ref[...] = scores(0, slice(None))

    def full_tile(kt, carry):
        softmax_pv(kt, False)
        return carry

    lax.fori_loop(0, k_diag, full_tile, 0)
    softmax_pv(k_diag, True)

    n_win = WINDOW + tq
    kstart = pl.multiple_of(jnp.maximum(s0 - WINDOW, 0), tq)
    kpos_w = kstart + lax.broadcasted_iota(jnp.int32, (1, n_win), 1)
    bias_w = jnp.where((kpos_w <= t_col) & (kpos_w > t_col - WINDOW), 0.0, NEG_INF)
    o_win = []
    linv_w = []
    for p in range(n_pairs):
        sw = _nt_dot(q_rows(p), kw_ref[pl.ds(kstart, n_win), :])
        parts = []
        for hh in range(2):
            e, l = _softmax_rows(sw[hh * tq:(hh + 1) * tq] + bias_w)
            parts.append(e.astype(BF16))
            linv_w.append(1.0 / l)
        o_win.append(_dot(jnp.concatenate(parts, axis=0), vw_ref[pl.ds(kstart, n_win), :]))

    gates = gate_ref[...]
    outs = []
    for h in range(hg):
        p, hh = divmod(h, 2)
        rows = slice(h * tq, (h + 1) * tq)
        sub = slice(hh * tq, (hh + 1) * tq)
        gc = gates[:, N_BRANCH * h + 0:N_BRANCH * h + 1]
        gs = gates[:, N_BRANCH * h + 1:N_BRANCH * h + 2]
        gw = gates[:, N_BRANCH * h + 2:N_BRANCH * h + 3]
        l_sel = jnp.sum(l_ref[rows, :], axis=1, keepdims=True)
        outs.append(gc * oc[p][sub] + (gs / l_sel) * acc_ref[rows, :] + (gw * linv_w[h]) * o_win[p][sub])
    lane = lax.broadcasted_iota(jnp.int32, (tq, LANES), 1)
    low = lane < HEAD_DIM
    is_g0 = g == 0
    for p in range(hg // 2):
        a = outs[2 * p]
        b = outs[2 * p + 1]
        xa = jnp.where(is_g0, a, pltpu.roll(a, HEAD_DIM, 1))
        xb = jnp.where(is_g0, pltpu.roll(b, HEAD_DIM, 1), b)
        o_ref[:, p * LANES:(p + 1) * LANES] = jnp.where(low, xa, xb).astype(BF16)


def _nsa_attn(q, qsw, ks, kw, vs, vw, kco, vco, gates, overlap):
    B, T, _ = q.shape
    tq, tk = TQ, TK_SEL
    n_cmp_pad = kco.shape[2]
    n_blk = T // SEL_BLOCK
    gw = HEADS_PER_GROUP * HEAD_DIM
    rows = HEADS_PER_GROUP * tq
    grid = (B, N_GROUPS, T // tq)
    return pl.pallas_call(
        functools.partial(_attn_body, tq=tq, tk=tk, n_cmp_pad=n_cmp_pad, n_blk=n_blk),
        grid=grid,
        in_specs=[
            pl.BlockSpec((None, tq, gw), lambda b, g, i: (b, i, g)),
            pl.BlockSpec((None, tq, gw), lambda b, g, i: (b, i, g)),
            pl.BlockSpec((None, None, T, 2 * LANES), lambda b, g, i: (b, g, 0, 0)),
            pl.BlockSpec((None, None, T, LANES), lambda b, g, i: (b, g, 0, 0)),
            pl.BlockSpec((None, T, LANES), lambda b, g, i: (b, 0, 0)),
            pl.BlockSpec((None, T, LANES), lambda b, g, i: (b, 0, 0)),
            pl.BlockSpec((None, None, n_cmp_pad, LANES), lambda b, g, i: (b, g, 0, 0)),
            pl.BlockSpec((None, n_cmp_pad, LANES), lambda b, g, i: (b, 0, 0)),
            pl.BlockSpec((None, None, tq, LANES), lambda b, g, i: (b, g, i, 0)),
            pl.BlockSpec(overlap.shape, lambda b, g, i: (0, 0)),
        ],
        out_specs=pl.BlockSpec((None, tq, gw), lambda b, g, i: (b, i, g)),
        out_shape=jax.ShapeDtypeStruct((B, T, N_HEADS * HEAD_DIM), BF16),
        scratch_shapes=[pltpu.VMEM((rows, 2 * LANES), BF16),
                        pltpu.VMEM((rows, tk), F32),
                        pltpu.VMEM((rows, LANES), F32),
                        pltpu.VMEM((rows, LANES), F32),
                        pltpu.VMEM((rows, LANES), F32)],
        compiler_params=pltpu.CompilerParams(
            dimension_semantics=("parallel", "parallel", "arbitrary"),
            vmem_limit_bytes=VMEM_LIMIT),
        name="nsa_attn",
    )(q, qsw, ks, kw, vs, vw, kco, vco, gates, overlap)


def _outproj_body(x_ref, o_ref, w_ref, y_ref):
    y_ref[...] = x_ref[...] + _dot(o_ref[...], w_ref[...])


def _out_proj(x2, o2, w):
    N, D = x2.shape
    tm = TM_PROJ
    return pl.pallas_call(
        _outproj_body,
        grid=(N // tm,),
        in_specs=[pl.BlockSpec((tm, D), lambda i: (i, 0)),
                  pl.BlockSpec((tm, D), lambda i: (i, 0)),
                  pl.BlockSpec(w.shape, lambda i: (0, 0))],
        out_specs=pl.BlockSpec((tm, D), lambda i: (i, 0)),
        out_shape=jax.ShapeDtypeStruct((N, D), F32),
        compiler_params=pltpu.CompilerParams(dimension_semantics=("parallel",),
                                             vmem_limit_bytes=VMEM_LIMIT),
        name="out_proj",
    )(x2, o2, w)


def _ffn_body(x_ref, xh_ref, g_ref, wg_ref, wv_ref, cwg_ref, cwv_ref, cbg_ref, cbv_ref, wd_ref,
              gf_ref, o_ref, h_ref, acc_ref, *, tm, tiles_per_seq, final_norm):
    i = pl.program_id(0)
    j = pl.program_id(1)
    halo = HALO_FFN

    @pl.when(j == 0)
    def _():
        g = g_ref[...]
        hh = _rmsnorm(xh_ref[...], g)
        seq_start = (i % tiles_per_seq) == 0
        h_ref[0:halo, :] = jnp.where(seq_start, 0.0, hh).astype(BF16)
        h_ref[halo:halo + tm, :] = _rmsnorm(x_ref[...], g).astype(BF16)
        acc_ref[...] = jnp.zeros(acc_ref.shape, F32)

    hfull = h_ref[...]

    def conv(w_ref, cw_ref, cb_ref):
        u = _dot(hfull, w_ref[...])
        n = halo + tm
        u1 = pltpu.roll(u, 1, 0)
        u2 = pltpu.roll(u, 2, 0)
        cw = cw_ref[...]
        c = cw[0:1, :] * u2 + cw[1:2, :] * u1 + cw[2:3, :] * u + cb_ref[...]
        return c[halo:n]

    cg = conv(wg_ref, cwg_ref, cbg_ref)
    cv = conv(wv_ref, cwv_ref, cbv_ref)
    a = (cg * (1.0 / (1.0 + jnp.exp(-cg)))) * cv
    acc_ref[...] += _dot(a.astype(BF16), wd_ref[...])

    @pl.when(j == pl.num_programs(1) - 1)
    def _():
        y = x_ref[...] + acc_ref[...]
        if final_norm:
            y = _rmsnorm(y, gf_ref[...])
        o_ref[...] = y


def _conv_ffn(x2, seq_len, g, w_up, conv_w, conv_b, w_down, g_final, final_norm):
    N, D = x2.shape
    dff = w_down.shape[0]
    tm, tf = TM_FFN, TF_FFN
    nj = dff // tf
    halo_blocks = tm // HALO_FFN
    return pl.pallas_call(
        functools.partial(_ffn_body, tm=tm, tiles_per_seq=seq_len // tm, final_norm=final_norm),
        grid=(N // tm, nj),
        in_specs=[
            pl.BlockSpec((tm, D), lambda i, j: (i, 0)),
            pl.BlockSpec((HALO_FFN, D), lambda i, j: (jnp.maximum(i * halo_blocks - 1, 0), 0)),
            pl.BlockSpec((1, D), lambda i, j: (0, 0)),
            pl.BlockSpec((D, tf), lambda i, j: (0, j)),
            pl.BlockSpec((D, tf), lambda i, j: (0, nj + j)),
            pl.BlockSpec((CONV_WIDTH, tf), lambda i, j: (0, j)),
            pl.BlockSpec((CONV_WIDTH, tf), lambda i, j: (0, nj + j)),
            pl.BlockSpec((1, tf), lambda i, j: (0, j)),
            pl.BlockSpec((1, tf), lambda i, j: (0, nj + j)),
            pl.BlockSpec((tf, D), lambda i, j: (j, 0)),
            pl.BlockSpec((1, D), lambda i, j: (0, 0)),
        ],
        out_specs=pl.BlockSpec((tm, D), lambda i, j: (i, 0)),
        out_shape=jax.ShapeDtypeStruct((N, D), F32),
        scratch_shapes=[pltpu.VMEM((HALO_FFN + tm, D), BF16),
                        pltpu.VMEM((tm, D), F32)],
        compiler_params=pltpu.CompilerParams(dimension_semantics=("parallel", "arbitrary"),
                                             vmem_limit_bytes=VMEM_LIMIT),
        name="conv_ffn",
    )(x2, x2, g, w_up, w_up, conv_w, conv_w, conv_b, conv_b, w_down, g_final)


def _pool_body(x_ref, xh_ref, g_ref, w_ref, b_ref, s_ref, o_ref, *, tm, tiles_per_seq, gdim):
    i = pl.program_id(0)
    halo = HALO_POOL
    g = g_ref[...]
    x = x_ref[...]
    h = _rmsnorm(x, g)
    seq_tile = i % tiles_per_seq
    hh = jnp.where(seq_tile == 0, 0.0, _rmsnorm(xh_ref[...], g))
    hext = jnp.concatenate([hh, h], axis=0)
    t_seq = seq_tile * tm + lax.broadcasted_iota(jnp.int32, (tm, 1), 0)
    ys = []
    for gi, w in enumerate(POOL_WINDOWS):
        cols = slice(gi * gdim, (gi + 1) * gdim)
        s = hext[:, cols]
        shift = 1
        while shift < w:
            s = s + pltpu.roll(s, shift, 0)
            shift *= 2
        cnt = jnp.minimum(t_seq + 1, w).astype(F32)
        pooled = s[halo:halo + tm] / cnt - h[:, cols]
        ys.append(_dot(pooled.astype(BF16), w_ref[gi]))
    y = jnp.concatenate(ys, axis=1) + b_ref[...]
    o_ref[...] = x + y * s_ref[...]


def _pool_mix(x2, seq_len, g, pool_w, pool_b, pool_scale):
    N, D = x2.shape
    tm = TM_POOL
    gdim = D // len(POOL_WINDOWS)
    halo_blocks = tm // HALO_POOL
    return pl.pallas_call(
        functools.partial(_pool_body, tm=tm, tiles_per_seq=seq_len // tm, gdim=gdim),
        grid=(N // tm,),
        in_specs=[
            pl.BlockSpec((tm, D), lambda i: (i, 0)),
            pl.BlockSpec((HALO_POOL, D), lambda i: (jnp.maximum(i * halo_blocks - 1, 0), 0)),
            pl.BlockSpec((1, D), lambda i: (0, 0)),
            pl.BlockSpec(pool_w.shape, lambda i: (0, 0, 0)),
            pl.BlockSpec((1, D), lambda i: (0, 0)),
            pl.BlockSpec((1, D), lambda i: (0, 0)),
        ],
        out_specs=pl.BlockSpec((tm, D), lambda i: (i, 0)),
        out_shape=jax.ShapeDtypeStruct((N, D), F32),
        compiler_params=pltpu.CompilerParams(dimension_semantics=("parallel",),
                                             vmem_limit_bytes=VMEM_LIMIT),
        name="pool_mix",
    )(x2, x2, g, pool_w, pool_b, pool_scale)


def _pad_cols(w, width):
    return jnp.pad(w, ((0, 0), (0, width - w.shape[1])))


def _inproj_weight(w_in):
    D = w_in.shape[0]
    kv = N_GROUPS * HEAD_DIM
    sizes = [N_HEADS * HEAD_DIM] + [kv] * 6 + [N_BRANCH * N_HEADS]
    offs = np.concatenate([[0], np.cumsum(sizes)])
    q, k_c, v_c, k_s, v_s, k_w, v_w, gt = [w_in[:, offs[n]:offs[n + 1]] for n in range(8)]
    per_group = lambda w, n: [_pad_cols(w[:, g * n:(g + 1) * n], LANES) for g in range(N_GROUPS)]
    cols = ([q] + per_group(k_s, HEAD_DIM) + per_group(k_w, HEAD_DIM) + [k_c, v_c, v_s, v_w]
            + per_group(gt, N_BRANCH * HEADS_PER_GROUP))
    return jnp.concatenate(cols, axis=1).astype(BF16)


def _rope_tables(pos):
    inv = 1.0 / (ROPE_THETA ** (jnp.arange(0, HEAD_DIM, 2, dtype=F32) / HEAD_DIM))
    ang = pos.astype(F32)[:, None] * inv[None, :]
    return jnp.cos(ang), jnp.sin(ang)


def _expand_w1(w1):
    hdim = w1.shape[1]
    halves = w1.reshape(2, CMP_STRIDE, HEAD_DIM, hdim)
    zeros = jnp.zeros((CMP_STRIDE, HEAD_DIM, hdim), w1.dtype)
    cols = []
    for half in range(2):
        for g in range(N_GROUPS):
            parts = [halves[half] if gg == g else zeros for gg in range(N_GROUPS)]
            cols.append(jnp.concatenate(parts, axis=1).reshape(CMP_STRIDE * N_GROUPS * HEAD_DIM, hdim))
    return jnp.concatenate(cols, axis=1).astype(BF16)


def _rot_half_cols(w):
    half = HEAD_DIM // 2
    return jnp.concatenate([-w[..., half:], w[..., :half]], axis=-1)


def _overlap_matrix(n_cmp_pad, n_blk):
    j = np.arange(n_cmp_pad)[:, None]
    s = np.arange(LANES)[None, :]
    lo = np.maximum(j * CMP_STRIDE, s * SEL_BLOCK)
    hi = np.minimum(j * CMP_STRIDE + CMP_BLOCK, (s + 1) * SEL_BLOCK)
    return jnp.asarray(np.clip(hi - lo, 0, None) / CMP_BLOCK, dtype=BF16)


def kernel(x, norm_mix_0, nsa_w_in, cmp_k_pos, cmp_k_w1, cmp_k_b1, cmp_k_w2, cmp_k_b2, cmp_v_pos, cmp_v_w1, cmp_v_b1, cmp_v_w2, cmp_v_b2, nsa_w_out, norm_ffn_0, ffn_up_0, ffn_conv_w_0, ffn_conv_b_0, ffn_down_0, norm_mix_1, pool_w, pool_b, pool_scale, norm_ffn_1, ffn_up_1, ffn_conv_w_1, ffn_conv_b_1, ffn_down_1, norm_final):
    B, T, D = x.shape
    assert D == N_HEADS * HEAD_DIM and SEL_BLOCK == 64
    assert T % TM_FFN == 0 and T % TK_SEL == 0 and TK_SEL % TQ == 0 and T >= WINDOW + TQ
    n_cmp_pad = T // CMP_STRIDE
    n_blk = T // SEL_BLOCK
    assert n_blk <= LANES
    row = lambda v: v.reshape(1, -1)

    cos, sin = _rope_tables(jnp.arange(T))
    cos4 = jnp.tile(cos, (1, 4))
    sin4 = jnp.tile(jnp.concatenate([-sin, sin], axis=1), (1, 2))
    q, qsw, ks, kw, kc, vc, vs, vw, gates = _in_proj(
        x, row(norm_mix_0), _inproj_weight(nsa_w_in), cos4, sin4)

    ccos, csin = _rope_tables(jnp.arange(n_cmp_pad) * CMP_STRIDE + (CMP_BLOCK - 1))
    zeros64 = jnp.zeros((n_cmp_pad, HEAD_DIM), F32)
    ccos2 = jnp.concatenate([ccos, ccos, zeros64], axis=1)
    csin2 = jnp.concatenate([csin, csin, zeros64], axis=1)
    pos_rows = lambda p: jnp.broadcast_to(p.reshape(1, -1), (SUBLANES, p.size)).astype(BF16)
    zero_w2 = jnp.zeros_like(cmp_v_w2)
    kparams = (_expand_w1(cmp_k_w1), cmp_k_w1.astype(BF16), pos_rows(cmp_k_pos), row(cmp_k_b1),
               jnp.concatenate([cmp_k_w2, _rot_half_cols(cmp_k_w2)], axis=1).astype(BF16),
               row(jnp.concatenate([cmp_k_b2, _rot_half_cols(cmp_k_b2)])))
    vparams = (_expand_w1(cmp_v_w1), cmp_v_w1.astype(BF16), pos_rows(cmp_v_pos), row(cmp_v_b1),
               jnp.concatenate([cmp_v_w2, zero_w2], axis=1).astype(BF16),
               jnp.concatenate([zero_w2, cmp_v_w2], axis=1).astype(BF16),
               row(jnp.concatenate([cmp_v_b2, cmp_v_b2])))
    width = CMP_STRIDE * N_GROUPS * HEAD_DIM
    kco, vco = _compress(kc.reshape(B, n_cmp_pad, width), vc.reshape(B, n_cmp_pad, width),
                         kparams, vparams, ccos2, csin2)

    o = _nsa_attn(q, qsw, ks, kw, vs, vw, kco, vco, gates, _overlap_matrix(n_cmp_pad, n_blk))
    x2 = _out_proj(x.reshape(B * T, D), o.reshape(B * T, D), nsa_w_out.astype(BF16))

    x2 = _conv_ffn(x2, T, row(norm_ffn_0), ffn_up_0.astype(BF16), ffn_conv_w_0, row(ffn_conv_b_0),
                   ffn_down_0.astype(BF16), row(norm_final), False)
    x2 = _pool_mix(x2, T, row(norm_mix_1), pool_w.astype(BF16), row(pool_b.reshape(-1)), row(pool_scale))
    x2 = _conv_ffn(x2, T, row(norm_ffn_1), ffn_up_1.astype(BF16), ffn_conv_w_1, row(ffn_conv_b_1),
                   ffn_down_1.astype(BF16), row(norm_final), True)
    return x2.reshape(B, T, D)
```

```python
import functools

import numpy as np
import jax
import jax.numpy as jnp
from jax import lax
from jax.experimental import pallas as pl
from jax.experimental.pallas import tpu as pltpu

F32 = jnp.float32
BF16 = jnp.bfloat16

N_HEADS = 16
HEAD_DIM = 64
N_GROUPS = 2
HEADS_PER_GROUP = N_HEADS // N_GROUPS
N_BRANCH = 3
CMP_STRIDE = 16
CMP_BLOCK = 32
SEL_BLOCK = 64
SEL_TOPK = 16
WINDOW = 512
ROPE_THETA = 10000.0
POOL_WINDOWS = (2, 4, 8, 16)
CONV_WIDTH = 3
NORM_EPS = 1e-6
NEG_INF = -1e30
FORCE_BONUS = 1e4
LOG2_E = 1.4426950408889634

LANES = 128
SUBLANES = 8
VMEM_LIMIT = 56 * 1024 * 1024

TM_PROJ = 512
TQ = 128
TK_SEL = 512
TM_FFN = 512
TF_FFN = 256
FFN_ROW_BLOCK = 256
TM_POOL = 512
HALO_FFN = 8
HALO_POOL = 16


def _rmsnorm(x, g):
    return x * lax.rsqrt(jnp.mean(x * x, axis=-1, keepdims=True) + NORM_EPS) * g


def _nt_dot(a, b):
    return lax.dot_general(a, b, (((1,), (1,)), ((), ())), preferred_element_type=F32)


def _dot(a, b):
    return jnp.dot(a, b, preferred_element_type=F32)


def _inproj_body(x_ref, g_ref, w_ref, cos_ref, sin_ref,
                 q_ref, qsw_ref, ks_ref, kw_ref, kc_ref, vc_ref, vs_ref, vw_ref, gate_ref, *, tm):
    i = pl.program_id(1)
    h = _rmsnorm(x_ref[...], g_ref[...]).astype(BF16)
    res = _dot(h, w_ref[...])
    cos = cos_ref[...]
    sin = sin_ref[...]
    lane = lax.broadcasted_iota(jnp.int32, (tm, LANES), 1)
    first_half = (lane & (HEAD_DIM - 1)) < (HEAD_DIM // 2)

    def rope(xc):
        sw = jnp.where(first_half, pltpu.roll(xc, LANES - HEAD_DIM // 2, 1),
                       pltpu.roll(xc, HEAD_DIM // 2, 1))
        return xc * cos + sw * sin

    scale = HEAD_DIM ** -0.5 * LOG2_E
    n_pairs = N_HEADS // 2
    for p in range(n_pairs):
        qc = rope(res[:, p * LANES:(p + 1) * LANES]) * scale
        q_ref[:, p * LANES:(p + 1) * LANES] = qc.astype(BF16)
        qsw_ref[:, p * LANES:(p + 1) * LANES] = pltpu.roll(qc, HEAD_DIM, 1).astype(BF16)
    off = n_pairs * LANES
    row_t = i * tm + lax.broadcasted_iota(jnp.int32, (tm, LANES), 0)
    onehot = jnp.where(lane == (row_t >> 6), 1.0, 0.0).astype(BF16)
    for g in range(N_GROUPS):
        ks_ref[g, :, 0:LANES] = onehot
        ks_ref[g, :, LANES:2 * LANES] = rope(res[:, off + g * LANES: off + (g + 1) * LANES]).astype(BF16)
    off += N_GROUPS * LANES
    for g in range(N_GROUPS):
        kw_ref[g] = rope(res[:, off + g * LANES: off + (g + 1) * LANES]).astype(BF16)
    off += N_GROUPS * LANES
    kc_ref[...] = res[:, off:off + LANES]
    vc_ref[...] = res[:, off + LANES:off + 2 * LANES]
    ones = jnp.ones((tm, LANES), BF16)
    vs_ref[:, 0:LANES] = res[:, off + 2 * LANES:off + 3 * LANES].astype(BF16)
    vs_ref[:, LANES:2 * LANES] = ones
    vw_ref[:, 0:LANES] = res[:, off + 3 * LANES:off + 4 * LANES].astype(BF16)
    vw_ref[:, LANES:2 * LANES] = ones
    off += 4 * LANES
    for g in range(N_GROUPS):
        z = res[:, off + g * LANES: off + (g + 1) * LANES]
        gate_ref[g] = 1.0 / (1.0 + jnp.exp(-z))


def _in_proj(x, g, wp, cos4, sin4):
    B, T, D = x.shape
    tm = TM_PROJ
    ncols = wp.shape[1]
    grid = (B, T // tm)
    tok = lambda last: pl.BlockSpec((None, tm, last), lambda b, i: (b, i, 0))
    grp = lambda last: pl.BlockSpec((None, N_GROUPS, tm, last), lambda b, i: (b, 0, i, 0))
    out_shape = (
        jax.ShapeDtypeStruct((B, T, N_HEADS * HEAD_DIM), BF16),
        jax.ShapeDtypeStruct((B, T, N_HEADS * HEAD_DIM), BF16),
        jax.ShapeDtypeStruct((B, N_GROUPS, T, 2 * LANES), BF16),
        jax.ShapeDtypeStruct((B, N_GROUPS, T, LANES), BF16),
        jax.ShapeDtypeStruct((B, T, LANES), F32),
        jax.ShapeDtypeStruct((B, T, LANES), F32),
        jax.ShapeDtypeStruct((B, T, 2 * LANES), BF16),
        jax.ShapeDtypeStruct((B, T, 2 * LANES), BF16),
        jax.ShapeDtypeStruct((B, N_GROUPS, T, LANES), F32),
    )
    return pl.pallas_call(
        functools.partial(_inproj_body, tm=tm),
        grid=grid,
        in_specs=[tok(D),
                  pl.BlockSpec((1, D), lambda b, i: (0, 0)),
                  pl.BlockSpec((D, ncols), lambda b, i: (0, 0)),
                  pl.BlockSpec((tm, LANES), lambda b, i: (i, 0)),
                  pl.BlockSpec((tm, LANES), lambda b, i: (i, 0))],
        out_specs=(tok(N_HEADS * HEAD_DIM), tok(N_HEADS * HEAD_DIM), grp(2 * LANES), grp(LANES),
                   tok(LANES), tok(LANES), tok(2 * LANES), tok(2 * LANES), grp(LANES)),
        out_shape=out_shape,
        compiler_params=pltpu.CompilerParams(dimension_semantics=("parallel", "parallel"),
                                             vmem_limit_bytes=VMEM_LIMIT),
        name="in_proj",
    )(x, g, wp, cos4, sin4)


def _gelu_tanh(x):
    return 0.5 * x * (1.0 + jnp.tanh(np.sqrt(2.0 / np.pi).astype(np.float32) * (x + 0.044715 * (x * x * x))))


def _compress_body(kc_ref, vc_ref,
                   kw1e_ref, kw1_ref, kpos_ref, kb1_ref, kw2_ref, kb2_ref,
                   vw1e_ref, vw1_ref, vpos_ref, vb1_ref, vw2a_ref, vw2b_ref, vb2_ref,
                   cos_ref, sin_ref, kco_ref, vco_ref, *, n_rows, hidden):
    row = lax.broadcasted_iota(jnp.int32, (n_rows, LANES), 0)
    keep = row < (n_rows - 1)

    def hidden_act(c_ref, w1e_ref, w1_ref, pos_ref, b1_ref):
        c2 = c_ref[...].astype(BF16)
        ab = _dot(c2, w1e_ref[...])
        posb = _dot(pos_ref[...], w1_ref[...])[0:1, :] + b1_ref[...]
        outs = []
        for g in range(N_GROUPS):
            a = ab[:, g * hidden:(g + 1) * hidden]
            b = ab[:, (N_GROUPS + g) * hidden:(N_GROUPS + g + 1) * hidden]
            pre = a + pltpu.roll(b, n_rows - 1, 0) + posb
            outs.append(_gelu_tanh(pre).astype(BF16))
        return outs

    hk = hidden_act(kc_ref, kw1e_ref, kw1_ref, kpos_ref, kb1_ref)
    cos = cos_ref[...]
    sin = sin_ref[...]
    for g in range(N_GROUPS):
        r = _dot(hk[g], kw2_ref[...]) + kb2_ref[...]
        kco_ref[g] = jnp.where(keep, r * cos + pltpu.roll(r, HEAD_DIM, 1) * sin, 0.0).astype(BF16)
    hv = hidden_act(vc_ref, vw1e_ref, vw1_ref, vpos_ref, vb1_ref)
    v = _dot(hv[0], vw2a_ref[...]) + _dot(hv[1], vw2b_ref[...]) + vb2_ref[...]
    vco_ref[...] = jnp.where(keep, v, 0.0).astype(BF16)


def _compress(kc2, vc2, kparams, vparams, ccos, csin):
    B, n_rows, width = kc2.shape
    hidden = kparams[3].shape[1]
    full = lambda a: pl.BlockSpec(a.shape, lambda b: (0,) * a.ndim)
    blk = pl.BlockSpec((None, n_rows, width), lambda b: (b, 0, 0))
    weights = list(kparams) + list(vparams) + [ccos, csin]
    return pl.pallas_call(
        functools.partial(_compress_body, n_rows=n_rows, hidden=hidden),
        grid=(B,),
        in_specs=[blk, blk] + [full(a) for a in weights],
        out_specs=(pl.BlockSpec((None, N_GROUPS, n_rows, LANES), lambda b: (b, 0, 0, 0)),
                   pl.BlockSpec((None, n_rows, LANES), lambda b: (b, 0, 0))),
        out_shape=(jax.ShapeDtypeStruct((B, N_GROUPS, n_rows, LANES), BF16),
                   jax.ShapeDtypeStruct((B, n_rows, LANES), BF16)),
        compiler_params=pltpu.CompilerParams(dimension_semantics=("parallel",),
                                             vmem_limit_bytes=VMEM_LIMIT),
        name="compress",
    )(kc2, vc2, *weights)


def _softmax_rows(s):
    m = jnp.max(s, axis=1, keepdims=True)
    e = jnp.exp2(s - m)
    return e, jnp.sum(e, axis=1, keepdims=True)


def _attn_body(q_ref, qsw_ref, ks_ref, kw_ref, vs_ref, vw_ref, kc_ref, vc_ref, gate_ref, ov_ref,
               o_ref, qaug_ref, s_ref, m_ref, l_ref, acc_ref, oacc_ref, *, tq, tk, n_cmp_pad, n_blk):
    hg = HEADS_PER_GROUP
    g = pl.program_id(1)
    i = pl.program_id(2)
    s0 = i * tq
    t_col = s0 + lax.broadcasted_iota(jnp.int32, (tq, 1), 0)

    n_pairs = hg // 2
    rc = 2 * tq

    def qpair(h):
        src = q_ref if h % 2 == 0 else qsw_ref
        return src[:, (h // 2) * LANES:(h // 2 + 1) * LANES]

    def q_rows(p):
        return jnp.concatenate([qpair(2 * p), qpair(2 * p + 1)], axis=0)

    j_row = lax.broadcasted_iota(jnp.int32, (1, n_cmp_pad), 1)
    bias_c = jnp.where(j_row * CMP_STRIDE + (CMP_BLOCK - 1) <= t_col, 0.0, NEG_INF)
    vis = jnp.where(t_col >= CMP_BLOCK - 1, 1.0, 0.0)
    gates = gate_ref[...]

    def gate(h, branch):
        c = N_BRANCH * h + branch
        return gates[:, c:c + 1]

    psum = jnp.zeros((tq, n_cmp_pad), F32)
    for p in range(n_pairs):
        sc = _nt_dot(q_rows(p), kc_ref[...])
        parts = []
        for hh in range(2):
            e, l = _softmax_rows(sc[hh * tq:(hh + 1) * tq] + bias_c)
            pn = e * (vis / l)
            psum = psum + pn
            parts.append(pn.astype(BF16))
        oc = _dot(jnp.concatenate(parts, axis=0), vc_ref[...])
        for hh in range(2):
            h = 2 * p + hh
            oacc_ref[h * tq:(h + 1) * tq, :] = gate(h, 0) * oc[hh * tq:(hh + 1) * tq]

    n_win = WINDOW + tq
    kstart = pl.multiple_of(jnp.maximum(s0 - WINDOW, 0), tq)
    kpos_w = kstart + lax.broadcasted_iota(jnp.int32, (1, n_win), 1)
    bias_w = jnp.where((kpos_w <= t_col) & (kpos_w > t_col - WINDOW), 0.0, NEG_INF)
    for p in range(n_pairs):
        sw = _nt_dot(q_rows(p), kw_ref[pl.ds(kstart, n_win), :])
        parts = []
        for hh in range(2):
            swh = sw[hh * tq:(hh + 1) * tq] + bias_w
            e = jnp.exp2(swh - jnp.max(swh, axis=1, keepdims=True))
            parts.append(e.astype(BF16))
        ow = _dot(jnp.concatenate(parts, axis=0), vw_ref[pl.ds(kstart, n_win), :])
        for hh in range(2):
            h = 2 * p + hh
            sub = slice(hh * tq, (hh + 1) * tq)
            oacc_ref[h * tq:(h + 1) * tq, :] += (gate(h, 2) / ow[sub, LANES:]) * ow[sub, :LANES]

    hi = psum.astype(BF16)
    r1 = psum - hi.astype(F32)
    mid = r1.astype(BF16)
    lo = (r1 - mid.astype(F32)).astype(BF16)
    ov = ov_ref[...]
    imp = _dot(hi, ov) + _dot(mid, ov) + _dot(lo, ov)
    blk = lax.broadcasted_iota(jnp.int32, (tq, LANES), 1)
    cur = t_col >> 6
    forced = (blk == 0) | (blk == cur) | (blk == cur - 1)
    score = jnp.where(blk <= cur, imp + jnp.where(forced, FORCE_BONUS, 0.0), NEG_INF)
    st = score.T
    blk_t = lax.broadcasted_iota(jnp.int32, (LANES, tq), 0).astype(F32)
    sel_t = jnp.zeros((LANES, tq), F32)
    for _ in range(min(SEL_TOPK, n_blk)):
        mx = jnp.max(st, axis=0, keepdims=True)
        idx = jnp.min(jnp.where(st == mx, blk_t, float(LANES)), axis=0, keepdims=True)
        hit = blk_t == idx
        sel_t = jnp.where(hit, 1.0, sel_t)
        st = jnp.where(hit, -jnp.inf, st)
    sel = sel_t.T
    bias_s = jnp.where((sel > 0.5) & (blk <= cur), 0.0, NEG_INF).astype(BF16)

    for h in range(hg):
        qaug_ref[h * tq:(h + 1) * tq, 0:LANES] = bias_s
        qaug_ref[h * tq:(h + 1) * tq, LANES:2 * LANES] = qpair(h)
    m_ref[...] = jnp.full(m_ref.shape, NEG_INF, F32)
    l_ref[...] = jnp.zeros(l_ref.shape, F32)
    acc_ref[...] = jnp.zeros(acc_ref.shape, F32)

    n_lc = tk // LANES

    def scores(kt, rows):
        start = pl.multiple_of(kt * tk, tk)
        return _nt_dot(qaug_ref[rows, :], ks_ref[pl.ds(start, tk), :])

    def softmax_pv(kt, causal):
        start = pl.multiple_of(kt * tk, tk)
        if causal:
            kpos = start + lax.broadcasted_iota(jnp.int32, (1, tk), 1)
            bias_d = jnp.where(kpos <= t_col, 0.0, NEG_INF)
        for p in range(n_pairs):
            rows = slice(p * rc, (p + 1) * rc)
            s = s_ref[rows, :]
            if causal:
                s = jnp.concatenate([s[hh * tq:(hh + 1) * tq] + bias_d for hh in range(2)], axis=0)
            chunks = [s[:, c * LANES:(c + 1) * LANES] for c in range(n_lc)]
            mx = functools.reduce(jnp.maximum, chunks)
            m_old = m_ref[rows, :]
            m_new = jnp.maximum(m_old, jnp.max(mx, axis=1, keepdims=True))
            alpha = jnp.exp2(m_old - m_new)
            p_bf = jnp.concatenate([jnp.exp2(c - m_new) for c in chunks], axis=1).astype(BF16)
            pv = _dot(p_bf, vs_ref[pl.ds(start, tk), :])
            acc_ref[rows, :] = alpha * acc_ref[rows, :] + pv[:, :LANES]
            l_ref[rows, :] = alpha * l_ref[rows, :] + pv[:, LANES:]
            m_ref[rows, :] = m_new
            if not causal:
                s_ref[rows, :] = scores(kt + 1, rows)

    k_diag = s0 // tk
    s_ref[...] = scores(0, slice(None))

    def two_tiles(j, carry):
        softmax_pv(2 * j, False)
        softmax_pv(2 * j + 1, False)
        return carry

    lax.fori_loop(0, k_diag // 2, two_tiles, 0)

    @pl.when(k_diag % 2 == 1)
    def _():
        softmax_pv(k_diag - 1, False)

    softmax_pv(k_diag, True)

    outs = []
    for h in range(hg):
        rows = slice(h * tq, (h + 1) * tq)
        outs.append(oacc_ref[rows, :] + (gate(h, 1) / l_ref[rows, :]) * acc_ref[rows, :])
    lane = lax.broadcasted_iota(jnp.int32, (tq, LANES), 1)
    low = lane < HEAD_DIM
    is_g0 = g == 0
    for p in range(hg // 2):
        a = outs[2 * p]
        b = outs[2 * p + 1]
        xa = jnp.where(is_g0, a, pltpu.roll(a, HEAD_DIM, 1))
        xb = jnp.where(is_g0, pltpu.roll(b, HEAD_DIM, 1), b)
        o_ref[:, p * LANES:(p + 1) * LANES] = jnp.where(low, xa, xb).astype(BF16)


def _nsa_attn(q, qsw, ks, kw, vs, vw, kco, vco, gates, overlap):
    B, T, _ = q.shape
    tq, tk = TQ, TK_SEL
    n_cmp_pad = kco.shape[2]
    n_blk = T // SEL_BLOCK
    gw = HEADS_PER_GROUP * HEAD_DIM
    rows = HEADS_PER_GROUP * tq
    grid = (B, N_GROUPS, T // tq)
    return pl.pallas_call(
        functools.partial(_attn_body, tq=tq, tk=tk, n_cmp_pad=n_cmp_pad, n_blk=n_blk),
        grid=grid,
        in_specs=[
            pl.BlockSpec((None, tq, gw), lambda b, g, i: (b, i, g)),
            pl.BlockSpec((None, tq, gw), lambda b, g, i: (b, i, g)),
            pl.BlockSpec((None, None, T, 2 * LANES), lambda b, g, i: (b, g, 0, 0)),
            pl.BlockSpec((None, None, T, LANES), lambda b, g, i: (b, g, 0, 0)),
            pl.BlockSpec((None, T, 2 * LANES), lambda b, g, i: (b, 0, 0)),
            pl.BlockSpec((None, T, 2 * LANES), lambda b, g, i: (b, 0, 0)),
            pl.BlockSpec((None, None, n_cmp_pad, LANES), lambda b, g, i: (b, g, 0, 0)),
            pl.BlockSpec((None, n_cmp_pad, LANES), lambda b, g, i: (b, 0, 0)),
            pl.BlockSpec((None, None, tq, LANES), lambda b, g, i: (b, g, i, 0)),
            pl.BlockSpec(overlap.shape, lambda b, g, i: (0, 0)),
        ],
        out_specs=pl.BlockSpec((None, tq, gw), lambda b, g, i: (b, i, g)),
        out_shape=jax.ShapeDtypeStruct((B, T, N_HEADS * HEAD_DIM), BF16),
        scratch_shapes=[pltpu.VMEM((rows, 2 * LANES), BF16),
                        pltpu.VMEM((rows, tk), F32),
                        pltpu.VMEM((rows, LANES), F32),
                        pltpu.VMEM((rows, LANES), F32),
                        pltpu.VMEM((rows, LANES), F32),
                        pltpu.VMEM((rows, LANES), F32)],
        compiler_params=pltpu.CompilerParams(
            dimension_semantics=("parallel", "parallel", "arbitrary"),
            vmem_limit_bytes=VMEM_LIMIT),
        name="nsa_attn",
    )(q, qsw, ks, kw, vs, vw, kco, vco, gates, overlap)


def _outproj_body(x_ref, o_ref, w_ref, y_ref):
    y_ref[...] = x_ref[...] + _dot(o_ref[...], w_ref[...])


def _out_proj(x2, o2, w):
    N, D = x2.shape
    tm = TM_PROJ
    return pl.pallas_call(
        _outproj_body,
        grid=(N // tm,),
        in_specs=[pl.BlockSpec((tm, D), lambda i: (i, 0)),
                  pl.BlockSpec((tm, D), lambda i: (i, 0)),
                  pl.BlockSpec(w.shape, lambda i: (0, 0))],
        out_specs=pl.BlockSpec((tm, D), lambda i: (i, 0)),
        out_shape=jax.ShapeDtypeStruct((N, D), F32),
        compiler_params=pltpu.CompilerParams(dimension_semantics=("parallel",),
                                             vmem_limit_bytes=VMEM_LIMIT),
        name="out_proj",
    )(x2, o2, w)


def _ffn_body(x_ref, xh_ref, g_ref, wup_ref, cw_ref, cb_ref, wd_ref, gf_ref, o_ref, h_ref, acc_ref,
              ug0_ref, uv0_ref, ug1_ref, uv1_ref, *, tm, tf, dff, tiles_per_seq, final_norm):
    u_refs = ((ug0_ref, uv0_ref), (ug1_ref, uv1_ref))
    i = pl.program_id(0)
    halo = HALO_FFN
    n = halo + tm
    g = g_ref[...]
    hh = _rmsnorm(xh_ref[...], g)
    seq_start = (i % tiles_per_seq) == 0
    h_ref[0:halo, :] = jnp.where(seq_start, 0.0, hh).astype(BF16)
    h_ref[halo:n, :] = _rmsnorm(x_ref[...], g).astype(BF16)

    nj = dff // tf
    rb = FFN_ROW_BLOCK
    n_rb = tm // rb

    def up_proj(j, half, b):
        col = half * dff + j * tf
        r0 = 0 if b == 0 else halo + b * rb
        r1 = halo + (b + 1) * rb
        u_refs[j % 2][half][r0:r1, :] = _dot(h_ref[r0:r1, :], wup_ref[:, col:col + tf])

    def conv(j, half, b):
        col = half * dff + j * tf
        u = u_refs[j % 2][half][b * rb:halo + (b + 1) * rb, :]
        u1 = pltpu.roll(u, 1, 0)
        u2 = pltpu.roll(u, 2, 0)
        cw = cw_ref[:, col:col + tf]
        c = cw[0:1, :] * u2 + cw[1:2, :] * u1 + cw[2:3, :] * u + cb_ref[:, col:col + tf]
        return c[halo:halo + rb]

    for half in range(2):
        for b in range(n_rb):
            up_proj(0, half, b)
    for j in range(nj):
        for b in range(n_rb):
            rows = slice(b * rb, (b + 1) * rb)
            if j + 1 < nj:
                up_proj(j + 1, 0, b)
            cg = conv(j, 0, b)
            if j + 1 < nj:
                up_proj(j + 1, 1, b)
            cv = conv(j, 1, b)
            a = (cg * (1.0 / (1.0 + jnp.exp(-cg)))) * cv
            d = _dot(a.astype(BF16), wd_ref[j * tf:(j + 1) * tf, :])
            if j == 0:
                acc_ref[rows, :] = d
            else:
                acc_ref[rows, :] += d

    y = x_ref[...] + acc_ref[...]
    if final_norm:
        y = _rmsnorm(y, gf_ref[...])
    o_ref[...] = y


def _conv_ffn(x2, seq_len, g, w_up, conv_w, conv_b, w_down, g_final, final_norm):
    N, D = x2.shape
    dff = w_down.shape[0]
    tm, tf = TM_FFN, TF_FFN
    halo_blocks = tm // HALO_FFN
    resident = lambda a: pl.BlockSpec(a.shape, lambda i: (0, 0), pipeline_mode=pl.Buffered(1))
    return pl.pallas_call(
        functools.partial(_ffn_body, tm=tm, tf=tf, dff=dff, tiles_per_seq=seq_len // tm,
                          final_norm=final_norm),
        grid=(N // tm,),
        in_specs=[
            pl.BlockSpec((tm, D), lambda i: (i, 0)),
            pl.BlockSpec((HALO_FFN, D), lambda i: (jnp.maximum(i * halo_blocks - 1, 0), 0)),
            resident(g), resident(w_up), resident(conv_w), resident(conv_b), resident(w_down),
            resident(g_final),
        ],
        out_specs=pl.BlockSpec((tm, D), lambda i: (i, 0)),
        out_shape=jax.ShapeDtypeStruct((N, D), F32),
        scratch_shapes=[pltpu.VMEM((HALO_FFN + tm, D), BF16),
                        pltpu.VMEM((tm, D), F32),
                        ] + [pltpu.VMEM((HALO_FFN + tm, tf), F32)] * 4,
        compiler_params=pltpu.CompilerParams(dimension_semantics=("parallel",),
                                             vmem_limit_bytes=VMEM_LIMIT),
        name="conv_ffn",
    )(x2, x2, g, w_up, conv_w, conv_b, w_down, g_final)


def _pool_body(x_ref, xh_ref, g_ref, w_ref, b_ref, s_ref, o_ref, *, tm, tiles_per_seq, gdim):
    i = pl.program_id(0)
    halo = HALO_POOL
    g = g_ref[...]
    x = x_ref[...]
    h = _rmsnorm(x, g)
    seq_tile = i % tiles_per_seq
    hh = jnp.where(seq_tile == 0, 0.0, _rmsnorm(xh_ref[...], g))
    hext = jnp.concatenate([hh, h], axis=0)
    t_seq = seq_tile * tm + lax.broadcasted_iota(jnp.int32, (tm, 1), 0)
    ys = []
    for gi, w in enumerate(POOL_WINDOWS):
        cols = slice(gi * gdim, (gi + 1) * gdim)
        s = hext[:, cols]
        shift = 1
        while shift < w:
            s = s + pltpu.roll(s, shift, 0)
            shift *= 2
        cnt = jnp.minimum(t_seq + 1, w).astype(F32)
        pooled = s[halo:halo + tm] / cnt - h[:, cols]
        ys.append(_dot(pooled.astype(BF16), w_ref[gi]))
    y = jnp.concatenate(ys, axis=1) + b_ref[...]
    o_ref[...] = x + y * s_ref[...]


def _pool_mix(x2, seq_len, g, pool_w, pool_b, pool_scale):
    N, D = x2.shape
    tm = TM_POOL
    gdim = D // len(POOL_WINDOWS)
    halo_blocks = tm // HALO_POOL
    return pl.pallas_call(
        functools.partial(_pool_body, tm=tm, tiles_per_seq=seq_len // tm, gdim=gdim),
        grid=(N // tm,),
        in_specs=[
            pl.BlockSpec((tm, D), lambda i: (i, 0)),
            pl.BlockSpec((HALO_POOL, D), lambda i: (jnp.maximum(i * halo_blocks - 1, 0), 0)),
            pl.BlockSpec((1, D), lambda i: (0, 0)),
            pl.BlockSpec(pool_w.shape, lambda i: (0, 0, 0)),
            pl.BlockSpec((1, D), lambda i: (0, 0)),
            pl.BlockSpec((1, D), lambda i: (0, 0)),
        ],
        out_specs=pl.BlockSpec((tm, D), lambda i: (i, 0)),
        out_shape=jax.ShapeDtypeStruct((N, D), F32),
        compiler_params=pltpu.CompilerParams(dimension_semantics=("parallel",),
                                             vmem_limit_bytes=VMEM_LIMIT),
        name="pool_mix",
    )(x2, x2, g, pool_w, pool_b, pool_scale)


def _pad_cols(w, width):
    return jnp.pad(w, ((0, 0), (0, width - w.shape[1])))


def _inproj_weight(w_in):
    D = w_in.shape[0]
    kv = N_GROUPS * HEAD_DIM
    sizes = [N_HEADS * HEAD_DIM] + [kv] * 6 + [N_BRANCH * N_HEADS]
    offs = np.concatenate([[0], np.cumsum(sizes)])
    q, k_c, v_c, k_s, v_s, k_w, v_w, gt = [w_in[:, offs[n]:offs[n + 1]] for n in range(8)]
    per_group = lambda w, n: [_pad_cols(w[:, g * n:(g + 1) * n], LANES) for g in range(N_GROUPS)]
    cols = ([q] + per_group(k_s, HEAD_DIM) + per_group(k_w, HEAD_DIM) + [k_c, v_c, v_s, v_w]
            + per_group(gt, N_BRANCH * HEADS_PER_GROUP))
    return jnp.concatenate(cols, axis=1).astype(BF16)


def _rope_tables(pos):
    inv = 1.0 / (ROPE_THETA ** (jnp.arange(0, HEAD_DIM, 2, dtype=F32) / HEAD_DIM))
    ang = pos.astype(F32)[:, None] * inv[None, :]
    return jnp.cos(ang), jnp.sin(ang)


def _expand_w1(w1):
    hdim = w1.shape[1]
    halves = w1.reshape(2, CMP_STRIDE, HEAD_DIM, hdim)
    zeros = jnp.zeros((CMP_STRIDE, HEAD_DIM, hdim), w1.dtype)
    cols = []
    for half in range(2):
        for g in range(N_GROUPS):
            parts = [halves[half] if gg == g else zeros for gg in range(N_GROUPS)]
            cols.append(jnp.concatenate(parts, axis=1).reshape(CMP_STRIDE * N_GROUPS * HEAD_DIM, hdim))
    return jnp.concatenate(cols, axis=1).astype(BF16)


def _rot_half_cols(w):
    half = HEAD_DIM // 2
    return jnp.concatenate([-w[..., half:], w[..., :half]], axis=-1)


def _overlap_matrix(n_cmp_pad, n_blk):
    j = np.arange(n_cmp_pad)[:, None]
    s = np.arange(LANES)[None, :]
    lo = np.maximum(j * CMP_STRIDE, s * SEL_BLOCK)
    hi = np.minimum(j * CMP_STRIDE + CMP_BLOCK, (s + 1) * SEL_BLOCK)
    return jnp.asarray(np.clip(hi - lo, 0, None) / CMP_BLOCK, dtype=BF16)


def kernel(x, norm_mix_0, nsa_w_in, cmp_k_pos, cmp_k_w1, cmp_k_b1, cmp_k_w2, cmp_k_b2, cmp_v_pos, cmp_v_w1, cmp_v_b1, cmp_v_w2, cmp_v_b2, nsa_w_out, norm_ffn_0, ffn_up_0, ffn_conv_w_0, ffn_conv_b_0, ffn_down_0, norm_mix_1, pool_w, pool_b, pool_scale, norm_ffn_1, ffn_up_1, ffn_conv_w_1, ffn_conv_b_1, ffn_down_1, norm_final):
    B, T, D = x.shape
    assert D == N_HEADS * HEAD_DIM and SEL_BLOCK == 64
    assert T % TM_FFN == 0 and T % TK_SEL == 0 and TK_SEL % TQ == 0 and T >= WINDOW + TQ
    n_cmp_pad = T // CMP_STRIDE
    n_blk = T // SEL_BLOCK
    assert n_blk <= LANES
    row = lambda v: v.reshape(1, -1)

    cos, sin = _rope_tables(jnp.arange(T))
    cos4 = jnp.tile(cos, (1, 4))
    sin4 = jnp.tile(jnp.concatenate([-sin, sin], axis=1), (1, 2))
    q, qsw, ks, kw, kc, vc, vs, vw, gates = _in_proj(
        x, row(norm_mix_0), _inproj_weight(nsa_w_in), cos4, sin4)

    ccos, csin = _rope_tables(jnp.arange(n_cmp_pad) * CMP_STRIDE + (CMP_BLOCK - 1))
    zeros64 = jnp.zeros((n_cmp_pad, HEAD_DIM), F32)
    ccos2 = jnp.concatenate([ccos, ccos, zeros64], axis=1)
    csin2 = jnp.concatenate([csin, csin, zeros64], axis=1)
    pos_rows = lambda p: jnp.broadcast_to(p.reshape(1, -1), (SUBLANES, p.size)).astype(BF16)
    zero_w2 = jnp.zeros_like(cmp_v_w2)
    kparams = (_expand_w1(cmp_k_w1), cmp_k_w1.astype(BF16), pos_rows(cmp_k_pos), row(cmp_k_b1),
               jnp.concatenate([cmp_k_w2, _rot_half_cols(cmp_k_w2)], axis=1).astype(BF16),
               row(jnp.concatenate([cmp_k_b2, _rot_half_cols(cmp_k_b2)])))
    vparams = (_expand_w1(cmp_v_w1), cmp_v_w1.astype(BF16), pos_rows(cmp_v_pos), row(cmp_v_b1),
               jnp.concatenate([cmp_v_w2, zero_w2], axis=1).astype(BF16),
               jnp.concatenate([zero_w2, cmp_v_w2], axis=1).astype(BF16),
               row(jnp.concatenate([cmp_v_b2, cmp_v_b2])))
    width = CMP_STRIDE * N_GROUPS * HEAD_DIM
    kco, vco = _compress(kc.reshape(B, n_cmp_pad, width), vc.reshape(B, n_cmp_pad, width),
                         kparams, vparams, ccos2, csin2)

    o = _nsa_attn(q, qsw, ks, kw, vs, vw, kco, vco, gates, _overlap_matrix(n_cmp_pad, n_blk))
    x2 = _out_proj(x.reshape(B * T, D), o.reshape(B * T, D), nsa_w_out.astype(BF16))

    x2 = _conv_ffn(x2, T, row(norm_ffn_0), ffn_up_0.astype(BF16), ffn_conv_w_0, row(ffn_conv_b_0),
                   ffn_down_0.astype(BF16), row(norm_final), False)
    x2 = _pool_mix(x2, T, row(norm_mix_1), pool_w.astype(BF16), row(pool_b.reshape(-1)), row(pool_scale))
    x2 = _conv_ffn(x2, T, row(norm_ffn_1), ffn_up_1.astype(BF16), ffn_conv_w_1, row(ffn_conv_b_1),
                   ffn_down_1.astype(BF16), row(norm_final), True)
    return x2.reshape(B, T, D)
```

```python
import functools

import numpy as np
import jax
import jax.numpy as jnp
from jax import lax
from jax.experimental import pallas as pl
from jax.experimental.pallas import tpu as pltpu

F32 = jnp.float32
BF16 = jnp.bfloat16

N_HEADS = 16
HEAD_DIM = 64
N_GROUPS = 2
HEADS_PER_GROUP = N_HEADS // N_GROUPS
N_BRANCH = 3
CMP_STRIDE = 16
CMP_BLOCK = 32
SEL_BLOCK = 64
SEL_TOPK = 16
WINDOW = 512
ROPE_THETA = 10000.0
POOL_WINDOWS = (2, 4, 8, 16)
CONV_WIDTH = 3
NORM_EPS = 1e-6
NEG_INF = -1e30
FORCE_BONUS = 1e4
N_FORCED = 3
LOG2_E = 1.4426950408889634

LANES = 128
SUBLANES = 8
VMEM_LIMIT = 56 * 1024 * 1024

TM_PROJ = 512
TQ = 128
TK_SEL = 512
SEL_UNROLL = 2
TM_FFN = 512
TF_FFN = 256
FFN_ROW_BLOCK = 256
TM_POOL = 512
HALO_FFN = 8
HALO_POOL = 16


def _rmsnorm(x, g):
    return x * lax.rsqrt(jnp.mean(x * x, axis=-1, keepdims=True) + NORM_EPS) * g


def _nt_dot(a, b):
    return lax.dot_general(a, b, (((1,), (1,)), ((), ())), preferred_element_type=F32)


def _dot(a, b):
    return jnp.dot(a, b, preferred_element_type=F32)


def _inproj_body(x_ref, g_ref, w_ref, cos_ref, sin_ref,
                 q_ref, qsw_ref, ks_ref, kw_ref, kc_ref, vc_ref, vs_ref, vw_ref, gate_ref, *, tm):
    i = pl.program_id(1)
    h = _rmsnorm(x_ref[...], g_ref[...]).astype(BF16)
    res = _dot(h, w_ref[...])
    cos = cos_ref[...]
    sin = sin_ref[...]
    lane = lax.broadcasted_iota(jnp.int32, (tm, LANES), 1)
    first_half = (lane & (HEAD_DIM - 1)) < (HEAD_DIM // 2)

    def rope(xc):
        sw = jnp.where(first_half, pltpu.roll(xc, LANES - HEAD_DIM // 2, 1),
                       pltpu.roll(xc, HEAD_DIM // 2, 1))
        return xc * cos + sw * sin

    scale = HEAD_DIM ** -0.5 * LOG2_E
    n_pairs = N_HEADS // 2
    for p in range(n_pairs):
        qc = rope(res[:, p * LANES:(p + 1) * LANES]) * scale
        q_ref[:, p * LANES:(p + 1) * LANES] = qc.astype(BF16)
        qsw_ref[:, p * LANES:(p + 1) * LANES] = pltpu.roll(qc, HEAD_DIM, 1).astype(BF16)
    off = n_pairs * LANES
    row_t = i * tm + lax.broadcasted_iota(jnp.int32, (tm, LANES), 0)
    onehot = jnp.where(lane == (row_t >> 6), 1.0, 0.0).astype(BF16)
    for g in range(N_GROUPS):
        ks_ref[g, :, 0:LANES] = onehot
        ks_ref[g, :, LANES:2 * LANES] = rope(res[:, off + g * LANES: off + (g + 1) * LANES]).astype(BF16)
    off += N_GROUPS * LANES
    for g in range(N_GROUPS):
        kw_ref[g] = rope(res[:, off + g * LANES: off + (g + 1) * LANES]).astype(BF16)
    off += N_GROUPS * LANES
    kc_ref[...] = res[:, off:off + LANES]
    vc_ref[...] = res[:, off + LANES:off + 2 * LANES]
    ones = jnp.ones((tm, LANES), BF16)
    vs_ref[:, 0:LANES] = res[:, off + 2 * LANES:off + 3 * LANES].astype(BF16)
    vs_ref[:, LANES:2 * LANES] = ones
    vw_ref[:, 0:LANES] = res[:, off + 3 * LANES:off + 4 * LANES].astype(BF16)
    vw_ref[:, LANES:2 * LANES] = ones
    off += 4 * LANES
    for g in range(N_GROUPS):
        z = res[:, off + g * LANES: off + (g + 1) * LANES]
        gate_ref[g] = 1.0 / (1.0 + jnp.exp(-z))


def _in_proj(x, g, wp, cos4, sin4):
    B, T, D = x.shape
    tm = TM_PROJ
    ncols = wp.shape[1]
    grid = (B, T // tm)
    tok = lambda last: pl.BlockSpec((None, tm, last), lambda b, i: (b, i, 0))
    grp = lambda last: pl.BlockSpec((None, N_GROUPS, tm, last), lambda b, i: (b, 0, i, 0))
    out_shape = (
        jax.ShapeDtypeStruct((B, T, N_HEADS * HEAD_DIM), BF16),
        jax.ShapeDtypeStruct((B, T, N_HEADS * HEAD_DIM), BF16),
        jax.ShapeDtypeStruct((B, N_GROUPS, T, 2 * LANES), BF16),
        jax.ShapeDtypeStruct((B, N_GROUPS, T, LANES), BF16),
        jax.ShapeDtypeStruct((B, T, LANES), F32),
        jax.ShapeDtypeStruct((B, T, LANES), F32),
        jax.ShapeDtypeStruct((B, T, 2 * LANES), BF16),
        jax.ShapeDtypeStruct((B, T, 2 * LANES), BF16),
        jax.ShapeDtypeStruct((B, N_GROUPS, T, LANES), F32),
    )
    return pl.pallas_call(
        functools.partial(_inproj_body, tm=tm),
        grid=grid,
        in_specs=[tok(D),
                  pl.BlockSpec((1, D), lambda b, i: (0, 0)),
                  pl.BlockSpec((D, ncols), lambda b, i: (0, 0)),
                  pl.BlockSpec((tm, LANES), lambda b, i: (i, 0)),
                  pl.BlockSpec((tm, LANES), lambda b, i: (i, 0))],
        out_specs=(tok(N_HEADS * HEAD_DIM), tok(N_HEADS * HEAD_DIM), grp(2 * LANES), grp(LANES),
                   tok(LANES), tok(LANES), tok(2 * LANES), tok(2 * LANES), grp(LANES)),
        out_shape=out_shape,
        compiler_params=pltpu.CompilerParams(dimension_semantics=("parallel", "parallel"),
                                             vmem_limit_bytes=VMEM_LIMIT),
        name="in_proj",
    )(x, g, wp, cos4, sin4)


def _gelu_tanh(x):
    return 0.5 * x * (1.0 + jnp.tanh(np.sqrt(2.0 / np.pi).astype(np.float32) * (x + 0.044715 * (x * x * x))))


def _compress_body(kc_ref, vc_ref,
                   kw1e_ref, kw1_ref, kpos_ref, kb1_ref, kw2_ref, kb2_ref,
                   vw1e_ref, vw1_ref, vpos_ref, vb1_ref, vw2a_ref, vw2b_ref, vb2_ref,
                   cos_ref, sin_ref, kco_ref, vco_ref, *, n_rows, hidden):
    row = lax.broadcasted_iota(jnp.int32, (n_rows, LANES), 0)
    keep = row < (n_rows - 1)

    def hidden_act(c_ref, w1e_ref, w1_ref, pos_ref, b1_ref):
        c2 = c_ref[...].astype(BF16)
        ab = _dot(c2, w1e_ref[...])
        posb = _dot(pos_ref[...], w1_ref[...])[0:1, :] + b1_ref[...]
        outs = []
        for g in range(N_GROUPS):
            a = ab[:, g * hidden:(g + 1) * hidden]
            b = ab[:, (N_GROUPS + g) * hidden:(N_GROUPS + g + 1) * hidden]
            pre = a + pltpu.roll(b, n_rows - 1, 0) + posb
            outs.append(_gelu_tanh(pre).astype(BF16))
        return outs

    hk = hidden_act(kc_ref, kw1e_ref, kw1_ref, kpos_ref, kb1_ref)
    cos = cos_ref[...]
    sin = sin_ref[...]
    for g in range(N_GROUPS):
        r = _dot(hk[g], kw2_ref[...]) + kb2_ref[...]
        kco_ref[g] = jnp.where(keep, r * cos + pltpu.roll(r, HEAD_DIM, 1) * sin, 0.0).astype(BF16)
    hv = hidden_act(vc_ref, vw1e_ref, vw1_ref, vpos_ref, vb1_ref)
    v = _dot(hv[0], vw2a_ref[...]) + _dot(hv[1], vw2b_ref[...]) + vb2_ref[...]
    vco_ref[...] = jnp.where(keep, v, 0.0).astype(BF16)


def _compress(kc2, vc2, kparams, vparams, ccos, csin):
    B, n_rows, width = kc2.shape
    hidden = kparams[3].shape[1]
    full = lambda a: pl.BlockSpec(a.shape, lambda b: (0,) * a.ndim)
    blk = pl.BlockSpec((None, n_rows, width), lambda b: (b, 0, 0))
    weights = list(kparams) + list(vparams) + [ccos, csin]
    return pl.pallas_call(
        functools.partial(_compress_body, n_rows=n_rows, hidden=hidden),
        grid=(B,),
        in_specs=[blk, blk] + [full(a) for a in weights],
        out_specs=(pl.BlockSpec((None, N_GROUPS, n_rows, LANES), lambda b: (b, 0, 0, 0)),
                   pl.BlockSpec((None, n_rows, LANES), lambda b: (b, 0, 0))),
        out_shape=(jax.ShapeDtypeStruct((B, N_GROUPS, n_rows, LANES), BF16),
                   jax.ShapeDtypeStruct((B, n_rows, LANES), BF16)),
        compiler_params=pltpu.CompilerParams(dimension_semantics=("parallel",),
                                             vmem_limit_bytes=VMEM_LIMIT),
        name="compress",
    )(kc2, vc2, *weights)


def _softmax_rows(s):
    m = jnp.max(s, axis=1, keepdims=True)
    e = jnp.exp2(s - m)
    return e, jnp.sum(e, axis=1, keepdims=True)


def _attn_body(q_ref, qsw_ref, ks_ref, kw_ref, vs_ref, vw_ref, kc_ref, vc_ref, gate_ref, ov_ref,
               o_ref, qaug_ref, s_ref, m_ref, l_ref, acc_ref, oacc_ref, *, tq, tk, n_cmp_pad, n_blk):
    hg = HEADS_PER_GROUP
    g = pl.program_id(1)
    i = pl.program_id(2)
    s0 = i * tq
    t_col = s0 + lax.broadcasted_iota(jnp.int32, (tq, 1), 0)

    n_pairs = hg // 2
    rc = 2 * tq

    def qpair(h):
        src = q_ref if h % 2 == 0 else qsw_ref
        return src[:, (h // 2) * LANES:(h // 2 + 1) * LANES]

    def q_rows(p):
        return jnp.concatenate([qpair(2 * p), qpair(2 * p + 1)], axis=0)

    j_row = lax.broadcasted_iota(jnp.int32, (1, n_cmp_pad), 1)
    bias_c = jnp.where(j_row * CMP_STRIDE + (CMP_BLOCK - 1) <= t_col, 0.0, NEG_INF)
    vis = jnp.where(t_col >= CMP_BLOCK - 1, 1.0, 0.0)
    gates = gate_ref[...]

    def gate(h, branch):
        c = N_BRANCH * h + branch
        return gates[:, c:c + 1]

    q_all = jnp.concatenate([qpair(h) for h in range(hg)], axis=0)
    sc = _nt_dot(q_all, kc_ref[...]).reshape(hg, tq, n_cmp_pad) + bias_c[None]
    e = jnp.exp2(sc - jnp.max(sc, axis=2, keepdims=True))
    pn = e * (vis[None] / jnp.sum(e, axis=2, keepdims=True))
    psum = jnp.sum(pn, axis=0)
    oc = _dot(pn.reshape(hg * tq, n_cmp_pad).astype(BF16), vc_ref[...])
    for h in range(hg):
        oacc_ref[h * tq:(h + 1) * tq, :] = gate(h, 0) * oc[h * tq:(h + 1) * tq]

    n_win = WINDOW + tq
    kstart = pl.multiple_of(jnp.maximum(s0 - WINDOW, 0), tq)
    kpos_w = kstart + lax.broadcasted_iota(jnp.int32, (1, n_win), 1)
    bias_w = jnp.where((kpos_w <= t_col) & (kpos_w > t_col - WINDOW), 0.0, NEG_INF)
    sw = _nt_dot(q_all, kw_ref[pl.ds(kstart, n_win), :]).reshape(hg, tq, n_win) + bias_w[None]
    ew = jnp.exp2(sw - jnp.max(sw, axis=2, keepdims=True))
    ow = _dot(ew.reshape(hg * tq, n_win).astype(BF16), vw_ref[pl.ds(kstart, n_win), :])
    for h in range(hg):
        rows = slice(h * tq, (h + 1) * tq)
        oacc_ref[rows, :] += (gate(h, 2) / ow[rows, LANES:]) * ow[rows, :LANES]

    hi = psum.astype(BF16)
    r1 = psum - hi.astype(F32)
    mid = r1.astype(BF16)
    lo = (r1 - mid.astype(F32)).astype(BF16)
    ov = ov_ref[...]
    imp = _dot(hi, ov) + _dot(mid, ov) + _dot(lo, ov)
    blk = lax.broadcasted_iota(jnp.int32, (tq, LANES), 1)
    cur = t_col >> 6
    forced = (blk == 0) | (blk == cur) | (blk == cur - 1)
    valid = blk <= cur
    score = jnp.where(valid & jnp.logical_not(forced), imp, NEG_INF)
    st = score.T
    blk_t = lax.broadcasted_iota(jnp.int32, (LANES, tq), 0).astype(F32)
    sel_t = jnp.zeros((LANES, tq), F32)
    for _ in range(SEL_TOPK - N_FORCED):
        mx = jnp.max(st, axis=0, keepdims=True)
        idx = jnp.min(jnp.where(st == mx, blk_t, float(LANES)), axis=0, keepdims=True)
        hit = blk_t == idx
        sel_t = jnp.where(hit, 1.0, sel_t)
        st = jnp.where(hit, -jnp.inf, st)
    sel = (sel_t.T > 0.5) | forced
    bias_s = jnp.where(sel & valid, 0.0, NEG_INF).astype(BF16)

    for h in range(hg):
        qaug_ref[h * tq:(h + 1) * tq, 0:LANES] = bias_s
        qaug_ref[h * tq:(h + 1) * tq, LANES:2 * LANES] = qpair(h)
    m_ref[...] = jnp.full(m_ref.shape, NEG_INF, F32)
    l_ref[...] = jnp.zeros(l_ref.shape, F32)
    acc_ref[...] = jnp.zeros(acc_ref.shape, F32)

    n_lc = tk // LANES

    def scores(kt, rows):
        start = pl.multiple_of(kt * tk, tk)
        return _nt_dot(qaug_ref[rows, :], ks_ref[pl.ds(start, tk), :])

    def softmax_pv(kt, causal):
        start = pl.multiple_of(kt * tk, tk)
        if causal:
            kpos = start + lax.broadcasted_iota(jnp.int32, (1, tk), 1)
            bias_d = jnp.where(kpos <= t_col, 0.0, NEG_INF)
        for p in range(n_pairs):
            rows = slice(p * rc, (p + 1) * rc)
            s = s_ref[rows, :]
            if causal:
                s = jnp.concatenate([s[hh * tq:(hh + 1) * tq] + bias_d for hh in range(2)], axis=0)
            chunks = [s[:, c * LANES:(c + 1) * LANES] for c in range(n_lc)]
            mx = functools.reduce(jnp.maximum, chunks)
            m_old = m_ref[rows, :]
            m_new = jnp.maximum(m_old, jnp.max(mx, axis=1, keepdims=True))
            alpha = jnp.exp2(m_old - m_new)
            p_bf = jnp.concatenate([jnp.exp2(c - m_new) for c in chunks], axis=1).astype(BF16)
            pv = _dot(p_bf, vs_ref[pl.ds(start, tk), :])
            acc_ref[rows, :] = alpha * acc_ref[rows, :] + pv[:, :LANES]
            l_ref[rows, :] = alpha * l_ref[rows, :] + pv[:, LANES:]
            m_ref[rows, :] = m_new
            if not causal:
                s_ref[rows, :] = scores(kt + 1, rows)

    k_diag = s0 // tk
    s_ref[...] = scores(0, slice(None))

    unroll = SEL_UNROLL

    def tiles(j, carry):
        for u in range(unroll):
            softmax_pv(unroll * j + u, False)
        return carry

    n_loop = k_diag // unroll
    lax.fori_loop(0, n_loop, tiles, 0)
    for u in range(unroll - 1):

        @pl.when(n_loop * unroll + u < k_diag)
        def _():
            softmax_pv(n_loop * unroll + u, False)

    softmax_pv(k_diag, True)

    outs = []
    for h in range(hg):
        rows = slice(h * tq, (h + 1) * tq)
        outs.append(oacc_ref[rows, :] + (gate(h, 1) / l_ref[rows, :]) * acc_ref[rows, :])
    lane = lax.broadcasted_iota(jnp.int32, (tq, LANES), 1)
    low = lane < HEAD_DIM
    is_g0 = g == 0
    for p in range(hg // 2):
        a = outs[2 * p]
        b = outs[2 * p + 1]
        xa = jnp.where(is_g0, a, pltpu.roll(a, HEAD_DIM, 1))
        xb = jnp.where(is_g0, pltpu.roll(b, HEAD_DIM, 1), b)
        o_ref[:, p * LANES:(p + 1) * LANES] = jnp.where(low, xa, xb).astype(BF16)


def _nsa_attn(q, qsw, ks, kw, vs, vw, kco, vco, gates, overlap):
    B, T, _ = q.shape
    tq, tk = TQ, TK_SEL
    n_cmp_pad = kco.shape[2]
    n_blk = T // SEL_BLOCK
    gw = HEADS_PER_GROUP * HEAD_DIM
    rows = HEADS_PER_GROUP * tq
    grid = (B, N_GROUPS, T // tq)
    return pl.pallas_call(
        functools.partial(_attn_body, tq=tq, tk=tk, n_cmp_pad=n_cmp_pad, n_blk=n_blk),
        grid=grid,
        in_specs=[
            pl.BlockSpec((None, tq, gw), lambda b, g, i: (b, i, g)),
            pl.BlockSpec((None, tq, gw), lambda b, g, i: (b, i, g)),
            pl.BlockSpec((None, None, T, 2 * LANES), lambda b, g, i: (b, g, 0, 0)),
            pl.BlockSpec((None, None, T, LANES), lambda b, g, i: (b, g, 0, 0)),
            pl.BlockSpec((None, T, 2 * LANES), lambda b, g, i: (b, 0, 0)),
            pl.BlockSpec((None, T, 2 * LANES), lambda b, g, i: (b, 0, 0)),
            pl.BlockSpec((None, None, n_cmp_pad, LANES), lambda b, g, i: (b, g, 0, 0)),
            pl.BlockSpec((None, n_cmp_pad, LANES), lambda b, g, i: (b, 0, 0)),
            pl.BlockSpec((None, None, tq, LANES), lambda b, g, i: (b, g, i, 0)),
            pl.BlockSpec(overlap.shape, lambda b, g, i: (0, 0)),
        ],
        out_specs=pl.BlockSpec((None, tq, gw), lambda b, g, i: (b, i, g)),
        out_shape=jax.ShapeDtypeStruct((B, T, N_HEADS * HEAD_DIM), BF16),
        scratch_shapes=[pltpu.VMEM((rows, 2 * LANES), BF16),
                        pltpu.VMEM((rows, tk), F32),
                        pltpu.VMEM((rows, LANES), F32),
                        pltpu.VMEM((rows, LANES), F32),
                        pltpu.VMEM((rows, LANES), F32),
                        pltpu.VMEM((rows, LANES), F32)],
        compiler_params=pltpu.CompilerParams(
            dimension_semantics=("parallel", "parallel", "arbitrary"),
            vmem_limit_bytes=VMEM_LIMIT),
        name="nsa_attn",
    )(q, qsw, ks, kw, vs, vw, kco, vco, gates, overlap)


def _outproj_body(x_ref, o_ref, w_ref, y_ref):
    y_ref[...] = x_ref[...] + _dot(o_ref[...], w_ref[...])


def _out_proj(x2, o2, w):
    N, D = x2.shape
    tm = TM_PROJ
    return pl.pallas_call(
        _outproj_body,
        grid=(N // tm,),
        in_specs=[pl.BlockSpec((tm, D), lambda i: (i, 0)),
                  pl.BlockSpec((tm, D), lambda i: (i, 0)),
                  pl.BlockSpec(w.shape, lambda i: (0, 0))],
        out_specs=pl.BlockSpec((tm, D), lambda i: (i, 0)),
        out_shape=jax.ShapeDtypeStruct((N, D), F32),
        compiler_params=pltpu.CompilerParams(dimension_semantics=("parallel",),
                                             vmem_limit_bytes=VMEM_LIMIT),
        name="out_proj",
    )(x2, o2, w)


def _ffn_body(x_ref, xh_ref, g_ref, wup_ref, cw_ref, cb_ref, wd_ref, gf_ref, o_ref, h_ref, acc_ref,
              ug0_ref, uv0_ref, ug1_ref, uv1_ref, *, tm, tf, dff, tiles_per_seq, final_norm):
    u_refs = ((ug0_ref, uv0_ref), (ug1_ref, uv1_ref))
    i = pl.program_id(0)
    halo = HALO_FFN
    n = halo + tm
    g = g_ref[...]
    hh = _rmsnorm(xh_ref[...], g)
    seq_start = (i % tiles_per_seq) == 0
    h_ref[0:halo, :] = jnp.where(seq_start, 0.0, hh).astype(BF16)
    h_ref[halo:n, :] = _rmsnorm(x_ref[...], g).astype(BF16)

    nj = dff // tf
    rb = FFN_ROW_BLOCK
    n_rb = tm // rb

    def up_proj(j, half, b):
        col = half * dff + j * tf
        r0 = 0 if b == 0 else halo + b * rb
        r1 = halo + (b + 1) * rb
        u_refs[j % 2][half][r0:r1, :] = _dot(h_ref[r0:r1, :], wup_ref[:, col:col + tf])

    def conv(j, half, b):
        col = half * dff + j * tf
        u = u_refs[j % 2][half][b * rb:halo + (b + 1) * rb, :]
        u1 = pltpu.roll(u, 1, 0)
        u2 = pltpu.roll(u, 2, 0)
        cw = cw_ref[:, col:col + tf]
        c = cw[0:1, :] * u2 + cw[1:2, :] * u1 + cw[2:3, :] * u + cb_ref[:, col:col + tf]
        return c[halo:halo + rb]

    for half in range(2):
        for b in range(n_rb):
            up_proj(0, half, b)
    for j in range(nj):
        for b in range(n_rb):
            rows = slice(b * rb, (b + 1) * rb)
            if j + 1 < nj:
                up_proj(j + 1, 0, b)
            cg = conv(j, 0, b)
            if j + 1 < nj:
                up_proj(j + 1, 1, b)
            cv = conv(j, 1, b)
            a = (cg * (1.0 / (1.0 + jnp.exp(-cg)))) * cv
            d = _dot(a.astype(BF16), wd_ref[j * tf:(j + 1) * tf, :])
            if j == 0:
                acc_ref[rows, :] = d
            else:
                acc_ref[rows, :] += d

    y = x_ref[...] + acc_ref[...]
    if final_norm:
        y = _rmsnorm(y, gf_ref[...])
    o_ref[...] = y


def _conv_ffn(x2, seq_len, g, w_up, conv_w, conv_b, w_down, g_final, final_norm):
    N, D = x2.shape
    dff = w_down.shape[0]
    tm, tf = TM_FFN, TF_FFN
    halo_blocks = tm // HALO_FFN
    resident = lambda a: pl.BlockSpec(a.shape, lambda i: (0, 0), pipeline_mode=pl.Buffered(1))
    return pl.pallas_call(
        functools.partial(_ffn_body, tm=tm, tf=tf, dff=dff, tiles_per_seq=seq_len // tm,
                          final_norm=final_norm),
        grid=(N // tm,),
        in_specs=[
            pl.BlockSpec((tm, D), lambda i: (i, 0)),
            pl.BlockSpec((HALO_FFN, D), lambda i: (jnp.maximum(i * halo_blocks - 1, 0), 0)),
            resident(g), resident(w_up), resident(conv_w), resident(conv_b), resident(w_down),
            resident(g_final),
        ],
        out_specs=pl.BlockSpec((tm, D), lambda i: (i, 0)),
        out_shape=jax.ShapeDtypeStruct((N, D), F32),
        scratch_shapes=[pltpu.VMEM((HALO_FFN + tm, D), BF16),
                        pltpu.VMEM((tm, D), F32),
                        ] + [pltpu.VMEM((HALO_FFN + tm, tf), F32)] * 4,
        compiler_params=pltpu.CompilerParams(dimension_semantics=("parallel",),
                                             vmem_limit_bytes=VMEM_LIMIT),
        name="conv_ffn",
    )(x2, x2, g, w_up, conv_w, conv_b, w_down, g_final)


def _pool_body(x_ref, xh_ref, g_ref, w_ref, b_ref, s_ref, o_ref, *, tm, tiles_per_seq, gdim):
    i = pl.program_id(0)
    halo = HALO_POOL
    g = g_ref[...]
    x = x_ref[...]
    h = _rmsnorm(x, g)
    seq_tile = i % tiles_per_seq
    hh = jnp.where(seq_tile == 0, 0.0, _rmsnorm(xh_ref[...], g))
    hext = jnp.concatenate([hh, h], axis=0)
    t_seq = seq_tile * tm + lax.broadcasted_iota(jnp.int32, (tm, 1), 0)
    ys = []
    for gi, w in enumerate(POOL_WINDOWS):
        cols = slice(gi * gdim, (gi + 1) * gdim)
        s = hext[:, cols]
        shift = 1
        while shift < w:
            s = s + pltpu.roll(s, shift, 0)
            shift *= 2
        cnt = jnp.minimum(t_seq + 1, w).astype(F32)
        pooled = s[halo:halo + tm] / cnt - h[:, cols]
        ys.append(_dot(pooled.astype(BF16), w_ref[gi]))
    y = jnp.concatenate(ys, axis=1) + b_ref[...]
    o_ref[...] = x + y * s_ref[...]


def _pool_mix(x2, seq_len, g, pool_w, pool_b, pool_scale):
    N, D = x2.shape
    tm = TM_POOL
    gdim = D // len(POOL_WINDOWS)
    halo_blocks = tm // HALO_POOL
    return pl.pallas_call(
        functools.partial(_pool_body, tm=tm, tiles_per_seq=seq_len // tm, gdim=gdim),
        grid=(N // tm,),
        in_specs=[
            pl.BlockSpec((tm, D), lambda i: (i, 0)),
            pl.BlockSpec((HALO_POOL, D), lambda i: (jnp.maximum(i * halo_blocks - 1, 0), 0)),
            pl.BlockSpec((1, D), lambda i: (0, 0)),
            pl.BlockSpec(pool_w.shape, lambda i: (0, 0, 0)),
            pl.BlockSpec((1, D), lambda i: (0, 0)),
            pl.BlockSpec((1, D), lambda i: (0, 0)),
        ],
        out_specs=pl.BlockSpec((tm, D), lambda i: (i, 0)),
        out_shape=jax.ShapeDtypeStruct((N, D), F32),
        compiler_params=pltpu.CompilerParams(dimension_semantics=("parallel",),
                                             vmem_limit_bytes=VMEM_LIMIT),
        name="pool_mix",
    )(x2, x2, g, pool_w, pool_b, pool_scale)


def _pad_cols(w, width):
    return jnp.pad(w, ((0, 0), (0, width - w.shape[1])))


def _inproj_weight(w_in):
    D = w_in.shape[0]
    kv = N_GROUPS * HEAD_DIM
    sizes = [N_HEADS * HEAD_DIM] + [kv] * 6 + [N_BRANCH * N_HEADS]
    offs = np.concatenate([[0], np.cumsum(sizes)])
    q, k_c, v_c, k_s, v_s, k_w, v_w, gt = [w_in[:, offs[n]:offs[n + 1]] for n in range(8)]
    per_group = lambda w, n: [_pad_cols(w[:, g * n:(g + 1) * n], LANES) for g in range(N_GROUPS)]
    cols = ([q] + per_group(k_s, HEAD_DIM) + per_group(k_w, HEAD_DIM) + [k_c, v_c, v_s, v_w]
            + per_group(gt, N_BRANCH * HEADS_PER_GROUP))
    return jnp.concatenate(cols, axis=1).astype(BF16)


def _rope_tables(pos):
    inv = 1.0 / (ROPE_THETA ** (jnp.arange(0, HEAD_DIM, 2, dtype=F32) / HEAD_DIM))
    ang = pos.astype(F32)[:, None] * inv[None, :]
    return jnp.cos(ang), jnp.sin(ang)


def _expand_w1(w1):
    hdim = w1.shape[1]
    halves = w1.reshape(2, CMP_STRIDE, HEAD_DIM, hdim)
    zeros = jnp.zeros((CMP_STRIDE, HEAD_DIM, hdim), w1.dtype)
    cols = []
    for half in range(2):
        for g in range(N_GROUPS):
            parts = [halves[half] if gg == g else zeros for gg in range(N_GROUPS)]
            cols.append(jnp.concatenate(parts, axis=1).reshape(CMP_STRIDE * N_GROUPS * HEAD_DIM, hdim))
    return jnp.concatenate(cols, axis=1).astype(BF16)


def _rot_half_cols(w):
    half = HEAD_DIM // 2
    return jnp.concatenate([-w[..., half:], w[..., :half]], axis=-1)


def _overlap_matrix(n_cmp_pad, n_blk):
    j = np.arange(n_cmp_pad)[:, None]
    s = np.arange(LANES)[None, :]
    lo = np.maximum(j * CMP_STRIDE, s * SEL_BLOCK)
    hi = np.minimum(j * CMP_STRIDE + CMP_BLOCK, (s + 1) * SEL_BLOCK)
    return jnp.asarray(np.clip(hi - lo, 0, None) / CMP_BLOCK, dtype=BF16)


def kernel(x, norm_mix_0, nsa_w_in, cmp_k_pos, cmp_k_w1, cmp_k_b1, cmp_k_w2, cmp_k_b2, cmp_v_pos, cmp_v_w1, cmp_v_b1, cmp_v_w2, cmp_v_b2, nsa_w_out, norm_ffn_0, ffn_up_0, ffn_conv_w_0, ffn_conv_b_0, ffn_down_0, norm_mix_1, pool_w, pool_b, pool_scale, norm_ffn_1, ffn_up_1, ffn_conv_w_1, ffn_conv_b_1, ffn_down_1, norm_final):
    B, T, D = x.shape
    assert D == N_HEADS * HEAD_DIM and SEL_BLOCK == 64
    assert T % TM_FFN == 0 and T % TK_SEL == 0 and TK_SEL % TQ == 0 and T >= WINDOW + TQ
    n_cmp_pad = T // CMP_STRIDE
    n_blk = T // SEL_BLOCK
    assert n_blk <= LANES
    row = lambda v: v.reshape(1, -1)

    cos, sin = _rope_tables(jnp.arange(T))
    cos4 = jnp.tile(cos, (1, 4))
    sin4 = jnp.tile(jnp.concatenate([-sin, sin], axis=1), (1, 2))
    q, qsw, ks, kw, kc, vc, vs, vw, gates = _in_proj(
        x, row(norm_mix_0), _inproj_weight(nsa_w_in), cos4, sin4)

    ccos, csin = _rope_tables(jnp.arange(n_cmp_pad) * CMP_STRIDE + (CMP_BLOCK - 1))
    zeros64 = jnp.zeros((n_cmp_pad, HEAD_DIM), F32)
    ccos2 = jnp.concatenate([ccos, ccos, zeros64], axis=1)
    csin2 = jnp.concatenate([csin, csin, zeros64], axis=1)
    pos_rows = lambda p: jnp.broadcast_to(p.reshape(1, -1), (SUBLANES, p.size)).astype(BF16)
    zero_w2 = jnp.zeros_like(cmp_v_w2)
    kparams = (_expand_w1(cmp_k_w1), cmp_k_w1.astype(BF16), pos_rows(cmp_k_pos), row(cmp_k_b1),
               jnp.concatenate([cmp_k_w2, _rot_half_cols(cmp_k_w2)], axis=1).astype(BF16),
               row(jnp.concatenate([cmp_k_b2, _rot_half_cols(cmp_k_b2)])))
    vparams = (_expand_w1(cmp_v_w1), cmp_v_w1.astype(BF16), pos_rows(cmp_v_pos), row(cmp_v_b1),
               jnp.concatenate([cmp_v_w2, zero_w2], axis=1).astype(BF16),
               jnp.concatenate([zero_w2, cmp_v_w2], axis=1).astype(BF16),
               row(jnp.concatenate([cmp_v_b2, cmp_v_b2])))
    width = CMP_STRIDE * N_GROUPS * HEAD_DIM
    kco, vco = _compress(kc.reshape(B, n_cmp_pad, width), vc.reshape(B, n_cmp_pad, width),
                         kparams, vparams, ccos2, csin2)

    o = _nsa_attn(q, qsw, ks, kw, vs, vw, kco, vco, gates, _overlap_matrix(n_cmp_pad, n_blk))
    x2 = _out_proj(x.reshape(B * T, D), o.reshape(B * T, D), nsa_w_out.astype(BF16))

    x2 = _conv_ffn(x2, T, row(norm_ffn_0), ffn_up_0.astype(BF16), ffn_conv_w_0, row(ffn_conv_b_0),
                   ffn_down_0.astype(BF16), row(norm_final), False)
    x2 = _pool_mix(x2, T, row(norm_mix_1), pool_w.astype(BF16), row(pool_b.reshape(-1)), row(pool_scale))
    x2 = _conv_ffn(x2, T, row(norm_ffn_1), ffn_up_1.astype(BF16), ffn_conv_w_1, row(ffn_conv_b_1),
                   ffn_down_1.astype(BF16), row(norm_final), True)
    return x2.reshape(B, T, D)
```

```python
import functools

import numpy as np
import jax
import jax.numpy as jnp
from jax import lax
from jax.experimental import pallas as pl
from jax.experimental.pallas import tpu as pltpu

F32 = jnp.float32
BF16 = jnp.bfloat16

N_HEADS = 16
HEAD_DIM = 64
N_GROUPS = 2
HEADS_PER_GROUP = N_HEADS // N_GROUPS
N_BRANCH = 3
CMP_STRIDE = 16
CMP_BLOCK = 32
SEL_BLOCK = 64
SEL_TOPK = 16
WINDOW = 512
ROPE_THETA = 10000.0
POOL_WINDOWS = (2, 4, 8, 16)
CONV_WIDTH = 3
NORM_EPS = 1e-6
NEG_INF = -1e30
FORCE_BONUS = 1e4
N_FORCED = 3
LOG2_E = 1.4426950408889634

LANES = 128
SUBLANES = 8
VMEM_LIMIT = 56 * 1024 * 1024

TM_PROJ = 512
TQ = 256
TK_SEL = 512
SEL_UNROLL = 2
SEL_ROW_CHUNK = 256
TM_FFN = 512
TF_FFN = 256
FFN_ROW_BLOCK = 256
TM_POOL = 512
HALO_FFN = 8
HALO_POOL = 16


def _rmsnorm(x, g):
    return x * lax.rsqrt(jnp.mean(x * x, axis=-1, keepdims=True) + NORM_EPS) * g


def _nt_dot(a, b):
    return lax.dot_general(a, b, (((1,), (1,)), ((), ())), preferred_element_type=F32)


def _dot(a, b):
    return jnp.dot(a, b, preferred_element_type=F32)


def _inproj_body(x_ref, g_ref, w_ref, cos_ref, sin_ref,
                 q_ref, qsw_ref, ks_ref, kw_ref, kc_ref, vc_ref, vs_ref, vw_ref, gate_ref, *, tm):
    i = pl.program_id(1)
    h = _rmsnorm(x_ref[...], g_ref[...]).astype(BF16)
    res = _dot(h, w_ref[...])
    cos = cos_ref[...]
    sin = sin_ref[...]
    lane = lax.broadcasted_iota(jnp.int32, (tm, LANES), 1)
    first_half = (lane & (HEAD_DIM - 1)) < (HEAD_DIM // 2)

    def rope(xc):
        sw = jnp.where(first_half, pltpu.roll(xc, LANES - HEAD_DIM // 2, 1),
                       pltpu.roll(xc, HEAD_DIM // 2, 1))
        return xc * cos + sw * sin

    scale = HEAD_DIM ** -0.5 * LOG2_E
    n_pairs = N_HEADS // 2
    for p in range(n_pairs):
        qc = rope(res[:, p * LANES:(p + 1) * LANES]) * scale
        q_ref[:, p * LANES:(p + 1) * LANES] = qc.astype(BF16)
        qsw_ref[:, p * LANES:(p + 1) * LANES] = pltpu.roll(qc, HEAD_DIM, 1).astype(BF16)
    off = n_pairs * LANES
    row_t = i * tm + lax.broadcasted_iota(jnp.int32, (tm, LANES), 0)
    onehot = jnp.where(lane == (row_t >> 6), 1.0, 0.0).astype(BF16)
    for g in range(N_GROUPS):
        ks_ref[g, :, 0:LANES] = onehot
        ks_ref[g, :, LANES:2 * LANES] = rope(res[:, off + g * LANES: off + (g + 1) * LANES]).astype(BF16)
    off += N_GROUPS * LANES
    for g in range(N_GROUPS):
        kw_ref[g] = rope(res[:, off + g * LANES: off + (g + 1) * LANES]).astype(BF16)
    off += N_GROUPS * LANES
    kc_ref[...] = res[:, off:off + LANES]
    vc_ref[...] = res[:, off + LANES:off + 2 * LANES]
    ones = jnp.ones((tm, LANES), BF16)
    vs_ref[:, 0:LANES] = res[:, off + 2 * LANES:off + 3 * LANES].astype(BF16)
    vs_ref[:, LANES:2 * LANES] = ones
    vw_ref[:, 0:LANES] = res[:, off + 3 * LANES:off + 4 * LANES].astype(BF16)
    vw_ref[:, LANES:2 * LANES] = ones
    off += 4 * LANES
    for g in range(N_GROUPS):
        z = res[:, off + g * LANES: off + (g + 1) * LANES]
        gate_ref[g] = 1.0 / (1.0 + jnp.exp(-z))


def _in_proj(x, g, wp, cos4, sin4):
    B, T, D = x.shape
    tm = TM_PROJ
    ncols = wp.shape[1]
    grid = (B, T // tm)
    tok = lambda last: pl.BlockSpec((None, tm, last), lambda b, i: (b, i, 0))
    grp = lambda last: pl.BlockSpec((None, N_GROUPS, tm, last), lambda b, i: (b, 0, i, 0))
    out_shape = (
        jax.ShapeDtypeStruct((B, T, N_HEADS * HEAD_DIM), BF16),
        jax.ShapeDtypeStruct((B, T, N_HEADS * HEAD_DIM), BF16),
        jax.ShapeDtypeStruct((B, N_GROUPS, T, 2 * LANES), BF16),
        jax.ShapeDtypeStruct((B, N_GROUPS, T, LANES), BF16),
        jax.ShapeDtypeStruct((B, T, LANES), F32),
        jax.ShapeDtypeStruct((B, T, LANES), F32),
        jax.ShapeDtypeStruct((B, T, 2 * LANES), BF16),
        jax.ShapeDtypeStruct((B, T, 2 * LANES), BF16),
        jax.ShapeDtypeStruct((B, N_GROUPS, T, LANES), F32),
    )
    return pl.pallas_call(
        functools.partial(_inproj_body, tm=tm),
        grid=grid,
        in_specs=[tok(D),
                  pl.BlockSpec((1, D), lambda b, i: (0, 0)),
                  pl.BlockSpec((D, ncols), lambda b, i: (0, 0)),
                  pl.BlockSpec((tm, LANES), lambda b, i: (i, 0)),
                  pl.BlockSpec((tm, LANES), lambda b, i: (i, 0))],
        out_specs=(tok(N_HEADS * HEAD_DIM), tok(N_HEADS * HEAD_DIM), grp(2 * LANES), grp(LANES),
                   tok(LANES), tok(LANES), tok(2 * LANES), tok(2 * LANES), grp(LANES)),
        out_shape=out_shape,
        compiler_params=pltpu.CompilerParams(dimension_semantics=("parallel", "parallel"),
                                             vmem_limit_bytes=VMEM_LIMIT),
        name="in_proj",
    )(x, g, wp, cos4, sin4)


def _gelu_tanh(x):
    return 0.5 * x * (1.0 + jnp.tanh(np.sqrt(2.0 / np.pi).astype(np.float32) * (x + 0.044715 * (x * x * x))))


def _compress_body(kc_ref, vc_ref,
                   kw1e_ref, kw1_ref, kpos_ref, kb1_ref, kw2_ref, kb2_ref,
                   vw1e_ref, vw1_ref, vpos_ref, vb1_ref, vw2a_ref, vw2b_ref, vb2_ref,
                   cos_ref, sin_ref, kco_ref, vco_ref, *, n_rows, hidden):
    row = lax.broadcasted_iota(jnp.int32, (n_rows, LANES), 0)
    keep = row < (n_rows - 1)

    def hidden_act(c_ref, w1e_ref, w1_ref, pos_ref, b1_ref):
        c2 = c_ref[...].astype(BF16)
        ab = _dot(c2, w1e_ref[...])
        posb = _dot(pos_ref[...], w1_ref[...])[0:1, :] + b1_ref[...]
        outs = []
        for g in range(N_GROUPS):
            a = ab[:, g * hidden:(g + 1) * hidden]
            b = ab[:, (N_GROUPS + g) * hidden:(N_GROUPS + g + 1) * hidden]
            pre = a + pltpu.roll(b, n_rows - 1, 0) + posb
            outs.append(_gelu_tanh(pre).astype(BF16))
        return outs

    hk = hidden_act(kc_ref, kw1e_ref, kw1_ref, kpos_ref, kb1_ref)
    cos = cos_ref[...]
    sin = sin_ref[...]
    for g in range(N_GROUPS):
        r = _dot(hk[g], kw2_ref[...]) + kb2_ref[...]
        kco_ref[g] = jnp.where(keep, r * cos + pltpu.roll(r, HEAD_DIM, 1) * sin, 0.0).astype(BF16)
    hv = hidden_act(vc_ref, vw1e_ref, vw1_ref, vpos_ref, vb1_ref)
    v = _dot(hv[0], vw2a_ref[...]) + _dot(hv[1], vw2b_ref[...]) + vb2_ref[...]
    vco_ref[...] = jnp.where(keep, v, 0.0).astype(BF16)


def _compress(kc2, vc2, kparams, vparams, ccos, csin):
    B, n_rows, width = kc2.shape
    hidden = kparams[3].shape[1]
    full = lambda a: pl.BlockSpec(a.shape, lambda b: (0,) * a.ndim)
    blk = pl.BlockSpec((None, n_rows, width), lambda b: (b, 0, 0))
    weights = list(kparams) + list(vparams) + [ccos, csin]
    return pl.pallas_call(
        functools.partial(_compress_body, n_rows=n_rows, hidden=hidden),
        grid=(B,),
        in_specs=[blk, blk] + [full(a) for a in weights],
        out_specs=(pl.BlockSpec((None, N_GROUPS, n_rows, LANES), lambda b: (b, 0, 0, 0)),
                   pl.BlockSpec((None, n_rows, LANES), lambda b: (b, 0, 0))),
        out_shape=(jax.ShapeDtypeStruct((B, N_GROUPS, n_rows, LANES), BF16),
                   jax.ShapeDtypeStruct((B, n_rows, LANES), BF16)),
        compiler_params=pltpu.CompilerParams(dimension_semantics=("parallel",),
                                             vmem_limit_bytes=VMEM_LIMIT),
        name="compress",
    )(kc2, vc2, *weights)


def _softmax_rows(s):
    m = jnp.max(s, axis=1, keepdims=True)
    e = jnp.exp2(s - m)
    return e, jnp.sum(e, axis=1, keepdims=True)


def _attn_body(q_ref, qsw_ref, ks_ref, kw_ref, vs_ref, vw_ref, kc_ref, vc_ref, gate_ref, ov_ref,
               o_ref, qaug_ref, s_ref, m_ref, l_ref, acc_ref, oacc_ref, *, tq, tk, n_cmp_pad, n_blk):
    hg = HEADS_PER_GROUP
    g = pl.program_id(1)
    i = pl.program_id(2)
    s0 = i * tq
    t_col = s0 + lax.broadcasted_iota(jnp.int32, (tq, 1), 0)

    rc = SEL_ROW_CHUNK
    n_chunks = hg * tq // rc
    reps = rc // tq

    def qpair(h):
        src = q_ref if h % 2 == 0 else qsw_ref
        return src[:, (h // 2) * LANES:(h // 2 + 1) * LANES]

    j_row = lax.broadcasted_iota(jnp.int32, (1, n_cmp_pad), 1)
    bias_c = jnp.where(j_row * CMP_STRIDE + (CMP_BLOCK - 1) <= t_col, 0.0, NEG_INF)
    vis = jnp.where(t_col >= CMP_BLOCK - 1, 1.0, 0.0)
    gates = gate_ref[...]

    def gate(h, branch):
        c = N_BRANCH * h + branch
        return gates[:, c:c + 1]

    q_all = jnp.concatenate([qpair(h) for h in range(hg)], axis=0)
    sc = _nt_dot(q_all, kc_ref[...]).reshape(hg, tq, n_cmp_pad) + bias_c[None]
    e = jnp.exp2(sc - jnp.max(sc, axis=2, keepdims=True))
    r_c = vis[None] / jnp.sum(e, axis=2, keepdims=True)
    rhs_c = jnp.concatenate([vc_ref[...], ov_ref[...]], axis=1)
    pvc = _dot(e.reshape(hg * tq, n_cmp_pad).astype(BF16), rhs_c)
    imp = jnp.zeros((tq, LANES), F32)
    for h in range(hg):
        rows = slice(h * tq, (h + 1) * tq)
        oacc_ref[rows, :] = (gate(h, 0) * r_c[h]) * pvc[rows, :LANES]
        imp = imp + r_c[h] * pvc[rows, LANES:]

    n_win = WINDOW + tq
    kstart = pl.multiple_of(jnp.maximum(s0 - WINDOW, 0), tq)
    kpos_w = kstart + lax.broadcasted_iota(jnp.int32, (1, n_win), 1)
    bias_w = jnp.where((kpos_w <= t_col) & (kpos_w > t_col - WINDOW), 0.0, NEG_INF)
    sw = _nt_dot(q_all, kw_ref[pl.ds(kstart, n_win), :]).reshape(hg, tq, n_win) + bias_w[None]
    ew = jnp.exp2(sw - jnp.max(sw, axis=2, keepdims=True))
    ow = _dot(ew.reshape(hg * tq, n_win).astype(BF16), vw_ref[pl.ds(kstart, n_win), :])
    for h in range(hg):
        rows = slice(h * tq, (h + 1) * tq)
        oacc_ref[rows, :] += (gate(h, 2) / ow[rows, LANES:]) * ow[rows, :LANES]

    blk = lax.broadcasted_iota(jnp.int32, (tq, LANES), 1)
    cur = t_col >> 6
    forced = (blk == 0) | (blk == cur) | (blk == cur - 1)
    valid = blk <= cur
    score = jnp.where(valid & jnp.logical_not(forced), imp, NEG_INF)
    st = score.T
    blk_t = lax.broadcasted_iota(jnp.int32, (LANES, tq), 0).astype(F32)
    sel_t = jnp.zeros((LANES, tq), F32)
    for _ in range(SEL_TOPK - N_FORCED):
        mx = jnp.max(st, axis=0, keepdims=True)
        idx = jnp.min(jnp.where(st == mx, blk_t, float(LANES)), axis=0, keepdims=True)
        hit = blk_t == idx
        sel_t = jnp.where(hit, 1.0, sel_t)
        st = jnp.where(hit, -jnp.inf, st)
    sel = (sel_t.T > 0.5) | forced
    bias_s = jnp.where(sel & valid, 0.0, NEG_INF).astype(BF16)

    for h in range(hg):
        qaug_ref[h * tq:(h + 1) * tq, 0:LANES] = bias_s
        qaug_ref[h * tq:(h + 1) * tq, LANES:2 * LANES] = qpair(h)
    m_ref[...] = jnp.full(m_ref.shape, NEG_INF, F32)
    l_ref[...] = jnp.zeros(l_ref.shape, F32)
    acc_ref[...] = jnp.zeros(acc_ref.shape, F32)

    n_lc = tk // LANES

    def scores(kt, rows):
        start = pl.multiple_of(kt * tk, tk)
        return _nt_dot(qaug_ref[rows, :], ks_ref[pl.ds(start, tk), :])

    def softmax_pv(kt, causal):
        start = pl.multiple_of(kt * tk, tk)
        if causal:
            kpos = start + lax.broadcasted_iota(jnp.int32, (1, tk), 1)
            bias_d = jnp.where(kpos <= t_col, 0.0, NEG_INF)
        for p in range(n_chunks):
            rows = slice(p * rc, (p + 1) * rc)
            s = s_ref[rows, :]
            if causal:
                s = jnp.concatenate([s[r * tq:(r + 1) * tq] + bias_d for r in range(reps)], axis=0)
            chunks = [s[:, c * LANES:(c + 1) * LANES] for c in range(n_lc)]
            mx = functools.reduce(jnp.maximum, chunks)
            m_old = m_ref[rows, :]
            m_new = jnp.maximum(m_old, jnp.max(mx, axis=1, keepdims=True))
            alpha = jnp.exp2(m_old - m_new)
            p_bf = jnp.concatenate([jnp.exp2(c - m_new) for c in chunks], axis=1).astype(BF16)
            pv = _dot(p_bf, vs_ref[pl.ds(start, tk), :])
            acc_ref[rows, :] = alpha * acc_ref[rows, :] + pv[:, :LANES]
            l_ref[rows, :] = alpha * l_ref[rows, :] + pv[:, LANES:]
            m_ref[rows, :] = m_new
            if not causal:
                s_ref[rows, :] = scores(kt + 1, rows)

    k_diag = s0 // tk
    s_ref[...] = scores(0, slice(None))

    unroll = SEL_UNROLL

    def tiles(j, carry):
        for u in range(unroll):
            softmax_pv(unroll * j + u, False)
        return carry

    n_loop = k_diag // unroll
    lax.fori_loop(0, n_loop, tiles, 0)
    for u in range(unroll - 1):

        @pl.when(n_loop * unroll + u < k_diag)
        def _():
            softmax_pv(n_loop * unroll + u, False)

    softmax_pv(k_diag, True)

    outs = []
    for h in range(hg):
        rows = slice(h * tq, (h + 1) * tq)
        outs.append(oacc_ref[rows, :] + (gate(h, 1) / l_ref[rows, :]) * acc_ref[rows, :])
    lane = lax.broadcasted_iota(jnp.int32, (tq, LANES), 1)
    low = lane < HEAD_DIM
    is_g0 = g == 0
    for p in range(hg // 2):
        a = outs[2 * p]
        b = outs[2 * p + 1]
        xa = jnp.where(is_g0, a, pltpu.roll(a, HEAD_DIM, 1))
        xb = jnp.where(is_g0, pltpu.roll(b, HEAD_DIM, 1), b)
        o_ref[:, p * LANES:(p + 1) * LANES] = jnp.where(low, xa, xb).astype(BF16)


def _nsa_attn(q, qsw, ks, kw, vs, vw, kco, vco, gates, overlap):
    B, T, _ = q.shape
    tq, tk = TQ, TK_SEL
    n_cmp_pad = kco.shape[2]
    n_blk = T // SEL_BLOCK
    gw = HEADS_PER_GROUP * HEAD_DIM
    rows = HEADS_PER_GROUP * tq
    grid = (B, N_GROUPS, T // tq)
    once = pl.Buffered(1)
    return pl.pallas_call(
        functools.partial(_attn_body, tq=tq, tk=tk, n_cmp_pad=n_cmp_pad, n_blk=n_blk),
        grid=grid,
        in_specs=[
            pl.BlockSpec((None, tq, gw), lambda b, g, i: (b, i, g)),
            pl.BlockSpec((None, tq, gw), lambda b, g, i: (b, i, g)),
            pl.BlockSpec((None, None, T, 2 * LANES), lambda b, g, i: (b, g, 0, 0), pipeline_mode=once),
            pl.BlockSpec((None, None, T, LANES), lambda b, g, i: (b, g, 0, 0), pipeline_mode=once),
            pl.BlockSpec((None, T, 2 * LANES), lambda b, g, i: (b, 0, 0), pipeline_mode=once),
            pl.BlockSpec((None, T, 2 * LANES), lambda b, g, i: (b, 0, 0), pipeline_mode=once),
            pl.BlockSpec((None, None, n_cmp_pad, LANES), lambda b, g, i: (b, g, 0, 0)),
            pl.BlockSpec((None, n_cmp_pad, LANES), lambda b, g, i: (b, 0, 0)),
            pl.BlockSpec((None, None, tq, LANES), lambda b, g, i: (b, g, i, 0)),
            pl.BlockSpec(overlap.shape, lambda b, g, i: (0, 0)),
        ],
        out_specs=pl.BlockSpec((None, tq, gw), lambda b, g, i: (b, i, g)),
        out_shape=jax.ShapeDtypeStruct((B, T, N_HEADS * HEAD_DIM), BF16),
        scratch_shapes=[pltpu.VMEM((rows, 2 * LANES), BF16),
                        pltpu.VMEM((rows, tk), F32),
                        pltpu.VMEM((rows, LANES), F32),
                        pltpu.VMEM((rows, LANES), F32),
                        pltpu.VMEM((rows, LANES), F32),
                        pltpu.VMEM((rows, LANES), F32)],
        compiler_params=pltpu.CompilerParams(
            dimension_semantics=("parallel", "parallel", "arbitrary"),
            vmem_limit_bytes=VMEM_LIMIT),
        name="nsa_attn",
    )(q, qsw, ks, kw, vs, vw, kco, vco, gates, overlap)


def _outproj_body(x_ref, o_ref, w_ref, y_ref):
    y_ref[...] = x_ref[...] + _dot(o_ref[...], w_ref[...])


def _out_proj(x2, o2, w):
    N, D = x2.shape
    tm = TM_PROJ
    return pl.pallas_call(
        _outproj_body,
        grid=(N // tm,),
        in_specs=[pl.BlockSpec((tm, D), lambda i: (i, 0)),
                  pl.BlockSpec((tm, D), lambda i: (i, 0)),
                  pl.BlockSpec(w.shape, lambda i: (0, 0))],
        out_specs=pl.BlockSpec((tm, D), lambda i: (i, 0)),
        out_shape=jax.ShapeDtypeStruct((N, D), F32),
        compiler_params=pltpu.CompilerParams(dimension_semantics=("parallel",),
                                             vmem_limit_bytes=VMEM_LIMIT),
        name="out_proj",
    )(x2, o2, w)


def _ffn_body(x_ref, xh_ref, g_ref, wup_ref, cw_ref, cb_ref, wd_ref, gf_ref, o_ref, h_ref, acc_ref,
              ug0_ref, uv0_ref, ug1_ref, uv1_ref, *, tm, tf, dff, tiles_per_seq, final_norm):
    u_refs = ((ug0_ref, uv0_ref), (ug1_ref, uv1_ref))
    i = pl.program_id(0)
    halo = HALO_FFN
    n = halo + tm
    g = g_ref[...]
    hh = _rmsnorm(xh_ref[...], g)
    seq_start = (i % tiles_per_seq) == 0
    h_ref[0:halo, :] = jnp.where(seq_start, 0.0, hh).astype(BF16)
    h_ref[halo:n, :] = _rmsnorm(x_ref[...], g).astype(BF16)

    nj = dff // tf
    rb = FFN_ROW_BLOCK
    n_rb = tm // rb

    def up_proj(j, half, b):
        col = half * dff + j * tf
        r0 = 0 if b == 0 else halo + b * rb
        r1 = halo + (b + 1) * rb
        u_refs[j % 2][half][r0:r1, :] = _dot(h_ref[r0:r1, :], wup_ref[:, col:col + tf])

    def conv(j, half, b):
        col = half * dff + j * tf
        u = u_refs[j % 2][half][b * rb:halo + (b + 1) * rb, :]
        u1 = pltpu.roll(u, 1, 0)
        u2 = pltpu.roll(u, 2, 0)
        cw = cw_ref[:, col:col + tf]
        c = cw[0:1, :] * u2 + cw[1:2, :] * u1 + cw[2:3, :] * u + cb_ref[:, col:col + tf]
        return c[halo:halo + rb]

    for half in range(2):
        for b in range(n_rb):
            up_proj(0, half, b)
    for j in range(nj):
        for b in range(n_rb):
            rows = slice(b * rb, (b + 1) * rb)
            if j + 1 < nj:
                up_proj(j + 1, 0, b)
            cg = conv(j, 0, b)
            if j + 1 < nj:
                up_proj(j + 1, 1, b)
            cv = conv(j, 1, b)
            a = (cg * (1.0 / (1.0 + jnp.exp(-cg)))) * cv
            d = _dot(a.astype(BF16), wd_ref[j * tf:(j + 1) * tf, :])
            if j == 0:
                acc_ref[rows, :] = d
            else:
                acc_ref[rows, :] += d

    y = x_ref[...] + acc_ref[...]
    if final_norm:
        y = _rmsnorm(y, gf_ref[...])
    o_ref[...] = y


def _conv_ffn(x2, seq_len, g, w_up, conv_w, conv_b, w_down, g_final, final_norm):
    N, D = x2.shape
    dff = w_down.shape[0]
    tm, tf = TM_FFN, TF_FFN
    halo_blocks = tm // HALO_FFN
    resident = lambda a: pl.BlockSpec(a.shape, lambda i: (0, 0), pipeline_mode=pl.Buffered(1))
    return pl.pallas_call(
        functools.partial(_ffn_body, tm=tm, tf=tf, dff=dff, tiles_per_seq=seq_len // tm,
                          final_norm=final_norm),
        grid=(N // tm,),
        in_specs=[
            pl.BlockSpec((tm, D), lambda i: (i, 0)),
            pl.BlockSpec((HALO_FFN, D), lambda i: (jnp.maximum(i * halo_blocks - 1, 0), 0)),
            resident(g), resident(w_up), resident(conv_w), resident(conv_b), resident(w_down),
            resident(g_final),
        ],
        out_specs=pl.BlockSpec((tm, D), lambda i: (i, 0)),
        out_shape=jax.ShapeDtypeStruct((N, D), F32),
        scratch_shapes=[pltpu.VMEM((HALO_FFN + tm, D), BF16),
                        pltpu.VMEM((tm, D), F32),
                        ] + [pltpu.VMEM((HALO_FFN + tm, tf), F32)] * 4,
        compiler_params=pltpu.CompilerParams(dimension_semantics=("parallel",),
                                             vmem_limit_bytes=VMEM_LIMIT),
        name="conv_ffn",
    )(x2, x2, g, w_up, conv_w, conv_b, w_down, g_final)


def _pool_body(x_ref, xh_ref, g_ref, w_ref, b_ref, s_ref, o_ref, *, tm, tiles_per_seq, gdim):
    i = pl.program_id(0)
    halo = HALO_POOL
    g = g_ref[...]
    x = x_ref[...]
    h = _rmsnorm(x, g)
    seq_tile = i % tiles_per_seq
    hh = jnp.where(seq_tile == 0, 0.0, _rmsnorm(xh_ref[...], g))
    hext = jnp.concatenate([hh, h], axis=0)
    t_seq = seq_tile * tm + lax.broadcasted_iota(jnp.int32, (tm, 1), 0)
    ys = []
    for gi, w in enumerate(POOL_WINDOWS):
        cols = slice(gi * gdim, (gi + 1) * gdim)
        s = hext[:, cols]
        shift = 1
        while shift < w:
            s = s + pltpu.roll(s, shift, 0)
            shift *= 2
        cnt = jnp.minimum(t_seq + 1, w).astype(F32)
        pooled = s[halo:halo + tm] / cnt - h[:, cols]
        ys.append(_dot(pooled.astype(BF16), w_ref[gi]))
    y = jnp.concatenate(ys, axis=1) + b_ref[...]
    o_ref[...] = x + y * s_ref[...]


def _pool_mix(x2, seq_len, g, pool_w, pool_b, pool_scale):
    N, D = x2.shape
    tm = TM_POOL
    gdim = D // len(POOL_WINDOWS)
    halo_blocks = tm // HALO_POOL
    return pl.pallas_call(
        functools.partial(_pool_body, tm=tm, tiles_per_seq=seq_len // tm, gdim=gdim),
        grid=(N // tm,),
        in_specs=[
            pl.BlockSpec((tm, D), lambda i: (i, 0)),
            pl.BlockSpec((HALO_POOL, D), lambda i: (jnp.maximum(i * halo_blocks - 1, 0), 0)),
            pl.BlockSpec((1, D), lambda i: (0, 0)),
            pl.BlockSpec(pool_w.shape, lambda i: (0, 0, 0)),
            pl.BlockSpec((1, D), lambda i: (0, 0)),
            pl.BlockSpec((1, D), lambda i: (0, 0)),
        ],
        out_specs=pl.BlockSpec((tm, D), lambda i: (i, 0)),
        out_shape=jax.ShapeDtypeStruct((N, D), F32),
        compiler_params=pltpu.CompilerParams(dimension_semantics=("parallel",),
                                             vmem_limit_bytes=VMEM_LIMIT),
        name="pool_mix",
    )(x2, x2, g, pool_w, pool_b, pool_scale)


def _pad_cols(w, width):
    return jnp.pad(w, ((0, 0), (0, width - w.shape[1])))


def _inproj_weight(w_in):
    D = w_in.shape[0]
    kv = N_GROUPS * HEAD_DIM
    sizes = [N_HEADS * HEAD_DIM] + [kv] * 6 + [N_BRANCH * N_HEADS]
    offs = np.concatenate([[0], np.cumsum(sizes)])
    q, k_c, v_c, k_s, v_s, k_w, v_w, gt = [w_in[:, offs[n]:offs[n + 1]] for n in range(8)]
    per_group = lambda w, n: [_pad_cols(w[:, g * n:(g + 1) * n], LANES) for g in range(N_GROUPS)]
    cols = ([q] + per_group(k_s, HEAD_DIM) + per_group(k_w, HEAD_DIM) + [k_c, v_c, v_s, v_w]
            + per_group(gt, N_BRANCH * HEADS_PER_GROUP))
    return jnp.concatenate(cols, axis=1).astype(BF16)


def _rope_tables(pos):
    inv = 1.0 / (ROPE_THETA ** (jnp.arange(0, HEAD_DIM, 2, dtype=F32) / HEAD_DIM))
    ang = pos.astype(F32)[:, None] * inv[None, :]
    return jnp.cos(ang), jnp.sin(ang)


def _expand_w1(w1):
    hdim = w1.shape[1]
    halves = w1.reshape(2, CMP_STRIDE, HEAD_DIM, hdim)
    zeros = jnp.zeros((CMP_STRIDE, HEAD_DIM, hdim), w1.dtype)
    cols = []
    for half in range(2):
        for g in range(N_GROUPS):
            parts = [halves[half] if gg == g else zeros for gg in range(N_GROUPS)]
            cols.append(jnp.concatenate(parts, axis=1).reshape(CMP_STRIDE * N_GROUPS * HEAD_DIM, hdim))
    return jnp.concatenate(cols, axis=1).astype(BF16)


def _rot_half_cols(w):
    half = HEAD_DIM // 2
    return jnp.concatenate([-w[..., half:], w[..., :half]], axis=-1)


def _overlap_matrix(n_cmp_pad, n_blk):
    j = np.arange(n_cmp_pad)[:, None]
    s = np.arange(LANES)[None, :]
    lo = np.maximum(j * CMP_STRIDE, s * SEL_BLOCK)
    hi = np.minimum(j * CMP_STRIDE + CMP_BLOCK, (s + 1) * SEL_BLOCK)
    return jnp.asarray(np.clip(hi - lo, 0, None) / CMP_BLOCK, dtype=BF16)


def kernel(x, norm_mix_0, nsa_w_in, cmp_k_pos, cmp_k_w1, cmp_k_b1, cmp_k_w2, cmp_k_b2, cmp_v_pos, cmp_v_w1, cmp_v_b1, cmp_v_w2, cmp_v_b2, nsa_w_out, norm_ffn_0, ffn_up_0, ffn_conv_w_0, ffn_conv_b_0, ffn_down_0, norm_mix_1, pool_w, pool_b, pool_scale, norm_ffn_1, ffn_up_1, ffn_conv_w_1, ffn_conv_b_1, ffn_down_1, norm_final):
    B, T, D = x.shape
    assert D == N_HEADS * HEAD_DIM and SEL_BLOCK == 64
    assert T % TM_FFN == 0 and T % TK_SEL == 0 and TK_SEL % TQ == 0 and T >= WINDOW + TQ
    n_cmp_pad = T // CMP_STRIDE
    n_blk = T // SEL_BLOCK
    assert n_blk <= LANES
    row = lambda v: v.reshape(1, -1)

    cos, sin = _rope_tables(jnp.arange(T))
    cos4 = jnp.tile(cos, (1, 4))
    sin4 = jnp.tile(jnp.concatenate([-sin, sin], axis=1), (1, 2))
    q, qsw, ks, kw, kc, vc, vs, vw, gates = _in_proj(
        x, row(norm_mix_0), _inproj_weight(nsa_w_in), cos4, sin4)

    ccos, csin = _rope_tables(jnp.arange(n_cmp_pad) * CMP_STRIDE + (CMP_BLOCK - 1))
    zeros64 = jnp.zeros((n_cmp_pad, HEAD_DIM), F32)
    ccos2 = jnp.concatenate([ccos, ccos, zeros64], axis=1)
    csin2 = jnp.concatenate([csin, csin, zeros64], axis=1)
    pos_rows = lambda p: jnp.broadcast_to(p.reshape(1, -1), (SUBLANES, p.size)).astype(BF16)
    zero_w2 = jnp.zeros_like(cmp_v_w2)
    kparams = (_expand_w1(cmp_k_w1), cmp_k_w1.astype(BF16), pos_rows(cmp_k_pos), row(cmp_k_b1),
               jnp.concatenate([cmp_k_w2, _rot_half_cols(cmp_k_w2)], axis=1).astype(BF16),
               row(jnp.concatenate([cmp_k_b2, _rot_half_cols(cmp_k_b2)])))
    vparams = (_expand_w1(cmp_v_w1), cmp_v_w1.astype(BF16), pos_rows(cmp_v_pos), row(cmp_v_b1),
               jnp.concatenate([cmp_v_w2, zero_w2], axis=1).astype(BF16),
               jnp.concatenate([zero_w2, cmp_v_w2], axis=1).astype(BF16),
               row(jnp.concatenate([cmp_v_b2, cmp_v_b2])))
    width = CMP_STRIDE * N_GROUPS * HEAD_DIM
    kco, vco = _compress(kc.reshape(B, n_cmp_pad, width), vc.reshape(B, n_cmp_pad, width),
                         kparams, vparams, ccos2, csin2)

    o = _nsa_attn(q, qsw, ks, kw, vs, vw, kco, vco, gates, _overlap_matrix(n_cmp_pad, n_blk))
    x2 = _out_proj(x.reshape(B * T, D), o.reshape(B * T, D), nsa_w_out.astype(BF16))

    x2 = _conv_ffn(x2, T, row(norm_ffn_0), ffn_up_0.astype(BF16), ffn_conv_w_0, row(ffn_conv_b_0),
                   ffn_down_0.astype(BF16), row(norm_final), False)
    x2 = _pool_mix(x2, T, row(norm_mix_1), pool_w.astype(BF16), row(pool_b.reshape(-1)), row(pool_scale))
    x2 = _conv_ffn(x2, T, row(norm_ffn_1), ffn_up_1.astype(BF16), ffn_conv_w_1, row(ffn_conv_b_1),
                   ffn_down_1.astype(BF16), row(norm_final), True)
    return x2.reshape(B, T, D)
```

```python
import functools

import numpy as np
import jax
import jax.numpy as jnp
from jax import lax
from jax.experimental import pallas as pl
from jax.experimental.pallas import tpu as pltpu

F32 = jnp.float32
BF16 = jnp.bfloat16

N_HEADS = 16
HEAD_DIM = 64
N_GROUPS = 2
HEADS_PER_GROUP = N_HEADS // N_GROUPS
N_BRANCH = 3
CMP_STRIDE = 16
CMP_BLOCK = 32
SEL_BLOCK = 64
SEL_TOPK = 16
WINDOW = 512
ROPE_THETA = 10000.0
POOL_WINDOWS = (2, 4, 8, 16)
CONV_WIDTH = 3
NORM_EPS = 1e-6
NEG_INF = -1e30
FORCE_BONUS = 1e4
N_FORCED = 3
LOG2_E = 1.4426950408889634

LANES = 128
SUBLANES = 8
VMEM_LIMIT = 56 * 1024 * 1024

TM_PROJ = 512
TQ = 256
TK_SEL = 512
SEL_UNROLL = 2
SEL_ROW_CHUNK = 512
TM_FFN = 512
TF_FFN = 256
FFN_ROW_BLOCK = 256
HALO_FFN = 8
HALO_ATTN_BF16 = 16
HALO_POOL = 32


def _rmsnorm(x, g):
    return x * lax.rsqrt(jnp.mean(x * x, axis=-1, keepdims=True) + NORM_EPS) * g


def _nt_dot(a, b):
    return lax.dot_general(a, b, (((1,), (1,)), ((), ())), preferred_element_type=F32)


def _dot(a, b):
    return jnp.dot(a, b, preferred_element_type=F32)


def _inproj_body(x_ref, g_ref, w_ref, cos_ref, sin_ref,
                 q_ref, qsw_ref, ks_ref, kw_ref, kc_ref, vc_ref, vs_ref, vw_ref, gate_ref, *, tm):
    i = pl.program_id(1)
    h = _rmsnorm(x_ref[...], g_ref[...]).astype(BF16)
    res = _dot(h, w_ref[...])
    cos = cos_ref[...]
    sin = sin_ref[...]
    lane = lax.broadcasted_iota(jnp.int32, (tm, LANES), 1)
    first_half = (lane & (HEAD_DIM - 1)) < (HEAD_DIM // 2)

    def rope(xc):
        sw = jnp.where(first_half, pltpu.roll(xc, LANES - HEAD_DIM // 2, 1),
                       pltpu.roll(xc, HEAD_DIM // 2, 1))
        return xc * cos + sw * sin

    scale = HEAD_DIM ** -0.5 * LOG2_E
    n_pairs = N_HEADS // 2
    for p in range(n_pairs):
        qc = rope(res[:, p * LANES:(p + 1) * LANES]) * scale
        q_ref[:, p * LANES:(p + 1) * LANES] = qc.astype(BF16)
        qsw_ref[:, p * LANES:(p + 1) * LANES] = pltpu.roll(qc, HEAD_DIM, 1).astype(BF16)
    off = n_pairs * LANES
    row_t = i * tm + lax.broadcasted_iota(jnp.int32, (tm, LANES), 0)
    onehot = jnp.where(lane == (row_t >> 6), 1.0, 0.0).astype(BF16)
    for g in range(N_GROUPS):
        ks_ref[g, :, 0:LANES] = onehot
        ks_ref[g, :, LANES:2 * LANES] = rope(res[:, off + g * LANES: off + (g + 1) * LANES]).astype(BF16)
    off += N_GROUPS * LANES
    for g in range(N_GROUPS):
        kw_ref[g] = rope(res[:, off + g * LANES: off + (g + 1) * LANES]).astype(BF16)
    off += N_GROUPS * LANES
    kc_ref[...] = res[:, off:off + LANES]
    vc_ref[...] = res[:, off + LANES:off + 2 * LANES]
    ones = jnp.ones((tm, LANES), BF16)
    vs_ref[:, 0:LANES] = res[:, off + 2 * LANES:off + 3 * LANES].astype(BF16)
    vs_ref[:, LANES:2 * LANES] = ones
    vw_ref[:, 0:LANES] = res[:, off + 3 * LANES:off + 4 * LANES].astype(BF16)
    vw_ref[:, LANES:2 * LANES] = ones
    off += 4 * LANES
    for g in range(N_GROUPS):
        z = res[:, off + g * LANES: off + (g + 1) * LANES]
        gate_ref[g] = 1.0 / (1.0 + jnp.exp(-z))


def _in_proj(x, g, wp, cos4, sin4):
    B, T, D = x.shape
    tm = TM_PROJ
    ncols = wp.shape[1]
    grid = (B, T // tm)
    tok = lambda last: pl.BlockSpec((None, tm, last), lambda b, i: (b, i, 0))
    grp = lambda last: pl.BlockSpec((None, N_GROUPS, tm, last), lambda b, i: (b, 0, i, 0))
    out_shape = (
        jax.ShapeDtypeStruct((B, T, N_HEADS * HEAD_DIM), BF16),
        jax.ShapeDtypeStruct((B, T, N_HEADS * HEAD_DIM), BF16),
        jax.ShapeDtypeStruct((B, N_GROUPS, T, 2 * LANES), BF16),
        jax.ShapeDtypeStruct((B, N_GROUPS, T, LANES), BF16),
        jax.ShapeDtypeStruct((B, T, LANES), F32),
        jax.ShapeDtypeStruct((B, T, LANES), F32),
        jax.ShapeDtypeStruct((B, T, 2 * LANES), BF16),
        jax.ShapeDtypeStruct((B, T, 2 * LANES), BF16),
        jax.ShapeDtypeStruct((B, N_GROUPS, T, LANES), F32),
    )
    return pl.pallas_call(
        functools.partial(_inproj_body, tm=tm),
        grid=grid,
        in_specs=[tok(D),
                  pl.BlockSpec((1, D), lambda b, i: (0, 0)),
                  pl.BlockSpec((D, ncols), lambda b, i: (0, 0)),
                  pl.BlockSpec((tm, LANES), lambda b, i: (i, 0)),
                  pl.BlockSpec((tm, LANES), lambda b, i: (i, 0))],
        out_specs=(tok(N_HEADS * HEAD_DIM), tok(N_HEADS * HEAD_DIM), grp(2 * LANES), grp(LANES),
                   tok(LANES), tok(LANES), tok(2 * LANES), tok(2 * LANES), grp(LANES)),
        out_shape=out_shape,
        compiler_params=pltpu.CompilerParams(dimension_semantics=("parallel", "parallel"),
                                             vmem_limit_bytes=VMEM_LIMIT),
        name="in_proj",
    )(x, g, wp, cos4, sin4)


def _gelu_tanh(x):
    return 0.5 * x * (1.0 + jnp.tanh(np.sqrt(2.0 / np.pi).astype(np.float32) * (x + 0.044715 * (x * x * x))))


def _compress_body(kc_ref, vc_ref,
                   kw1e_ref, kw1_ref, kpos_ref, kb1_ref, kw2_ref, kb2_ref,
                   vw1e_ref, vw1_ref, vpos_ref, vb1_ref, vw2a_ref, vw2b_ref, vb2_ref,
                   cos_ref, sin_ref, kco_ref, vco_ref, *, n_rows, hidden):
    row = lax.broadcasted_iota(jnp.int32, (n_rows, LANES), 0)
    keep = row < (n_rows - 1)

    def hidden_act(c_ref, w1e_ref, w1_ref, pos_ref, b1_ref):
        c2 = c_ref[...].astype(BF16)
        ab = _dot(c2, w1e_ref[...])
        posb = _dot(pos_ref[...], w1_ref[...])[0:1, :] + b1_ref[...]
        outs = []
        for g in range(N_GROUPS):
            a = ab[:, g * hidden:(g + 1) * hidden]
            b = ab[:, (N_GROUPS + g) * hidden:(N_GROUPS + g + 1) * hidden]
            pre = a + pltpu.roll(b, n_rows - 1, 0) + posb
            outs.append(_gelu_tanh(pre).astype(BF16))
        return outs

    hk = hidden_act(kc_ref, kw1e_ref, kw1_ref, kpos_ref, kb1_ref)
    cos = cos_ref[...]
    sin = sin_ref[...]
    for g in range(N_GROUPS):
        r = _dot(hk[g], kw2_ref[...]) + kb2_ref[...]
        kco_ref[g] = jnp.where(keep, r * cos + pltpu.roll(r, HEAD_DIM, 1) * sin, 0.0).astype(BF16)
    hv = hidden_act(vc_ref, vw1e_ref, vw1_ref, vpos_ref, vb1_ref)
    v = _dot(hv[0], vw2a_ref[...]) + _dot(hv[1], vw2b_ref[...]) + vb2_ref[...]
    vco_ref[...] = jnp.where(keep, v, 0.0).astype(BF16)


def _compress(kc2, vc2, kparams, vparams, ccos, csin):
    B, n_rows, width = kc2.shape
    hidden = kparams[3].shape[1]
    full = lambda a: pl.BlockSpec(a.shape, lambda b: (0,) * a.ndim)
    blk = pl.BlockSpec((None, n_rows, width), lambda b: (b, 0, 0))
    weights = list(kparams) + list(vparams) + [ccos, csin]
    return pl.pallas_call(
        functools.partial(_compress_body, n_rows=n_rows, hidden=hidden),
        grid=(B,),
        in_specs=[blk, blk] + [full(a) for a in weights],
        out_specs=(pl.BlockSpec((None, N_GROUPS, n_rows, LANES), lambda b: (b, 0, 0, 0)),
                   pl.BlockSpec((None, n_rows, LANES), lambda b: (b, 0, 0))),
        out_shape=(jax.ShapeDtypeStruct((B, N_GROUPS, n_rows, LANES), BF16),
                   jax.ShapeDtypeStruct((B, n_rows, LANES), BF16)),
        compiler_params=pltpu.CompilerParams(dimension_semantics=("parallel",),
                                             vmem_limit_bytes=VMEM_LIMIT),
        name="compress",
    )(kc2, vc2, *weights)


def _softmax_rows(s):
    m = jnp.max(s, axis=1, keepdims=True)
    e = jnp.exp2(s - m)
    return e, jnp.sum(e, axis=1, keepdims=True)


def _attn_body(q_ref, qsw_ref, ks_ref, kw_ref, vs_ref, vw_ref, kc_ref, vc_ref, gate_ref, ov_ref,
               o_ref, qaug_ref, s_ref, m_ref, l_ref, acc_ref, oacc_ref, *, tq, tk, n_cmp_pad, n_blk):
    hg = HEADS_PER_GROUP
    g = pl.program_id(1)
    i = pl.program_id(2)
    s0 = i * tq
    t_col = s0 + lax.broadcasted_iota(jnp.int32, (tq, 1), 0)

    rc = SEL_ROW_CHUNK
    n_chunks = hg * tq // rc
    reps = rc // tq

    def qpair(h):
        src = q_ref if h % 2 == 0 else qsw_ref
        return src[:, (h // 2) * LANES:(h // 2 + 1) * LANES]

    j_row = lax.broadcasted_iota(jnp.int32, (1, n_cmp_pad), 1)
    bias_c = jnp.where(j_row * CMP_STRIDE + (CMP_BLOCK - 1) <= t_col, 0.0, NEG_INF)
    vis = jnp.where(t_col >= CMP_BLOCK - 1, 1.0, 0.0)
    gates = gate_ref[...]

    def gate(h, branch):
        c = N_BRANCH * h + branch
        return gates[:, c:c + 1]

    q_all = jnp.concatenate([qpair(h) for h in range(hg)], axis=0)
    sc = _nt_dot(q_all, kc_ref[...]).reshape(hg, tq, n_cmp_pad) + bias_c[None]
    e = jnp.exp2(sc - jnp.max(sc, axis=2, keepdims=True))
    r_c = vis[None] / jnp.sum(e, axis=2, keepdims=True)
    rhs_c = jnp.concatenate([vc_ref[...], ov_ref[...]], axis=1)
    pvc = _dot(e.reshape(hg * tq, n_cmp_pad).astype(BF16), rhs_c)
    imp = jnp.zeros((tq, LANES), F32)
    for h in range(hg):
        rows = slice(h * tq, (h + 1) * tq)
        oacc_ref[rows, :] = (gate(h, 0) * r_c[h]) * pvc[rows, :LANES]
        imp = imp + r_c[h] * pvc[rows, LANES:]

    n_win = WINDOW + tq
    kstart = pl.multiple_of(jnp.maximum(s0 - WINDOW, 0), tq)
    kpos_w = kstart + lax.broadcasted_iota(jnp.int32, (1, n_win), 1)
    bias_w = jnp.where((kpos_w <= t_col) & (kpos_w > t_col - WINDOW), 0.0, NEG_INF)
    sw = _nt_dot(q_all, kw_ref[pl.ds(kstart, n_win), :]).reshape(hg, tq, n_win) + bias_w[None]
    ew = jnp.exp2(sw - jnp.max(sw, axis=2, keepdims=True))
    ow = _dot(ew.reshape(hg * tq, n_win).astype(BF16), vw_ref[pl.ds(kstart, n_win), :])
    for h in range(hg):
        rows = slice(h * tq, (h + 1) * tq)
        oacc_ref[rows, :] += (gate(h, 2) / ow[rows, LANES:]) * ow[rows, :LANES]

    blk = lax.broadcasted_iota(jnp.int32, (tq, LANES), 1)
    cur = t_col >> 6
    forced = (blk == 0) | (blk == cur) | (blk == cur - 1)
    valid = blk <= cur
    score = jnp.where(valid & jnp.logical_not(forced), imp, NEG_INF)
    st = score.T
    blk_t = lax.broadcasted_iota(jnp.int32, (LANES, tq), 0).astype(F32)
    sel_t = jnp.zeros((LANES, tq), F32)
    for _ in range(SEL_TOPK - N_FORCED):
        mx = jnp.max(st, axis=0, keepdims=True)
        idx = jnp.min(jnp.where(st == mx, blk_t, float(LANES)), axis=0, keepdims=True)
        hit = blk_t == idx
        sel_t = jnp.where(hit, 1.0, sel_t)
        st = jnp.where(hit, -jnp.inf, st)
    sel = (sel_t.T > 0.5) | forced
    bias_s = jnp.where(sel & valid, 0.0, NEG_INF).astype(BF16)

    for h in range(hg):
        qaug_ref[h * tq:(h + 1) * tq, 0:LANES] = bias_s
        qaug_ref[h * tq:(h + 1) * tq, LANES:2 * LANES] = qpair(h)
    m_ref[...] = jnp.full(m_ref.shape, NEG_INF, F32)
    l_ref[...] = jnp.zeros(l_ref.shape, F32)
    acc_ref[...] = jnp.zeros(acc_ref.shape, F32)

    n_lc = tk // LANES

    def scores(kt, rows):
        start = pl.multiple_of(kt * tk, tk)
        return _nt_dot(qaug_ref[rows, :], ks_ref[pl.ds(start, tk), :])

    def softmax_pv(kt, causal):
        start = pl.multiple_of(kt * tk, tk)
        if causal:
            kpos = start + lax.broadcasted_iota(jnp.int32, (1, tk), 1)
            bias_d = jnp.where(kpos <= t_col, 0.0, NEG_INF)
        for p in range(n_chunks):
            rows = slice(p * rc, (p + 1) * rc)
            s = s_ref[rows, :]
            if causal:
                s = jnp.concatenate([s[r * tq:(r + 1) * tq] + bias_d for r in range(reps)], axis=0)
            chunks = [s[:, c * LANES:(c + 1) * LANES] for c in range(n_lc)]
            mx = functools.reduce(jnp.maximum, chunks)
            m_old = m_ref[rows, :]
            m_new = jnp.maximum(m_old, jnp.max(mx, axis=1, keepdims=True))
            alpha = jnp.exp2(m_old - m_new)
            p_bf = jnp.concatenate([jnp.exp2(c - m_new) for c in chunks], axis=1).astype(BF16)
            pv = _dot(p_bf, vs_ref[pl.ds(start, tk), :])
            acc_ref[rows, :] = alpha * acc_ref[rows, :] + pv[:, :LANES]
            l_ref[rows, :] = alpha * l_ref[rows, :] + pv[:, LANES:]
            m_ref[rows, :] = m_new
            if not causal:
                s_ref[rows, :] = scores(kt + 1, rows)

    k_diag = s0 // tk
    s_ref[...] = scores(0, slice(None))

    unroll = SEL_UNROLL

    def tiles(j, carry):
        for u in range(unroll):
            softmax_pv(unroll * j + u, False)
        return carry

    n_loop = k_diag // unroll
    lax.fori_loop(0, n_loop, tiles, 0)
    for u in range(unroll - 1):

        @pl.when(n_loop * unroll + u < k_diag)
        def _():
            softmax_pv(n_loop * unroll + u, False)

    softmax_pv(k_diag, True)

    outs = []
    for h in range(hg):
        rows = slice(h * tq, (h + 1) * tq)
        outs.append(oacc_ref[rows, :] + (gate(h, 1) / l_ref[rows, :]) * acc_ref[rows, :])
    lane = lax.broadcasted_iota(jnp.int32, (tq, LANES), 1)
    low = lane < HEAD_DIM
    is_g0 = g == 0
    for p in range(hg // 2):
        a = outs[2 * p]
        b = outs[2 * p + 1]
        xa = jnp.where(is_g0, a, pltpu.roll(a, HEAD_DIM, 1))
        xb = jnp.where(is_g0, pltpu.roll(b, HEAD_DIM, 1), b)
        o_ref[:, p * LANES:(p + 1) * LANES] = jnp.where(low, xa, xb).astype(BF16)


def _nsa_attn(q, qsw, ks, kw, vs, vw, kco, vco, gates, overlap):
    B, T, _ = q.shape
    tq, tk = TQ, TK_SEL
    n_cmp_pad = kco.shape[2]
    n_blk = T // SEL_BLOCK
    gw = HEADS_PER_GROUP * HEAD_DIM
    rows = HEADS_PER_GROUP * tq
    grid = (B, N_GROUPS, T // tq)
    once = pl.Buffered(1)
    return pl.pallas_call(
        functools.partial(_attn_body, tq=tq, tk=tk, n_cmp_pad=n_cmp_pad, n_blk=n_blk),
        grid=grid,
        in_specs=[
            pl.BlockSpec((None, tq, gw), lambda b, g, i: (b, i, g)),
            pl.BlockSpec((None, tq, gw), lambda b, g, i: (b, i, g)),
            pl.BlockSpec((None, None, T, 2 * LANES), lambda b, g, i: (b, g, 0, 0), pipeline_mode=once),
            pl.BlockSpec((None, None, T, LANES), lambda b, g, i: (b, g, 0, 0), pipeline_mode=once),
            pl.BlockSpec((None, T, 2 * LANES), lambda b, g, i: (b, 0, 0), pipeline_mode=once),
            pl.BlockSpec((None, T, 2 * LANES), lambda b, g, i: (b, 0, 0), pipeline_mode=once),
            pl.BlockSpec((None, None, n_cmp_pad, LANES), lambda b, g, i: (b, g, 0, 0)),
            pl.BlockSpec((None, n_cmp_pad, LANES), lambda b, g, i: (b, 0, 0)),
            pl.BlockSpec((None, None, tq, LANES), lambda b, g, i: (b, g, i, 0)),
            pl.BlockSpec(overlap.shape, lambda b, g, i: (0, 0)),
        ],
        out_specs=pl.BlockSpec((None, tq, gw), lambda b, g, i: (b, i, g)),
        out_shape=jax.ShapeDtypeStruct((B, T, N_HEADS * HEAD_DIM), BF16),
        scratch_shapes=[pltpu.VMEM((rows, 2 * LANES), BF16),
                        pltpu.VMEM((rows, tk), F32),
                        pltpu.VMEM((rows, LANES), F32),
                        pltpu.VMEM((rows, LANES), F32),
                        pltpu.VMEM((rows, LANES), F32),
                        pltpu.VMEM((rows, LANES), F32)],
        compiler_params=pltpu.CompilerParams(
            dimension_semantics=("parallel", "parallel", "arbitrary"),
            vmem_limit_bytes=VMEM_LIMIT),
        name="nsa_attn",
    )(q, qsw, ks, kw, vs, vw, kco, vco, gates, overlap)


def _mixer_out_proj(x_ref, xh_ref, a_ref, ah_ref, wo_ref, xin_ref, *, tm):
    halo = HALO_FFN
    lead = ah_ref.shape[0]
    a_ext = jnp.concatenate([ah_ref[...], a_ref[...]], axis=0)
    y = _dot(a_ext, wo_ref[...])
    xin_ref[0:halo, :] = xh_ref[...] + y[lead - halo:lead]
    xin_ref[halo:halo + tm, :] = x_ref[...] + y[lead:lead + tm]


def _mixer_pool(x_ref, xh_ref, gm_ref, pw_ref, pb_ref, ps_ref, xin_ref, *, tm, seq_tile):
    halo = HALO_FFN
    lead = xh_ref.shape[0]
    n_ext = lead + tm
    xe = jnp.concatenate([xh_ref[...], x_ref[...]], axis=0)
    row = lax.broadcasted_iota(jnp.int32, (n_ext, 1), 0)
    before_seq = (seq_tile == 0) & (row < lead)
    hm = jnp.where(before_seq, 0.0, _rmsnorm(xe, gm_ref[...]))
    t_seq = seq_tile * tm + row - lead
    gdim = xe.shape[1] // len(POOL_WINDOWS)
    ys = []
    for gi, w in enumerate(POOL_WINDOWS):
        cols = slice(gi * gdim, (gi + 1) * gdim)
        hg = hm[:, cols]
        s = hg
        shift = 1
        while shift < w:
            s = s + pltpu.roll(s, shift, 0)
            shift *= 2
        cnt = jnp.clip(t_seq + 1, 1, w).astype(F32)
        ys.append(_dot((s / cnt - hg).astype(BF16), pw_ref[gi]))
    y = (jnp.concatenate(ys, axis=1) + pb_ref[...]) * ps_ref[...]
    xin_ref[...] = (xe + y)[lead - halo:n_ext]


def _ffn_body(*refs, tm, tf, dff, tiles_per_seq, final_norm, mixer):
    n_mix = {"out_proj": 5, "pool": 6}[mixer]
    mix_refs, refs = refs[:n_mix], refs[n_mix:]
    (g_ref, wup_ref, cw_ref, cb_ref, wd_ref, gf_ref, o_ref,
     h_ref, acc_ref, xin_ref, ug0_ref, uv0_ref, ug1_ref, uv1_ref) = refs
    u_refs = ((ug0_ref, uv0_ref), (ug1_ref, uv1_ref))
    i = pl.program_id(0)
    halo = HALO_FFN
    n = halo + tm
    seq_tile = i % tiles_per_seq
    if mixer == "out_proj":
        _mixer_out_proj(*mix_refs, xin_ref, tm=tm)
    else:
        _mixer_pool(*mix_refs, xin_ref, tm=tm, seq_tile=seq_tile)
    hn = _rmsnorm(xin_ref[...], g_ref[...])
    row = lax.broadcasted_iota(jnp.int32, (n, 1), 0)
    no_context = (seq_tile == 0) & (row < halo)
    h_ref[...] = jnp.where(no_context, 0.0, hn).astype(BF16)

    nj = dff // tf
    rb = FFN_ROW_BLOCK
    n_rb = tm // rb

    def up_proj(j, half, b):
        col = half * dff + j * tf
        r0 = 0 if b == 0 else halo + b * rb
        r1 = halo + (b + 1) * rb
        u_refs[j % 2][half][r0:r1, :] = _dot(h_ref[r0:r1, :], wup_ref[:, col:col + tf])

    def conv(j, half, b):
        col = half * dff + j * tf
        u = u_refs[j % 2][half][b * rb:halo + (b + 1) * rb, :]
        u1 = pltpu.roll(u, 1, 0)
        u2 = pltpu.roll(u, 2, 0)
        cw = cw_ref[:, col:col + tf]
        c = cw[0:1, :] * u2 + cw[1:2, :] * u1 + cw[2:3, :] * u + cb_ref[:, col:col + tf]
        return c[halo:halo + rb]

    for half in range(2):
        for b in range(n_rb):
            up_proj(0, half, b)
    for j in range(nj):
        for b in range(n_rb):
            rows = slice(b * rb, (b + 1) * rb)
            if j + 1 < nj:
                up_proj(j + 1, 0, b)
            cg = conv(j, 0, b)
            if j + 1 < nj:
                up_proj(j + 1, 1, b)
            cv = conv(j, 1, b)
            a = (cg * (1.0 / (1.0 + jnp.exp(-cg)))) * cv
            d = _dot(a.astype(BF16), wd_ref[j * tf:(j + 1) * tf, :])
            if j == 0:
                acc_ref[rows, :] = d
            else:
                acc_ref[rows, :] += d

    y = xin_ref[halo:n, :] + acc_ref[...]
    if final_norm:
        y = _rmsnorm(y, gf_ref[...])
    o_ref[...] = y


def _mixer_ffn(x2, seq_len, mixer, mixer_args, g, w_up, conv_w, conv_b, w_down, g_final, final_norm):
    N, D = x2.shape
    dff = w_down.shape[0]
    tm, tf = TM_FFN, TF_FFN
    resident = lambda a: pl.BlockSpec(a.shape, lambda i: (0,) * a.ndim, pipeline_mode=pl.Buffered(1))
    tile = pl.BlockSpec((tm, D), lambda i: (i, 0))

    def left_context(rows):
        return pl.BlockSpec((rows, D), lambda i: (jnp.maximum(i * (tm // rows) - 1, 0), 0))

    if mixer == "out_proj":
        attn, w_out = mixer_args
        mix_in = [x2, x2, attn, attn, w_out]
        mix_specs = [tile, left_context(HALO_FFN), tile, left_context(HALO_ATTN_BF16), resident(w_out)]
    else:
        mix_in = [x2, x2] + list(mixer_args)
        mix_specs = [tile, left_context(HALO_POOL)] + [resident(a) for a in mixer_args]
    weights = [g, w_up, conv_w, conv_b, w_down, g_final]
    return pl.pallas_call(
        functools.partial(_ffn_body, tm=tm, tf=tf, dff=dff, tiles_per_seq=seq_len // tm,
                          final_norm=final_norm, mixer=mixer),
        grid=(N // tm,),
        in_specs=mix_specs + [resident(a) for a in weights],
        out_specs=tile,
        out_shape=jax.ShapeDtypeStruct((N, D), F32),
        scratch_shapes=[pltpu.VMEM((HALO_FFN + tm, D), BF16),
                        pltpu.VMEM((tm, D), F32),
                        pltpu.VMEM((HALO_FFN + tm, D), F32),
                        ] + [pltpu.VMEM((HALO_FFN + tm, tf), F32)] * 4,
        compiler_params=pltpu.CompilerParams(dimension_semantics=("parallel",),
                                             vmem_limit_bytes=VMEM_LIMIT),
        name="mixer_ffn_" + mixer,
    )(*mix_in, *weights)


def _pad_cols(w, width):
    return jnp.pad(w, ((0, 0), (0, width - w.shape[1])))


def _inproj_weight(w_in):
    D = w_in.shape[0]
    kv = N_GROUPS * HEAD_DIM
    sizes = [N_HEADS * HEAD_DIM] + [kv] * 6 + [N_BRANCH * N_HEADS]
    offs = np.concatenate([[0], np.cumsum(sizes)])
    q, k_c, v_c, k_s, v_s, k_w, v_w, gt = [w_in[:, offs[n]:offs[n + 1]] for n in range(8)]
    per_group = lambda w, n: [_pad_cols(w[:, g * n:(g + 1) * n], LANES) for g in range(N_GROUPS)]
    cols = ([q] + per_group(k_s, HEAD_DIM) + per_group(k_w, HEAD_DIM) + [k_c, v_c, v_s, v_w]
            + per_group(gt, N_BRANCH * HEADS_PER_GROUP))
    return jnp.concatenate(cols, axis=1).astype(BF16)


def _rope_tables(pos):
    inv = 1.0 / (ROPE_THETA ** (jnp.arange(0, HEAD_DIM, 2, dtype=F32) / HEAD_DIM))
    ang = pos.astype(F32)[:, None] * inv[None, :]
    return jnp.cos(ang), jnp.sin(ang)


def _expand_w1(w1):
    hdim = w1.shape[1]
    halves = w1.reshape(2, CMP_STRIDE, HEAD_DIM, hdim)
    zeros = jnp.zeros((CMP_STRIDE, HEAD_DIM, hdim), w1.dtype)
    cols = []
    for half in range(2):
        for g in range(N_GROUPS):
            parts = [halves[half] if gg == g else zeros for gg in range(N_GROUPS)]
            cols.append(jnp.concatenate(parts, axis=1).reshape(CMP_STRIDE * N_GROUPS * HEAD_DIM, hdim))
    return jnp.concatenate(cols, axis=1).astype(BF16)


def _rot_half_cols(w):
    half = HEAD_DIM // 2
    return jnp.concatenate([-w[..., half:], w[..., :half]], axis=-1)


def _overlap_matrix(n_cmp_pad, n_blk):
    j = np.arange(n_cmp_pad)[:, None]
    s = np.arange(LANES)[None, :]
    lo = np.maximum(j * CMP_STRIDE, s * SEL_BLOCK)
    hi = np.minimum(j * CMP_STRIDE + CMP_BLOCK, (s + 1) * SEL_BLOCK)
    return jnp.asarray(np.clip(hi - lo, 0, None) / CMP_BLOCK, dtype=BF16)


def kernel(x, norm_mix_0, nsa_w_in, cmp_k_pos, cmp_k_w1, cmp_k_b1, cmp_k_w2, cmp_k_b2, cmp_v_pos, cmp_v_w1, cmp_v_b1, cmp_v_w2, cmp_v_b2, nsa_w_out, norm_ffn_0, ffn_up_0, ffn_conv_w_0, ffn_conv_b_0, ffn_down_0, norm_mix_1, pool_w, pool_b, pool_scale, norm_ffn_1, ffn_up_1, ffn_conv_w_1, ffn_conv_b_1, ffn_down_1, norm_final):
    B, T, D = x.shape
    assert D == N_HEADS * HEAD_DIM and SEL_BLOCK == 64
    assert T % TM_FFN == 0 and T % TK_SEL == 0 and TK_SEL % TQ == 0 and T >= WINDOW + TQ
    n_cmp_pad = T // CMP_STRIDE
    n_blk = T // SEL_BLOCK
    assert n_blk <= LANES
    row = lambda v: v.reshape(1, -1)

    cos, sin = _rope_tables(jnp.arange(T))
    cos4 = jnp.tile(cos, (1, 4))
    sin4 = jnp.tile(jnp.concatenate([-sin, sin], axis=1), (1, 2))
    q, qsw, ks, kw, kc, vc, vs, vw, gates = _in_proj(
        x, row(norm_mix_0), _inproj_weight(nsa_w_in), cos4, sin4)

    ccos, csin = _rope_tables(jnp.arange(n_cmp_pad) * CMP_STRIDE + (CMP_BLOCK - 1))
    zeros64 = jnp.zeros((n_cmp_pad, HEAD_DIM), F32)
    ccos2 = jnp.concatenate([ccos, ccos, zeros64], axis=1)
    csin2 = jnp.concatenate([csin, csin, zeros64], axis=1)
    pos_rows = lambda p: jnp.broadcast_to(p.reshape(1, -1), (SUBLANES, p.size)).astype(BF16)
    zero_w2 = jnp.zeros_like(cmp_v_w2)
    kparams = (_expand_w1(cmp_k_w1), cmp_k_w1.astype(BF16), pos_rows(cmp_k_pos), row(cmp_k_b1),
               jnp.concatenate([cmp_k_w2, _rot_half_cols(cmp_k_w2)], axis=1).astype(BF16),
               row(jnp.concatenate([cmp_k_b2, _rot_half_cols(cmp_k_b2)])))
    vparams = (_expand_w1(cmp_v_w1), cmp_v_w1.astype(BF16), pos_rows(cmp_v_pos), row(cmp_v_b1),
               jnp.concatenate([cmp_v_w2, zero_w2], axis=1).astype(BF16),
               jnp.concatenate([zero_w2, cmp_v_w2], axis=1).astype(BF16),
               row(jnp.concatenate([cmp_v_b2, cmp_v_b2])))
    width = CMP_STRIDE * N_GROUPS * HEAD_DIM
    kco, vco = _compress(kc.reshape(B, n_cmp_pad, width), vc.reshape(B, n_cmp_pad, width),
                         kparams, vparams, ccos2, csin2)

    o = _nsa_attn(q, qsw, ks, kw, vs, vw, kco, vco, gates, _overlap_matrix(n_cmp_pad, n_blk))

    x2 = _mixer_ffn(x.reshape(B * T, D), T, "out_proj",
                    (o.reshape(B * T, D), nsa_w_out.astype(BF16)),
                    row(norm_ffn_0), ffn_up_0.astype(BF16), ffn_conv_w_0, row(ffn_conv_b_0),
                    ffn_down_0.astype(BF16), row(norm_final), False)
    x2 = _mixer_ffn(x2, T, "pool",
                    (row(norm_mix_1), pool_w.astype(BF16), row(pool_b.reshape(-1)), row(pool_scale)),
                    row(norm_ffn_1), ffn_up_1.astype(BF16), ffn_conv_w_1, row(ffn_conv_b_1),
                    ffn_down_1.astype(BF16), row(norm_final), True)
    return x2.reshape(B, T, D)
```

```python
import functools

import numpy as np
import jax
import jax.numpy as jnp
from jax import lax
from jax.experimental import pallas as pl
from jax.experimental.pallas import tpu as pltpu

F32 = jnp.float32
BF16 = jnp.bfloat16

N_HEADS = 16
HEAD_DIM = 64
N_GROUPS = 2
HEADS_PER_GROUP = N_HEADS // N_GROUPS
N_BRANCH = 3
CMP_STRIDE = 16
CMP_BLOCK = 32
SEL_BLOCK = 64
SEL_TOPK = 16
WINDOW = 512
ROPE_THETA = 10000.0
POOL_WINDOWS = (2, 4, 8, 16)
CONV_WIDTH = 3
NORM_EPS = 1e-6
NEG_INF = -1e30
FORCE_BONUS = 1e4
N_FORCED = 3
LOG2_E = 1.4426950408889634

LANES = 128
SUBLANES = 8
VMEM_LIMIT = 56 * 1024 * 1024

TM_PROJ = 512
TQ = 256
TK_SEL = 512
SEL_UNROLL = 2
CMP_VARIANTS = 2
SEL_ROW_CHUNK = 512
TM_FFN = 512
TF_FFN = 256
FFN_ROW_BLOCK = 256
HALO_FFN = 8
HALO_ATTN_BF16 = 16
HALO_POOL = 32


def _rmsnorm(x, g):
    return x * lax.rsqrt(jnp.mean(x * x, axis=-1, keepdims=True) + NORM_EPS) * g


def _nt_dot(a, b):
    return lax.dot_general(a, b, (((1,), (1,)), ((), ())), preferred_element_type=F32)


def _dot(a, b):
    return jnp.dot(a, b, preferred_element_type=F32)


def _inproj_body(x_ref, g_ref, w_ref, cos_ref, sin_ref,
                 q_ref, qsw_ref, ks_ref, kw_ref, kc_ref, vc_ref, vs_ref, vw_ref, gate_ref, *, tm):
    i = pl.program_id(1)
    h = _rmsnorm(x_ref[...], g_ref[...]).astype(BF16)
    res = _dot(h, w_ref[...])
    cos = cos_ref[...]
    sin = sin_ref[...]
    lane = lax.broadcasted_iota(jnp.int32, (tm, LANES), 1)
    first_half = (lane & (HEAD_DIM - 1)) < (HEAD_DIM // 2)

    def rope(xc):
        sw = jnp.where(first_half, pltpu.roll(xc, LANES - HEAD_DIM // 2, 1),
                       pltpu.roll(xc, HEAD_DIM // 2, 1))
        return xc * cos + sw * sin

    scale = HEAD_DIM ** -0.5 * LOG2_E
    n_pairs = N_HEADS // 2
    for p in range(n_pairs):
        qc = rope(res[:, p * LANES:(p + 1) * LANES]) * scale
        q_ref[:, p * LANES:(p + 1) * LANES] = qc.astype(BF16)
        qsw_ref[:, p * LANES:(p + 1) * LANES] = pltpu.roll(qc, HEAD_DIM, 1).astype(BF16)
    off = n_pairs * LANES
    row_t = i * tm + lax.broadcasted_iota(jnp.int32, (tm, LANES), 0)
    onehot = jnp.where(lane == (row_t >> 6), 1.0, 0.0).astype(BF16)
    for g in range(N_GROUPS):
        ks_ref[g, :, 0:LANES] = onehot
        ks_ref[g, :, LANES:2 * LANES] = rope(res[:, off + g * LANES: off + (g + 1) * LANES]).astype(BF16)
    off += N_GROUPS * LANES
    for g in range(N_GROUPS):
        kw_ref[g] = rope(res[:, off + g * LANES: off + (g + 1) * LANES]).astype(BF16)
    off += N_GROUPS * LANES
    for g in range(N_GROUPS):
        kc_ref[g] = res[:, off + g * HEAD_DIM:off + (g + 1) * HEAD_DIM]
        vc_ref[g] = res[:, off + LANES + g * HEAD_DIM:off + LANES + (g + 1) * HEAD_DIM]
    ones = jnp.ones((tm, LANES), BF16)
    vs_ref[:, 0:LANES] = res[:, off + 2 * LANES:off + 3 * LANES].astype(BF16)
    vs_ref[:, LANES:2 * LANES] = ones
    vw_ref[:, 0:LANES] = res[:, off + 3 * LANES:off + 4 * LANES].astype(BF16)
    vw_ref[:, LANES:2 * LANES] = ones
    off += 4 * LANES
    for g in range(N_GROUPS):
        z = res[:, off + g * LANES: off + (g + 1) * LANES]
        gate_ref[g] = 1.0 / (1.0 + jnp.exp(-z))


def _in_proj(x, g, wp, cos4, sin4):
    B, T, D = x.shape
    tm = TM_PROJ
    ncols = wp.shape[1]
    grid = (B, T // tm)
    tok = lambda last: pl.BlockSpec((None, tm, last), lambda b, i: (b, i, 0))
    grp = lambda last: pl.BlockSpec((None, N_GROUPS, tm, last), lambda b, i: (b, 0, i, 0))
    out_shape = (
        jax.ShapeDtypeStruct((B, T, N_HEADS * HEAD_DIM), BF16),
        jax.ShapeDtypeStruct((B, T, N_HEADS * HEAD_DIM), BF16),
        jax.ShapeDtypeStruct((B, N_GROUPS, T, 2 * LANES), BF16),
        jax.ShapeDtypeStruct((B, N_GROUPS, T, LANES), BF16),
        jax.ShapeDtypeStruct((B, N_GROUPS, T, HEAD_DIM), F32),
        jax.ShapeDtypeStruct((B, N_GROUPS, T, HEAD_DIM), F32),
        jax.ShapeDtypeStruct((B, T, 2 * LANES), BF16),
        jax.ShapeDtypeStruct((B, T, 2 * LANES), BF16),
        jax.ShapeDtypeStruct((B, N_GROUPS, T, LANES), F32),
    )
    return pl.pallas_call(
        functools.partial(_inproj_body, tm=tm),
        grid=grid,
        in_specs=[tok(D),
                  pl.BlockSpec((1, D), lambda b, i: (0, 0)),
                  pl.BlockSpec((D, ncols), lambda b, i: (0, 0)),
                  pl.BlockSpec((tm, LANES), lambda b, i: (i, 0)),
                  pl.BlockSpec((tm, LANES), lambda b, i: (i, 0))],
        out_specs=(tok(N_HEADS * HEAD_DIM), tok(N_HEADS * HEAD_DIM), grp(2 * LANES), grp(LANES),
                   grp(HEAD_DIM), grp(HEAD_DIM), tok(2 * LANES), tok(2 * LANES), grp(LANES)),
        out_shape=out_shape,
        compiler_params=pltpu.CompilerParams(dimension_semantics=("parallel", "parallel"),
                                             vmem_limit_bytes=VMEM_LIMIT),
        name="in_proj",
    )(x, g, wp, cos4, sin4)


def _gelu_tanh(x):
    return 0.5 * x * (1.0 + jnp.tanh(np.sqrt(2.0 / np.pi).astype(np.float32) * (x + 0.044715 * (x * x * x))))


def _compress_body(kc_ref, vc_ref,
                   kw1_ref, kpos_ref, kb1_ref, kw2_ref, kb2_ref,
                   vw1_ref, vpos_ref, vb1_ref, vw2a_ref, vw2b_ref, vb2_ref,
                   cos_ref, sin_ref, kco_ref, vco_ref, *, n_rows, half):
    row = lax.broadcasted_iota(jnp.int32, (n_rows, LANES), 0)
    keep = row < (n_rows - 1)

    def hidden_act(c_ref, w1_ref, pos_ref, b1_ref):
        w1 = w1_ref[...]
        posb = _dot(pos_ref[...], w1)[0:1, :] + b1_ref[...]
        outs = []
        for g in range(N_GROUPS):
            c = c_ref[g].astype(BF16)
            a = _dot(c, w1[0:half])
            b = _dot(c, w1[half:2 * half])
            pre = a + pltpu.roll(b, n_rows - 1, 0) + posb
            outs.append(_gelu_tanh(pre).astype(BF16))
        return outs

    hk = hidden_act(kc_ref, kw1_ref, kpos_ref, kb1_ref)
    cos = cos_ref[...]
    sin = sin_ref[...]
    for g in range(N_GROUPS):
        r = _dot(hk[g], kw2_ref[...]) + kb2_ref[...]
        kco_ref[g] = jnp.where(keep, r * cos + pltpu.roll(r, HEAD_DIM, 1) * sin, 0.0).astype(BF16)
    hv = hidden_act(vc_ref, vw1_ref, vpos_ref, vb1_ref)
    v = _dot(hv[0], vw2a_ref[...]) + _dot(hv[1], vw2b_ref[...]) + vb2_ref[...]
    vco_ref[...] = jnp.where(keep, v, 0.0).astype(BF16)


def _compress(kc2, vc2, kparams, vparams, ccos, csin):
    B, _, n_rows, width = kc2.shape
    full = lambda a: pl.BlockSpec(a.shape, lambda b: (0,) * a.ndim)
    blk = pl.BlockSpec((None, N_GROUPS, n_rows, width), lambda b: (b, 0, 0, 0))
    weights = list(kparams) + list(vparams) + [ccos, csin]
    return pl.pallas_call(
        functools.partial(_compress_body, n_rows=n_rows, half=width),
        grid=(B,),
        in_specs=[blk, blk] + [full(a) for a in weights],
        out_specs=(pl.BlockSpec((None, N_GROUPS, n_rows, LANES), lambda b: (b, 0, 0, 0)),
                   pl.BlockSpec((None, n_rows, LANES), lambda b: (b, 0, 0))),
        out_shape=(jax.ShapeDtypeStruct((B, N_GROUPS, n_rows, LANES), BF16),
                   jax.ShapeDtypeStruct((B, n_rows, LANES), BF16)),
        compiler_params=pltpu.CompilerParams(dimension_semantics=("parallel",),
                                             vmem_limit_bytes=VMEM_LIMIT),
        name="compress",
    )(kc2, vc2, *weights)


def _softmax_rows(s):
    m = jnp.max(s, axis=1, keepdims=True)
    e = jnp.exp2(s - m)
    return e, jnp.sum(e, axis=1, keepdims=True)


def _attn_body(q_ref, qsw_ref, ks_ref, kw_ref, vs_ref, vw_ref, kc_ref, vc_ref, gate_ref, ov_ref,
               o_ref, qaug_ref, s_ref, m_ref, l_ref, acc_ref, oacc_ref, *, tq, tk, n_cmp_pad, n_blk):
    hg = HEADS_PER_GROUP
    g = pl.program_id(1)
    i = pl.program_id(2)
    s0 = i * tq
    t_col = s0 + lax.broadcasted_iota(jnp.int32, (tq, 1), 0)

    rc = SEL_ROW_CHUNK
    n_chunks = hg * tq // rc
    reps = rc // tq

    def qpair(h):
        src = q_ref if h % 2 == 0 else qsw_ref
        return src[:, (h // 2) * LANES:(h // 2 + 1) * LANES]

    gates = gate_ref[...]

    def gate(h, branch):
        c = N_BRANCH * h + branch
        return gates[:, c:c + 1]

    n_lc = tk // LANES

    def scores(kt, rows):
        start = pl.multiple_of(kt * tk, tk)
        return _nt_dot(qaug_ref[rows, :], ks_ref[pl.ds(start, tk), :])

    def softmax_pv(kt, causal):
        start = pl.multiple_of(kt * tk, tk)
        if causal:
            kpos = start + lax.broadcasted_iota(jnp.int32, (1, tk), 1)
            bias_d = jnp.where(kpos <= t_col, 0.0, NEG_INF)
        for p in range(n_chunks):
            rows = slice(p * rc, (p + 1) * rc)
            s = s_ref[rows, :]
            if causal:
                s = jnp.concatenate([s[r * tq:(r + 1) * tq] + bias_d for r in range(reps)], axis=0)
            chunks = [s[:, c * LANES:(c + 1) * LANES] for c in range(n_lc)]
            mx = functools.reduce(jnp.maximum, chunks)
            m_old = m_ref[rows, :]
            m_new = jnp.maximum(m_old, jnp.max(mx, axis=1, keepdims=True))
            alpha = jnp.exp2(m_old - m_new)
            p_bf = jnp.concatenate([jnp.exp2(c - m_new) for c in chunks], axis=1).astype(BF16)
            pv = _dot(p_bf, vs_ref[pl.ds(start, tk), :])
            acc_ref[rows, :] = alpha * acc_ref[rows, :] + pv[:, :LANES]
            l_ref[rows, :] = alpha * l_ref[rows, :] + pv[:, LANES:]
            m_ref[rows, :] = m_new
            if not causal:
                s_ref[rows, :] = scores(kt + 1, rows)

    def pre_loop(nk):
        j_row = lax.broadcasted_iota(jnp.int32, (1, nk), 1)
        bias_c = jnp.where(j_row * CMP_STRIDE + (CMP_BLOCK - 1) <= t_col, 0.0, NEG_INF)
        vis = jnp.where(t_col >= CMP_BLOCK - 1, 1.0, 0.0)
        q_all = jnp.concatenate([qpair(h) for h in range(hg)], axis=0)
        sc = _nt_dot(q_all, kc_ref[0:nk, :]).reshape(hg, tq, nk) + bias_c[None]
        e = jnp.exp2(sc - jnp.max(sc, axis=2, keepdims=True))
        r_c = vis[None] / jnp.sum(e, axis=2, keepdims=True)
        rhs_c = jnp.concatenate([vc_ref[0:nk, :], ov_ref[0:nk, :]], axis=1)
        pvc = _dot(e.reshape(hg * tq, nk).astype(BF16), rhs_c)
        imp = jnp.zeros((tq, LANES), F32)
        for h in range(hg):
            rows = slice(h * tq, (h + 1) * tq)
            oacc_ref[rows, :] = (gate(h, 0) * r_c[h]) * pvc[rows, :LANES]
            imp = imp + r_c[h] * pvc[rows, LANES:]

        n_win = WINDOW + tq
        kstart = pl.multiple_of(jnp.maximum(s0 - WINDOW, 0), tq)
        kpos_w = kstart + lax.broadcasted_iota(jnp.int32, (1, n_win), 1)
        bias_w = jnp.where((kpos_w <= t_col) & (kpos_w > t_col - WINDOW), 0.0, NEG_INF)
        sw = _nt_dot(q_all, kw_ref[pl.ds(kstart, n_win), :]).reshape(hg, tq, n_win) + bias_w[None]
        ew = jnp.exp2(sw - jnp.max(sw, axis=2, keepdims=True))
        ow = _dot(ew.reshape(hg * tq, n_win).astype(BF16), vw_ref[pl.ds(kstart, n_win), :])
        for h in range(hg):
            rows = slice(h * tq, (h + 1) * tq)
            oacc_ref[rows, :] += (gate(h, 2) / ow[rows, LANES:]) * ow[rows, :LANES]

        blk = lax.broadcasted_iota(jnp.int32, (tq, LANES), 1)
        cur = t_col >> 6
        forced = (blk == 0) | (blk == cur) | (blk == cur - 1)
        valid = blk <= cur
        score = jnp.where(valid & jnp.logical_not(forced), imp, NEG_INF)
        st = score.T
        blk_t = lax.broadcasted_iota(jnp.int32, (LANES, tq), 0).astype(F32)
        sel_t = jnp.zeros((LANES, tq), F32)
        for _ in range(SEL_TOPK - N_FORCED):
            mx = jnp.max(st, axis=0, keepdims=True)
            idx = jnp.min(jnp.where(st == mx, blk_t, float(LANES)), axis=0, keepdims=True)
            hit = blk_t == idx
            sel_t = jnp.where(hit, 1.0, sel_t)
            st = jnp.where(hit, -jnp.inf, st)
        sel = (sel_t.T > 0.5) | forced
        bias_s = jnp.where(sel & valid, 0.0, NEG_INF).astype(BF16)

        for h in range(hg):
            qaug_ref[h * tq:(h + 1) * tq, 0:LANES] = bias_s
            qaug_ref[h * tq:(h + 1) * tq, LANES:2 * LANES] = qpair(h)
        m_ref[...] = jnp.full(m_ref.shape, NEG_INF, F32)
        l_ref[...] = jnp.zeros(l_ref.shape, F32)
        acc_ref[...] = jnp.zeros(acc_ref.shape, F32)
        s_ref[...] = scores(0, slice(None))

    n_vis = (s0 + tq) // CMP_STRIDE - 1
    step = n_cmp_pad // CMP_VARIANTS
    for v in range(CMP_VARIANTS):
        lo, hi = v * step, (v + 1) * step

        @pl.when((n_vis > lo) & (n_vis <= hi) if v else n_vis <= hi)
        def _():
            pre_loop(hi)

    k_diag = s0 // tk

    unroll = SEL_UNROLL

    def tiles(j, carry):
        for u in range(unroll):
            softmax_pv(unroll * j + u, False)
        return carry

    n_loop = k_diag // unroll
    lax.fori_loop(0, n_loop, tiles, 0)
    for u in range(unroll - 1):

        @pl.when(n_loop * unroll + u < k_diag)
        def _():
            softmax_pv(n_loop * unroll + u, False)

    softmax_pv(k_diag, True)

    outs = []
    for h in range(hg):
        rows = slice(h * tq, (h + 1) * tq)
        outs.append(oacc_ref[rows, :] + (gate(h, 1) / l_ref[rows, :]) * acc_ref[rows, :])
    lane = lax.broadcasted_iota(jnp.int32, (tq, LANES), 1)
    low = lane < HEAD_DIM
    is_g0 = g == 0
    for p in range(hg // 2):
        a = outs[2 * p]
        b = outs[2 * p + 1]
        xa = jnp.where(is_g0, a, pltpu.roll(a, HEAD_DIM, 1))
        xb = jnp.where(is_g0, pltpu.roll(b, HEAD_DIM, 1), b)
        o_ref[:, p * LANES:(p + 1) * LANES] = jnp.where(low, xa, xb).astype(BF16)


def _nsa_attn(q, qsw, ks, kw, vs, vw, kco, vco, gates, overlap):
    B, T, _ = q.shape
    tq, tk = TQ, TK_SEL
    n_cmp_pad = kco.shape[2]
    n_blk = T // SEL_BLOCK
    gw = HEADS_PER_GROUP * HEAD_DIM
    rows = HEADS_PER_GROUP * tq
    grid = (B, N_GROUPS, T // tq)
    once = pl.Buffered(1)
    return pl.pallas_call(
        functools.partial(_attn_body, tq=tq, tk=tk, n_cmp_pad=n_cmp_pad, n_blk=n_blk),
        grid=grid,
        in_specs=[
            pl.BlockSpec((None, tq, gw), lambda b, g, i: (b, i, g)),
            pl.BlockSpec((None, tq, gw), lambda b, g, i: (b, i, g)),
            pl.BlockSpec((None, None, T, 2 * LANES), lambda b, g, i: (b, g, 0, 0), pipeline_mode=once),
            pl.BlockSpec((None, None, T, LANES), lambda b, g, i: (b, g, 0, 0), pipeline_mode=once),
            pl.BlockSpec((None, T, 2 * LANES), lambda b, g, i: (b, 0, 0), pipeline_mode=once),
            pl.BlockSpec((None, T, 2 * LANES), lambda b, g, i: (b, 0, 0), pipeline_mode=once),
            pl.BlockSpec((None, None, n_cmp_pad, LANES), lambda b, g, i: (b, g, 0, 0)),
            pl.BlockSpec((None, n_cmp_pad, LANES), lambda b, g, i: (b, 0, 0)),
            pl.BlockSpec((None, None, tq, LANES), lambda b, g, i: (b, g, i, 0)),
            pl.BlockSpec(overlap.shape, lambda b, g, i: (0, 0)),
        ],
        out_specs=pl.BlockSpec((None, tq, gw), lambda b, g, i: (b, i, g)),
        out_shape=jax.ShapeDtypeStruct((B, T, N_HEADS * HEAD_DIM), BF16),
        scratch_shapes=[pltpu.VMEM((rows, 2 * LANES), BF16),
                        pltpu.VMEM((rows, tk), F32),
                        pltpu.VMEM((rows, LANES), F32),
                        pltpu.VMEM((rows, LANES), F32),
                        pltpu.VMEM((rows, LANES), F32),
                        pltpu.VMEM((rows, LANES), F32)],
        compiler_params=pltpu.CompilerParams(
            dimension_semantics=("parallel", "parallel", "arbitrary"),
            vmem_limit_bytes=VMEM_LIMIT),
        name="nsa_attn",
    )(q, qsw, ks, kw, vs, vw, kco, vco, gates, overlap)


def _mixer_out_proj(x_ref, xh_ref, a_ref, ah_ref, wo_ref, xin_ref, *, tm):
    halo = HALO_FFN
    lead = ah_ref.shape[0]
    a_ext = jnp.concatenate([ah_ref[...], a_ref[...]], axis=0)
    y = _dot(a_ext, wo_ref[...])
    xin_ref[0:halo, :] = xh_ref[...] + y[lead - halo:lead]
    xin_ref[halo:halo + tm, :] = x_ref[...] + y[lead:lead + tm]


def _mixer_pool(x_ref, xh_ref, gm_ref, pw_ref, pb_ref, ps_ref, xin_ref, *, tm, seq_tile):
    halo = HALO_FFN
    lead = xh_ref.shape[0]
    n_ext = lead + tm
    xe = jnp.concatenate([xh_ref[...], x_ref[...]], axis=0)
    row = lax.broadcasted_iota(jnp.int32, (n_ext, 1), 0)
    before_seq = (seq_tile == 0) & (row < lead)
    hm = jnp.where(before_seq, 0.0, _rmsnorm(xe, gm_ref[...]))
    t_seq = seq_tile * tm + row - lead
    gdim = xe.shape[1] // len(POOL_WINDOWS)
    ys = []
    for gi, w in enumerate(POOL_WINDOWS):
        cols = slice(gi * gdim, (gi + 1) * gdim)
        hg = hm[:, cols]
        s = hg
        shift = 1
        while shift < w:
            s = s + pltpu.roll(s, shift, 0)
            shift *= 2
        cnt = jnp.clip(t_seq + 1, 1, w).astype(F32)
        ys.append(_dot((s / cnt - hg).astype(BF16), pw_ref[gi]))
    y = (jnp.concatenate(ys, axis=1) + pb_ref[...]) * ps_ref[...]
    xin_ref[...] = (xe + y)[lead - halo:n_ext]


def _ffn_body(*refs, tm, tf, dff, tiles_per_seq, final_norm, mixer):
    n_mix = {"out_proj": 5, "pool": 6}[mixer]
    mix_refs, refs = refs[:n_mix], refs[n_mix:]
    (g_ref, wup_ref, cw_ref, cb_ref, wd_ref, gf_ref, o_ref,
     h_ref, acc_ref, xin_ref, ug0_ref, uv0_ref, ug1_ref, uv1_ref) = refs
    u_refs = ((ug0_ref, uv0_ref), (ug1_ref, uv1_ref))
    i = pl.program_id(0)
    halo = HALO_FFN
    n = halo + tm
    seq_tile = i % tiles_per_seq
    if mixer == "out_proj":
        _mixer_out_proj(*mix_refs, xin_ref, tm=tm)
    else:
        _mixer_pool(*mix_refs, xin_ref, tm=tm, seq_tile=seq_tile)
    hn = _rmsnorm(xin_ref[...], g_ref[...])
    row = lax.broadcasted_iota(jnp.int32, (n, 1), 0)
    no_context = (seq_tile == 0) & (row < halo)
    h_ref[...] = jnp.where(no_context, 0.0, hn).astype(BF16)

    nj = dff // tf
    rb = FFN_ROW_BLOCK
    n_rb = tm // rb

    def up_proj(j, half, b):
        col = half * dff + j * tf
        r0 = 0 if b == 0 else halo + b * rb
        r1 = halo + (b + 1) * rb
        u_refs[j % 2][half][r0:r1, :] = _dot(h_ref[r0:r1, :], wup_ref[:, col:col + tf])

    def conv(j, half, b):
        col = half * dff + j * tf
        u = u_refs[j % 2][half][b * rb:halo + (b + 1) * rb, :]
        u1 = pltpu.roll(u, 1, 0)
        u2 = pltpu.roll(u, 2, 0)
        cw = cw_ref[:, col:col + tf]
        c = cw[0:1, :] * u2 + cw[1:2, :] * u1 + cw[2:3, :] * u + cb_ref[:, col:col + tf]
        return c[halo:halo + rb]

    for half in range(2):
        for b in range(n_rb):
            up_proj(0, half, b)
    for j in range(nj):
        for b in range(n_rb):
            rows = slice(b * rb, (b + 1) * rb)
            if j + 1 < nj:
                up_proj(j + 1, 0, b)
            cg = conv(j, 0, b)
            if j + 1 < nj:
                up_proj(j + 1, 1, b)
            cv = conv(j, 1, b)
            a = (cg * (1.0 / (1.0 + jnp.exp(-cg)))) * cv
            d = _dot(a.astype(BF16), wd_ref[j * tf:(j + 1) * tf, :])
            if j == 0:
                acc_ref[rows, :] = d
            else:
                acc_ref[rows, :] += d

    y = xin_ref[halo:n, :] + acc_ref[...]
    if final_norm:
        y = _rmsnorm(y, gf_ref[...])
    o_ref[...] = y


def _mixer_ffn(x2, seq_len, mixer, mixer_args, g, w_up, conv_w, conv_b, w_down, g_final, final_norm):
    N, D = x2.shape
    dff = w_down.shape[0]
    tm, tf = TM_FFN, TF_FFN
    resident = lambda a: pl.BlockSpec(a.shape, lambda i: (0,) * a.ndim, pipeline_mode=pl.Buffered(1))
    tile = pl.BlockSpec((tm, D), lambda i: (i, 0))

    def left_context(rows):
        return pl.BlockSpec((rows, D), lambda i: (jnp.maximum(i * (tm // rows) - 1, 0), 0))

    if mixer == "out_proj":
        attn, w_out = mixer_args
        mix_in = [x2, x2, attn, attn, w_out]
        mix_specs = [tile, left_context(HALO_FFN), tile, left_context(HALO_ATTN_BF16), resident(w_out)]
    else:
        mix_in = [x2, x2] + list(mixer_args)
        mix_specs = [tile, left_context(HALO_POOL)] + [resident(a) for a in mixer_args]
    weights = [g, w_up, conv_w, conv_b, w_down, g_final]
    return pl.pallas_call(
        functools.partial(_ffn_body, tm=tm, tf=tf, dff=dff, tiles_per_seq=seq_len // tm,
                          final_norm=final_norm, mixer=mixer),
        grid=(N // tm,),
        in_specs=mix_specs + [resident(a) for a in weights],
        out_specs=tile,
        out_shape=jax.ShapeDtypeStruct((N, D), F32),
        scratch_shapes=[pltpu.VMEM((HALO_FFN + tm, D), BF16),
                        pltpu.VMEM((tm, D), F32),
                        pltpu.VMEM((HALO_FFN + tm, D), F32),
                        ] + [pltpu.VMEM((HALO_FFN + tm, tf), F32)] * 4,
        compiler_params=pltpu.CompilerParams(dimension_semantics=("parallel",),
                                             vmem_limit_bytes=VMEM_LIMIT),
        name="mixer_ffn_" + mixer,
    )(*mix_in, *weights)


def _pad_cols(w, width):
    return jnp.pad(w, ((0, 0), (0, width - w.shape[1])))


def _inproj_weight(w_in):
    D = w_in.shape[0]
    kv = N_GROUPS * HEAD_DIM
    sizes = [N_HEADS * HEAD_DIM] + [kv] * 6 + [N_BRANCH * N_HEADS]
    offs = np.concatenate([[0], np.cumsum(sizes)])
    q, k_c, v_c, k_s, v_s, k_w, v_w, gt = [w_in[:, offs[n]:offs[n + 1]] for n in range(8)]
    per_group = lambda w, n: [_pad_cols(w[:, g * n:(g + 1) * n], LANES) for g in range(N_GROUPS)]
    cols = ([q] + per_group(k_s, HEAD_DIM) + per_group(k_w, HEAD_DIM) + [k_c, v_c, v_s, v_w]
            + per_group(gt, N_BRANCH * HEADS_PER_GROUP))
    return jnp.concatenate(cols, axis=1).astype(BF16)


def _rope_tables(pos):
    inv = 1.0 / (ROPE_THETA ** (jnp.arange(0, HEAD_DIM, 2, dtype=F32) / HEAD_DIM))
    ang = pos.astype(F32)[:, None] * inv[None, :]
    return jnp.cos(ang), jnp.sin(ang)


def _rot_half_cols(w):
    half = HEAD_DIM // 2
    return jnp.concatenate([-w[..., half:], w[..., :half]], axis=-1)


def _overlap_matrix(n_cmp_pad, n_blk):
    j = np.arange(n_cmp_pad)[:, None]
    s = np.arange(LANES)[None, :]
    lo = np.maximum(j * CMP_STRIDE, s * SEL_BLOCK)
    hi = np.minimum(j * CMP_STRIDE + CMP_BLOCK, (s + 1) * SEL_BLOCK)
    return jnp.asarray(np.clip(hi - lo, 0, None) / CMP_BLOCK, dtype=BF16)


def kernel(x, norm_mix_0, nsa_w_in, cmp_k_pos, cmp_k_w1, cmp_k_b1, cmp_k_w2, cmp_k_b2, cmp_v_pos, cmp_v_w1, cmp_v_b1, cmp_v_w2, cmp_v_b2, nsa_w_out, norm_ffn_0, ffn_up_0, ffn_conv_w_0, ffn_conv_b_0, ffn_down_0, norm_mix_1, pool_w, pool_b, pool_scale, norm_ffn_1, ffn_up_1, ffn_conv_w_1, ffn_conv_b_1, ffn_down_1, norm_final):
    B, T, D = x.shape
    assert D == N_HEADS * HEAD_DIM and SEL_BLOCK == 64
    assert T % TM_FFN == 0 and T % TK_SEL == 0 and TK_SEL % TQ == 0 and T >= WINDOW + TQ
    n_cmp_pad = T // CMP_STRIDE
    n_blk = T // SEL_BLOCK
    assert n_blk <= LANES
    row = lambda v: v.reshape(1, -1)

    cos, sin = _rope_tables(jnp.arange(T))
    cos4 = jnp.tile(cos, (1, 4))
    sin4 = jnp.tile(jnp.concatenate([-sin, sin], axis=1), (1, 2))
    q, qsw, ks, kw, kc, vc, vs, vw, gates = _in_proj(
        x, row(norm_mix_0), _inproj_weight(nsa_w_in), cos4, sin4)

    ccos, csin = _rope_tables(jnp.arange(n_cmp_pad) * CMP_STRIDE + (CMP_BLOCK - 1))
    zeros64 = jnp.zeros((n_cmp_pad, HEAD_DIM), F32)
    ccos2 = jnp.concatenate([ccos, ccos, zeros64], axis=1)
    csin2 = jnp.concatenate([csin, csin, zeros64], axis=1)
    pos_rows = lambda p: jnp.broadcast_to(p.reshape(1, -1), (SUBLANES, p.size)).astype(BF16)
    zero_w2 = jnp.zeros_like(cmp_v_w2)
    kparams = (cmp_k_w1.astype(BF16), pos_rows(cmp_k_pos), row(cmp_k_b1),
               jnp.concatenate([cmp_k_w2, _rot_half_cols(cmp_k_w2)], axis=1).astype(BF16),
               row(jnp.concatenate([cmp_k_b2, _rot_half_cols(cmp_k_b2)])))
    vparams = (cmp_v_w1.astype(BF16), pos_rows(cmp_v_pos), row(cmp_v_b1),
               jnp.concatenate([cmp_v_w2, zero_w2], axis=1).astype(BF16),
               jnp.concatenate([zero_w2, cmp_v_w2], axis=1).astype(BF16),
               row(jnp.concatenate([cmp_v_b2, cmp_v_b2])))
    width = CMP_STRIDE * HEAD_DIM
    kco, vco = _compress(kc.reshape(B, N_GROUPS, n_cmp_pad, width),
                         vc.reshape(B, N_GROUPS, n_cmp_pad, width), kparams, vparams, ccos2, csin2)

    o = _nsa_attn(q, qsw, ks, kw, vs, vw, kco, vco, gates, _overlap_matrix(n_cmp_pad, n_blk))

    x2 = _mixer_ffn(x.reshape(B * T, D), T, "out_proj",
                    (o.reshape(B * T, D), nsa_w_out.astype(BF16)),
                    row(norm_ffn_0), ffn_up_0.astype(BF16), ffn_conv_w_0, row(ffn_conv_b_0),
                    ffn_down_0.astype(BF16), row(norm_final), False)
    x2 = _mixer_ffn(x2, T, "pool",
                    (row(norm_mix_1), pool_w.astype(BF16), row(pool_b.reshape(-1)), row(pool_scale)),
                    row(norm_ffn_1), ffn_up_1.astype(BF16), ffn_conv_w_1, row(ffn_conv_b_1),
                    ffn_down_1.astype(BF16), row(norm_final), True)
    return x2.reshape(B, T, D)
```

```python
import functools

import numpy as np
import jax
import jax.numpy as jnp
from jax import lax
from jax.experimental import pallas as pl
from jax.experimental.pallas import tpu as pltpu

F32 = jnp.float32
BF16 = jnp.bfloat16

N_HEADS = 16
HEAD_DIM = 64
N_GROUPS = 2
HEADS_PER_GROUP = N_HEADS // N_GROUPS
N_BRANCH = 3
CMP_STRIDE = 16
CMP_BLOCK = 32
SEL_BLOCK = 64
SEL_TOPK = 16
WINDOW = 512
ROPE_THETA = 10000.0
POOL_WINDOWS = (2, 4, 8, 16)
CONV_WIDTH = 3
NORM_EPS = 1e-6
NEG_INF = -1e30
FORCE_BONUS = 1e4
N_FORCED = 3
LOG2_E = 1.4426950408889634

LANES = 128
SUBLANES = 8
VMEM_LIMIT = 56 * 1024 * 1024

TM_PROJ = 512
TQ = 256
TK_SEL = 512
SEL_UNROLL = 2
CMP_VARIANTS = 4
SEL_ROW_CHUNK = 512
TM_FFN = 512
TF_FFN = 256
FFN_ROW_BLOCK = 256
HALO_FFN = 8
HALO_ATTN_BF16 = 16
HALO_POOL = 32


def _rmsnorm(x, g):
    return x * lax.rsqrt(jnp.mean(x * x, axis=-1, keepdims=True) + NORM_EPS) * g


def _nt_dot(a, b):
    return lax.dot_general(a, b, (((1,), (1,)), ((), ())), preferred_element_type=F32)


def _dot(a, b):
    return jnp.dot(a, b, preferred_element_type=F32)


def _inproj_body(x_ref, g_ref, w_ref, cos_ref, sin_ref,
                 q_ref, qsw_ref, ks_ref, kw_ref, kc_ref, vc_ref, vs_ref, vw_ref, gate_ref, *, tm):
    i = pl.program_id(1)
    h = _rmsnorm(x_ref[...], g_ref[...]).astype(BF16)
    res = _dot(h, w_ref[...])
    cos = cos_ref[...]
    sin = sin_ref[...]
    lane = lax.broadcasted_iota(jnp.int32, (tm, LANES), 1)
    first_half = (lane & (HEAD_DIM - 1)) < (HEAD_DIM // 2)

    def rope(xc):
        sw = jnp.where(first_half, pltpu.roll(xc, LANES - HEAD_DIM // 2, 1),
                       pltpu.roll(xc, HEAD_DIM // 2, 1))
        return xc * cos + sw * sin

    scale = HEAD_DIM ** -0.5 * LOG2_E
    n_pairs = N_HEADS // 2
    for p in range(n_pairs):
        qc = rope(res[:, p * LANES:(p + 1) * LANES]) * scale
        q_ref[:, p * LANES:(p + 1) * LANES] = qc.astype(BF16)
        qsw_ref[:, p * LANES:(p + 1) * LANES] = pltpu.roll(qc, HEAD_DIM, 1).astype(BF16)
    off = n_pairs * LANES
    row_t = i * tm + lax.broadcasted_iota(jnp.int32, (tm, LANES), 0)
    onehot = jnp.where(lane == (row_t >> 6), 1.0, 0.0).astype(BF16)
    for g in range(N_GROUPS):
        ks_ref[g, :, 0:LANES] = onehot
        ks_ref[g, :, LANES:2 * LANES] = rope(res[:, off + g * LANES: off + (g + 1) * LANES]).astype(BF16)
    off += N_GROUPS * LANES
    for g in range(N_GROUPS):
        kw_ref[g] = rope(res[:, off + g * LANES: off + (g + 1) * LANES]).astype(BF16)
    off += N_GROUPS * LANES
    for g in range(N_GROUPS):
        kc_ref[g] = res[:, off + g * HEAD_DIM:off + (g + 1) * HEAD_DIM]
        vc_ref[g] = res[:, off + LANES + g * HEAD_DIM:off + LANES + (g + 1) * HEAD_DIM]
    ones = jnp.ones((tm, LANES), BF16)
    vs_ref[:, 0:LANES] = res[:, off + 2 * LANES:off + 3 * LANES].astype(BF16)
    vs_ref[:, LANES:2 * LANES] = ones
    vw_ref[:, 0:LANES] = res[:, off + 3 * LANES:off + 4 * LANES].astype(BF16)
    vw_ref[:, LANES:2 * LANES] = ones
    off += 4 * LANES
    for g in range(N_GROUPS):
        z = res[:, off + g * LANES: off + (g + 1) * LANES]
        gate_ref[g] = 1.0 / (1.0 + jnp.exp(-z))


def _in_proj(x, g, wp, cos4, sin4):
    B, T, D = x.shape
    tm = TM_PROJ
    ncols = wp.shape[1]
    grid = (B, T // tm)
    tok = lambda last: pl.BlockSpec((None, tm, last), lambda b, i: (b, i, 0))
    grp = lambda last: pl.BlockSpec((None, N_GROUPS, tm, last), lambda b, i: (b, 0, i, 0))
    out_shape = (
        jax.ShapeDtypeStruct((B, T, N_HEADS * HEAD_DIM), BF16),
        jax.ShapeDtypeStruct((B, T, N_HEADS * HEAD_DIM), BF16),
        jax.ShapeDtypeStruct((B, N_GROUPS, T, 2 * LANES), BF16),
        jax.ShapeDtypeStruct((B, N_GROUPS, T, LANES), BF16),
        jax.ShapeDtypeStruct((B, N_GROUPS, T, HEAD_DIM), F32),
        jax.ShapeDtypeStruct((B, N_GROUPS, T, HEAD_DIM), F32),
        jax.ShapeDtypeStruct((B, T, 2 * LANES), BF16),
        jax.ShapeDtypeStruct((B, T, 2 * LANES), BF16),
        jax.ShapeDtypeStruct((B, N_GROUPS, T, LANES), F32),
    )
    return pl.pallas_call(
        functools.partial(_inproj_body, tm=tm),
        grid=grid,
        in_specs=[tok(D),
                  pl.BlockSpec((1, D), lambda b, i: (0, 0)),
                  pl.BlockSpec((D, ncols), lambda b, i: (0, 0)),
                  pl.BlockSpec((tm, LANES), lambda b, i: (i, 0)),
                  pl.BlockSpec((tm, LANES), lambda b, i: (i, 0))],
        out_specs=(tok(N_HEADS * HEAD_DIM), tok(N_HEADS * HEAD_DIM), grp(2 * LANES), grp(LANES),
                   grp(HEAD_DIM), grp(HEAD_DIM), tok(2 * LANES), tok(2 * LANES), grp(LANES)),
        out_shape=out_shape,
        compiler_params=pltpu.CompilerParams(dimension_semantics=("parallel", "parallel"),
                                             vmem_limit_bytes=VMEM_LIMIT),
        name="in_proj",
    )(x, g, wp, cos4, sin4)


def _gelu_tanh(x):
    return 0.5 * x * (1.0 + jnp.tanh(np.sqrt(2.0 / np.pi).astype(np.float32) * (x + 0.044715 * (x * x * x))))


def _compress_body(kc_ref, vc_ref,
                   kw1_ref, kpos_ref, kb1_ref, kw2_ref, kb2_ref,
                   vw1_ref, vpos_ref, vb1_ref, vw2a_ref, vw2b_ref, vb2_ref,
                   cos_ref, sin_ref, kco_ref, vco_ref, *, n_rows, half):
    row = lax.broadcasted_iota(jnp.int32, (n_rows, LANES), 0)
    keep = row < (n_rows - 1)

    def hidden_act(c_ref, w1_ref, pos_ref, b1_ref):
        posb = _dot(pos_ref[...], w1_ref[...])[0:1, :] + b1_ref[...]
        outs = []
        for g in range(N_GROUPS):
            a = jnp.zeros((n_rows, w1_ref.shape[1]), F32)
            b = jnp.zeros((n_rows, w1_ref.shape[1]), F32)
            for t in range(CMP_STRIDE):
                c = c_ref[g, pl.ds(t, n_rows, stride=CMP_STRIDE), :].astype(BF16)
                a = a + _dot(c, w1_ref[t * HEAD_DIM:(t + 1) * HEAD_DIM, :])
                b = b + _dot(c, w1_ref[half + t * HEAD_DIM:half + (t + 1) * HEAD_DIM, :])
            pre = a + pltpu.roll(b, n_rows - 1, 0) + posb
            outs.append(_gelu_tanh(pre).astype(BF16))
        return outs

    hk = hidden_act(kc_ref, kw1_ref, kpos_ref, kb1_ref)
    cos = cos_ref[...]
    sin = sin_ref[...]
    for g in range(N_GROUPS):
        r = _dot(hk[g], kw2_ref[...]) + kb2_ref[...]
        kco_ref[g] = jnp.where(keep, r * cos + pltpu.roll(r, HEAD_DIM, 1) * sin, 0.0).astype(BF16)
    hv = hidden_act(vc_ref, vw1_ref, vpos_ref, vb1_ref)
    v = _dot(hv[0], vw2a_ref[...]) + _dot(hv[1], vw2b_ref[...]) + vb2_ref[...]
    vco_ref[...] = jnp.where(keep, v, 0.0).astype(BF16)


def _compress(kc2, vc2, kparams, vparams, ccos, csin):
    B, _, T, dh = kc2.shape
    n_rows = T // CMP_STRIDE
    full = lambda a: pl.BlockSpec(a.shape, lambda b: (0,) * a.ndim)
    blk = pl.BlockSpec((None, N_GROUPS, T, dh), lambda b: (b, 0, 0, 0), pipeline_mode=pl.Buffered(1))
    weights = list(kparams) + list(vparams) + [ccos, csin]
    return pl.pallas_call(
        functools.partial(_compress_body, n_rows=n_rows, half=CMP_STRIDE * dh),
        grid=(B,),
        in_specs=[blk, blk] + [full(a) for a in weights],
        out_specs=(pl.BlockSpec((None, N_GROUPS, n_rows, LANES), lambda b: (b, 0, 0, 0)),
                   pl.BlockSpec((None, n_rows, LANES), lambda b: (b, 0, 0))),
        out_shape=(jax.ShapeDtypeStruct((B, N_GROUPS, n_rows, LANES), BF16),
                   jax.ShapeDtypeStruct((B, n_rows, LANES), BF16)),
        compiler_params=pltpu.CompilerParams(dimension_semantics=("parallel",),
                                             vmem_limit_bytes=VMEM_LIMIT),
        name="compress",
    )(kc2, vc2, *weights)


def _softmax_rows(s):
    m = jnp.max(s, axis=1, keepdims=True)
    e = jnp.exp2(s - m)
    return e, jnp.sum(e, axis=1, keepdims=True)


def _attn_body(q_ref, qsw_ref, ks_ref, kw_ref, vs_ref, vw_ref, kc_ref, vc_ref, gate_ref, ov_ref,
               o_ref, qaug_ref, s_ref, m_ref, l_ref, acc_ref, oacc_ref, *, tq, tk, n_cmp_pad, n_blk):
    hg = HEADS_PER_GROUP
    g = pl.program_id(1)
    i = pl.program_id(2)
    s0 = i * tq
    t_col = s0 + lax.broadcasted_iota(jnp.int32, (tq, 1), 0)

    rc = SEL_ROW_CHUNK
    n_chunks = hg * tq // rc
    reps = rc // tq

    def qpair(h):
        src = q_ref if h % 2 == 0 else qsw_ref
        return src[:, (h // 2) * LANES:(h // 2 + 1) * LANES]

    gates = gate_ref[...]

    def gate(h, branch):
        c = N_BRANCH * h + branch
        return gates[:, c:c + 1]

    n_lc = tk // LANES

    def scores(kt, rows):
        start = pl.multiple_of(kt * tk, tk)
        return _nt_dot(qaug_ref[rows, :], ks_ref[pl.ds(start, tk), :])

    def softmax_pv(kt, causal):
        start = pl.multiple_of(kt * tk, tk)
        if causal:
            kpos = start + lax.broadcasted_iota(jnp.int32, (1, tk), 1)
            bias_d = jnp.where(kpos <= t_col, 0.0, NEG_INF)
        for p in range(n_chunks):
            rows = slice(p * rc, (p + 1) * rc)
            s = s_ref[rows, :]
            if causal:
                s = jnp.concatenate([s[r * tq:(r + 1) * tq] + bias_d for r in range(reps)], axis=0)
            chunks = [s[:, c * LANES:(c + 1) * LANES] for c in range(n_lc)]
            mx = functools.reduce(jnp.maximum, chunks)
            m_old = m_ref[rows, :]
            m_new = jnp.maximum(m_old, jnp.max(mx, axis=1, keepdims=True))
            alpha = jnp.exp2(m_old - m_new)
            p_bf = jnp.concatenate([jnp.exp2(c - m_new) for c in chunks], axis=1).astype(BF16)
            pv = _dot(p_bf, vs_ref[pl.ds(start, tk), :])
            acc_ref[rows, :] = alpha * acc_ref[rows, :] + pv[:, :LANES]
            l_ref[rows, :] = alpha * l_ref[rows, :] + pv[:, LANES:]
            m_ref[rows, :] = m_new
            if not causal:
                s_ref[rows, :] = scores(kt + 1, rows)

    def pre_loop(nk):
        j_row = lax.broadcasted_iota(jnp.int32, (1, nk), 1)
        bias_c = jnp.where(j_row * CMP_STRIDE + (CMP_BLOCK - 1) <= t_col, 0.0, NEG_INF)
        vis = jnp.where(t_col >= CMP_BLOCK - 1, 1.0, 0.0)
        q_all = jnp.concatenate([qpair(h) for h in range(hg)], axis=0)
        sc = _nt_dot(q_all, kc_ref[0:nk, :]).reshape(hg, tq, nk) + bias_c[None]
        e = jnp.exp2(sc - jnp.max(sc, axis=2, keepdims=True))
        r_c = vis[None] / jnp.sum(e, axis=2, keepdims=True)
        rhs_c = jnp.concatenate([vc_ref[0:nk, :], ov_ref[0:nk, :]], axis=1)
        pvc = _dot(e.reshape(hg * tq, nk).astype(BF16), rhs_c)
        imp = jnp.zeros((tq, LANES), F32)
        for h in range(hg):
            rows = slice(h * tq, (h + 1) * tq)
            oacc_ref[rows, :] = (gate(h, 0) * r_c[h]) * pvc[rows, :LANES]
            imp = imp + r_c[h] * pvc[rows, LANES:]

        n_win = WINDOW + tq
        kstart = pl.multiple_of(jnp.maximum(s0 - WINDOW, 0), tq)
        kpos_w = kstart + lax.broadcasted_iota(jnp.int32, (1, n_win), 1)
        bias_w = jnp.where((kpos_w <= t_col) & (kpos_w > t_col - WINDOW), 0.0, NEG_INF)
        sw = _nt_dot(q_all, kw_ref[pl.ds(kstart, n_win), :]).reshape(hg, tq, n_win) + bias_w[None]
        ew = jnp.exp2(sw - jnp.max(sw, axis=2, keepdims=True))
        ow = _dot(ew.reshape(hg * tq, n_win).astype(BF16), vw_ref[pl.ds(kstart, n_win), :])
        for h in range(hg):
            rows = slice(h * tq, (h + 1) * tq)
            oacc_ref[rows, :] += (gate(h, 2) / ow[rows, LANES:]) * ow[rows, :LANES]

        blk = lax.broadcasted_iota(jnp.int32, (tq, LANES), 1)
        cur = t_col >> 6
        forced = (blk == 0) | (blk == cur) | (blk == cur - 1)
        valid = blk <= cur
        score = jnp.where(valid & jnp.logical_not(forced), imp, NEG_INF)
        st = score.T
        blk_t = lax.broadcasted_iota(jnp.int32, (LANES, tq), 0).astype(F32)
        sel_t = jnp.zeros((LANES, tq), F32)
        for _ in range(SEL_TOPK - N_FORCED):
            mx = jnp.max(st, axis=0, keepdims=True)
            idx = jnp.min(jnp.where(st == mx, blk_t, float(LANES)), axis=0, keepdims=True)
            hit = blk_t == idx
            sel_t = jnp.where(hit, 1.0, sel_t)
            st = jnp.where(hit, -jnp.inf, st)
        sel = (sel_t.T > 0.5) | forced
        bias_s = jnp.where(sel & valid, 0.0, NEG_INF).astype(BF16)

        for h in range(hg):
            qaug_ref[h * tq:(h + 1) * tq, 0:LANES] = bias_s
            qaug_ref[h * tq:(h + 1) * tq, LANES:2 * LANES] = qpair(h)
        m_ref[...] = jnp.full(m_ref.shape, NEG_INF, F32)
        l_ref[...] = jnp.zeros(l_ref.shape, F32)
        acc_ref[...] = jnp.zeros(acc_ref.shape, F32)
        s_ref[...] = scores(0, slice(None))

    n_vis = (s0 + tq) // CMP_STRIDE - 1
    step = n_cmp_pad // CMP_VARIANTS
    for v in range(CMP_VARIANTS):
        lo, hi = v * step, (v + 1) * step

        @pl.when((n_vis > lo) & (n_vis <= hi) if v else n_vis <= hi)
        def _():
            pre_loop(hi)

    k_diag = s0 // tk

    unroll = SEL_UNROLL

    def tiles(j, carry):
        for u in range(unroll):
            softmax_pv(unroll * j + u, False)
        return carry

    n_loop = k_diag // unroll
    lax.fori_loop(0, n_loop, tiles, 0)
    for u in range(unroll - 1):

        @pl.when(n_loop * unroll + u < k_diag)
        def _():
            softmax_pv(n_loop * unroll + u, False)

    softmax_pv(k_diag, True)

    outs = []
    for h in range(hg):
        rows = slice(h * tq, (h + 1) * tq)
        outs.append(oacc_ref[rows, :] + (gate(h, 1) / l_ref[rows, :]) * acc_ref[rows, :])
    lane = lax.broadcasted_iota(jnp.int32, (tq, LANES), 1)
    low = lane < HEAD_DIM
    is_g0 = g == 0
    for p in range(hg // 2):
        a = outs[2 * p]
        b = outs[2 * p + 1]
        xa = jnp.where(is_g0, a, pltpu.roll(a, HEAD_DIM, 1))
        xb = jnp.where(is_g0, pltpu.roll(b, HEAD_DIM, 1), b)
        o_ref[:, p * LANES:(p + 1) * LANES] = jnp.where(low, xa, xb).astype(BF16)


def _nsa_attn(q, qsw, ks, kw, vs, vw, kco, vco, gates, overlap):
    B, T, _ = q.shape
    tq, tk = TQ, TK_SEL
    n_cmp_pad = kco.shape[2]
    n_blk = T // SEL_BLOCK
    gw = HEADS_PER_GROUP * HEAD_DIM
    rows = HEADS_PER_GROUP * tq
    grid = (B, N_GROUPS, T // tq)
    once = pl.Buffered(1)
    return pl.pallas_call(
        functools.partial(_attn_body, tq=tq, tk=tk, n_cmp_pad=n_cmp_pad, n_blk=n_blk),
        grid=grid,
        in_specs=[
            pl.BlockSpec((None, tq, gw), lambda b, g, i: (b, i, g)),
            pl.BlockSpec((None, tq, gw), lambda b, g, i: (b, i, g)),
            pl.BlockSpec((None, None, T, 2 * LANES), lambda b, g, i: (b, g, 0, 0), pipeline_mode=once),
            pl.BlockSpec((None, None, T, LANES), lambda b, g, i: (b, g, 0, 0), pipeline_mode=once),
            pl.BlockSpec((None, T, 2 * LANES), lambda b, g, i: (b, 0, 0), pipeline_mode=once),
            pl.BlockSpec((None, T, 2 * LANES), lambda b, g, i: (b, 0, 0), pipeline_mode=once),
            pl.BlockSpec((None, None, n_cmp_pad, LANES), lambda b, g, i: (b, g, 0, 0)),
            pl.BlockSpec((None, n_cmp_pad, LANES), lambda b, g, i: (b, 0, 0)),
            pl.BlockSpec((None, None, tq, LANES), lambda b, g, i: (b, g, i, 0)),
            pl.BlockSpec(overlap.shape, lambda b, g, i: (0, 0)),
        ],
        out_specs=pl.BlockSpec((None, tq, gw), lambda b, g, i: (b, i, g)),
        out_shape=jax.ShapeDtypeStruct((B, T, N_HEADS * HEAD_DIM), BF16),
        scratch_shapes=[pltpu.VMEM((rows, 2 * LANES), BF16),
                        pltpu.VMEM((rows, tk), F32),
                        pltpu.VMEM((rows, LANES), F32),
                        pltpu.VMEM((rows, LANES), F32),
                        pltpu.VMEM((rows, LANES), F32),
                        pltpu.VMEM((rows, LANES), F32)],
        compiler_params=pltpu.CompilerParams(
            dimension_semantics=("parallel", "parallel", "arbitrary"),
            vmem_limit_bytes=VMEM_LIMIT),
        name="nsa_attn",
    )(q, qsw, ks, kw, vs, vw, kco, vco, gates, overlap)


def _mixer_out_proj(x_ref, xh_ref, a_ref, ah_ref, wo_ref, xin_ref, *, tm):
    halo = HALO_FFN
    lead = ah_ref.shape[0]
    a_ext = jnp.concatenate([ah_ref[...], a_ref[...]], axis=0)
    y = _dot(a_ext, wo_ref[...])
    xin_ref[0:halo, :] = xh_ref[...] + y[lead - halo:lead]
    xin_ref[halo:halo + tm, :] = x_ref[...] + y[lead:lead + tm]


def _mixer_pool(x_ref, xh_ref, gm_ref, pw_ref, pb_ref, ps_ref, xin_ref, *, tm, seq_tile):
    halo = HALO_FFN
    lead = xh_ref.shape[0]
    n_ext = lead + tm
    xe = jnp.concatenate([xh_ref[...], x_ref[...]], axis=0)
    row = lax.broadcasted_iota(jnp.int32, (n_ext, 1), 0)
    before_seq = (seq_tile == 0) & (row < lead)
    hm = jnp.where(before_seq, 0.0, _rmsnorm(xe, gm_ref[...]))
    t_seq = seq_tile * tm + row - lead
    gdim = xe.shape[1] // len(POOL_WINDOWS)
    ys = []
    for gi, w in enumerate(POOL_WINDOWS):
        cols = slice(gi * gdim, (gi + 1) * gdim)
        hg = hm[:, cols]
        s = hg
        shift = 1
        while shift < w:
            s = s + pltpu.roll(s, shift, 0)
            shift *= 2
        cnt = jnp.clip(t_seq + 1, 1, w).astype(F32)
        ys.append(_dot((s / cnt - hg).astype(BF16), pw_ref[gi]))
    y = (jnp.concatenate(ys, axis=1) + pb_ref[...]) * ps_ref[...]
    xin_ref[...] = (xe + y)[lead - halo:n_ext]


def _ffn_body(*refs, tm, tf, dff, tiles_per_seq, final_norm, mixer):
    n_mix = {"out_proj": 5, "pool": 6}[mixer]
    mix_refs, refs = refs[:n_mix], refs[n_mix:]
    (g_ref, wup_ref, cw_ref, cb_ref, wd_ref, gf_ref, o_ref,
     h_ref, acc_ref, xin_ref, ug0_ref, uv0_ref, ug1_ref, uv1_ref) = refs
    u_refs = ((ug0_ref, uv0_ref), (ug1_ref, uv1_ref))
    i = pl.program_id(0)
    halo = HALO_FFN
    n = halo + tm
    seq_tile = i % tiles_per_seq
    if mixer == "out_proj":
        _mixer_out_proj(*mix_refs, xin_ref, tm=tm)
    else:
        _mixer_pool(*mix_refs, xin_ref, tm=tm, seq_tile=seq_tile)
    hn = _rmsnorm(xin_ref[...], g_ref[...])
    row = lax.broadcasted_iota(jnp.int32, (n, 1), 0)
    no_context = (seq_tile == 0) & (row < halo)
    h_ref[...] = jnp.where(no_context, 0.0, hn).astype(BF16)

    nj = dff // tf
    rb = FFN_ROW_BLOCK
    n_rb = tm // rb

    def up_proj(j, half, b):
        col = half * dff + j * tf
        r0 = 0 if b == 0 else halo + b * rb
        r1 = halo + (b + 1) * rb
        u_refs[j % 2][half][r0:r1, :] = _dot(h_ref[r0:r1, :], wup_ref[:, col:col + tf])

    def conv(j, half, b):
        col = half * dff + j * tf
        u = u_refs[j % 2][half][b * rb:halo + (b + 1) * rb, :]
        u1 = pltpu.roll(u, 1, 0)
        u2 = pltpu.roll(u, 2, 0)
        cw = cw_ref[:, col:col + tf]
        c = cw[0:1, :] * u2 + cw[1:2, :] * u1 + cw[2:3, :] * u + cb_ref[:, col:col + tf]
        return c[halo:halo + rb]

    for half in range(2):
        for b in range(n_rb):
            up_proj(0, half, b)
    for j in range(nj):
        for b in range(n_rb):
            rows = slice(b * rb, (b + 1) * rb)
            if j + 1 < nj:
                up_proj(j + 1, 0, b)
            cg = conv(j, 0, b)
            if j + 1 < nj:
                up_proj(j + 1, 1, b)
            cv = conv(j, 1, b)
            a = (cg * (1.0 / (1.0 + jnp.exp(-cg)))) * cv
            d = _dot(a.astype(BF16), wd_ref[j * tf:(j + 1) * tf, :])
            if j == 0:
                acc_ref[rows, :] = d
            else:
                acc_ref[rows, :] += d

    y = xin_ref[halo:n, :] + acc_ref[...]
    if final_norm:
        y = _rmsnorm(y, gf_ref[...])
    o_ref[...] = y


def _mixer_ffn(x2, seq_len, mixer, mixer_args, g, w_up, conv_w, conv_b, w_down, g_final, final_norm):
    N, D = x2.shape
    dff = w_down.shape[0]
    tm, tf = TM_FFN, TF_FFN
    resident = lambda a: pl.BlockSpec(a.shape, lambda i: (0,) * a.ndim, pipeline_mode=pl.Buffered(1))
    tile = pl.BlockSpec((tm, D), lambda i: (i, 0))

    def left_context(rows):
        return pl.BlockSpec((rows, D), lambda i: (jnp.maximum(i * (tm // rows) - 1, 0), 0))

    if mixer == "out_proj":
        attn, w_out = mixer_args
        mix_in = [x2, x2, attn, attn, w_out]
        mix_specs = [tile, left_context(HALO_FFN), tile, left_context(HALO_ATTN_BF16), resident(w_out)]
    else:
        mix_in = [x2, x2] + list(mixer_args)
        mix_specs = [tile, left_context(HALO_POOL)] + [resident(a) for a in mixer_args]
    weights = [g, w_up, conv_w, conv_b, w_down, g_final]
    return pl.pallas_call(
        functools.partial(_ffn_body, tm=tm, tf=tf, dff=dff, tiles_per_seq=seq_len // tm,
                          final_norm=final_norm, mixer=mixer),
        grid=(N // tm,),
        in_specs=mix_specs + [resident(a) for a in weights],
        out_specs=tile,
        out_shape=jax.ShapeDtypeStruct((N, D), F32),
        scratch_shapes=[pltpu.VMEM((HALO_FFN + tm, D), BF16),
                        pltpu.VMEM((tm, D), F32),
                        pltpu.VMEM((HALO_FFN + tm, D), F32),
                        ] + [pltpu.VMEM((HALO_FFN + tm, tf), F32)] * 4,
        compiler_params=pltpu.CompilerParams(dimension_semantics=("parallel",),
                                             vmem_limit_bytes=VMEM_LIMIT),
        name="mixer_ffn_" + mixer,
    )(*mix_in, *weights)


def _pad_cols(w, width):
    return jnp.pad(w, ((0, 0), (0, width - w.shape[1])))


def _inproj_weight(w_in):
    D = w_in.shape[0]
    kv = N_GROUPS * HEAD_DIM
    sizes = [N_HEADS * HEAD_DIM] + [kv] * 6 + [N_BRANCH * N_HEADS]
    offs = np.concatenate([[0], np.cumsum(sizes)])
    q, k_c, v_c, k_s, v_s, k_w, v_w, gt = [w_in[:, offs[n]:offs[n + 1]] for n in range(8)]
    per_group = lambda w, n: [_pad_cols(w[:, g * n:(g + 1) * n], LANES) for g in range(N_GROUPS)]
    cols = ([q] + per_group(k_s, HEAD_DIM) + per_group(k_w, HEAD_DIM) + [k_c, v_c, v_s, v_w]
            + per_group(gt, N_BRANCH * HEADS_PER_GROUP))
    return jnp.concatenate(cols, axis=1).astype(BF16)


def _rope_tables(pos):
    inv = 1.0 / (ROPE_THETA ** (jnp.arange(0, HEAD_DIM, 2, dtype=F32) / HEAD_DIM))
    ang = pos.astype(F32)[:, None] * inv[None, :]
    return jnp.cos(ang), jnp.sin(ang)


def _rot_half_cols(w):
    half = HEAD_DIM // 2
    return jnp.concatenate([-w[..., half:], w[..., :half]], axis=-1)


def _overlap_matrix(n_cmp_pad, n_blk):
    j = np.arange(n_cmp_pad)[:, None]
    s = np.arange(LANES)[None, :]
    lo = np.maximum(j * CMP_STRIDE, s * SEL_BLOCK)
    hi = np.minimum(j * CMP_STRIDE + CMP_BLOCK, (s + 1) * SEL_BLOCK)
    return jnp.asarray(np.clip(hi - lo, 0, None) / CMP_BLOCK, dtype=BF16)


def kernel(x, norm_mix_0, nsa_w_in, cmp_k_pos, cmp_k_w1, cmp_k_b1, cmp_k_w2, cmp_k_b2, cmp_v_pos, cmp_v_w1, cmp_v_b1, cmp_v_w2, cmp_v_b2, nsa_w_out, norm_ffn_0, ffn_up_0, ffn_conv_w_0, ffn_conv_b_0, ffn_down_0, norm_mix_1, pool_w, pool_b, pool_scale, norm_ffn_1, ffn_up_1, ffn_conv_w_1, ffn_conv_b_1, ffn_down_1, norm_final):
    B, T, D = x.shape
    assert D == N_HEADS * HEAD_DIM and SEL_BLOCK == 64
    assert T % TM_FFN == 0 and T % TK_SEL == 0 and TK_SEL % TQ == 0 and T >= WINDOW + TQ
    n_cmp_pad = T // CMP_STRIDE
    n_blk = T // SEL_BLOCK
    assert n_blk <= LANES
    row = lambda v: v.reshape(1, -1)

    cos, sin = _rope_tables(jnp.arange(T))
    cos4 = jnp.tile(cos, (1, 4))
    sin4 = jnp.tile(jnp.concatenate([-sin, sin], axis=1), (1, 2))
    q, qsw, ks, kw, kc, vc, vs, vw, gates = _in_proj(
        x, row(norm_mix_0), _inproj_weight(nsa_w_in), cos4, sin4)

    ccos, csin = _rope_tables(jnp.arange(n_cmp_pad) * CMP_STRIDE + (CMP_BLOCK - 1))
    zeros64 = jnp.zeros((n_cmp_pad, HEAD_DIM), F32)
    ccos2 = jnp.concatenate([ccos, ccos, zeros64], axis=1)
    csin2 = jnp.concatenate([csin, csin, zeros64], axis=1)
    pos_rows = lambda p: jnp.broadcast_to(p.reshape(1, -1), (SUBLANES, p.size)).astype(BF16)
    zero_w2 = jnp.zeros_like(cmp_v_w2)
    kparams = (cmp_k_w1.astype(BF16), pos_rows(cmp_k_pos), row(cmp_k_b1),
               jnp.concatenate([cmp_k_w2, _rot_half_cols(cmp_k_w2)], axis=1).astype(BF16),
               row(jnp.concatenate([cmp_k_b2, _rot_half_cols(cmp_k_b2)])))
    vparams = (cmp_v_w1.astype(BF16), pos_rows(cmp_v_pos), row(cmp_v_b1),
               jnp.concatenate([cmp_v_w2, zero_w2], axis=1).astype(BF16),
               jnp.concatenate([zero_w2, cmp_v_w2], axis=1).astype(BF16),
               row(jnp.concatenate([cmp_v_b2, cmp_v_b2])))
    kco, vco = _compress(kc, vc, kparams, vparams, ccos2, csin2)

    o = _nsa_attn(q, qsw, ks, kw, vs, vw, kco, vco, gates, _overlap_matrix(n_cmp_pad, n_blk))

    x2 = _mixer_ffn(x.reshape(B * T, D), T, "out_proj",
                    (o.reshape(B * T, D), nsa_w_out.astype(BF16)),
                    row(norm_ffn_0), ffn_up_0.astype(BF16), ffn_conv_w_0, row(ffn_conv_b_0),
                    ffn_down_0.astype(BF16), row(norm_final), False)
    x2 = _mixer_ffn(x2, T, "pool",
                    (row(norm_mix_1), pool_w.astype(BF16), row(pool_b.reshape(-1)), row(pool_scale)),
                    row(norm_ffn_1), ffn_up_1.astype(BF16), ffn_conv_w_1, row(ffn_conv_b_1),
                    ffn_down_1.astype(BF16), row(norm_final), True)
    return x2.reshape(B, T, D)
```

```python
import functools

import numpy as np
import jax
import jax.numpy as jnp
from jax import lax
from jax.experimental import pallas as pl
from jax.experimental.pallas import tpu as pltpu

F32 = jnp.float32
BF16 = jnp.bfloat16

N_HEADS = 16
HEAD_DIM = 64
N_GROUPS = 2
HEADS_PER_GROUP = N_HEADS // N_GROUPS
N_BRANCH = 3
CMP_STRIDE = 16
CMP_BLOCK = 32
SEL_BLOCK = 64
SEL_TOPK = 16
WINDOW = 512
ROPE_THETA = 10000.0
POOL_WINDOWS = (2, 4, 8, 16)
CONV_WIDTH = 3
NORM_EPS = 1e-6
NEG_INF = -1e30
FORCE_BONUS = 1e4
N_FORCED = 3
LOG2_E = 1.4426950408889634

LANES = 128
SUBLANES = 8
VMEM_LIMIT = 56 * 1024 * 1024

TM_PROJ = 512
TQ = 256
TK_SEL = 512
CMP_VARIANTS = 2
SEL_ROW_CHUNK = 512
TM_FFN = 512
TF_FFN = 256
FFN_ROW_BLOCK = 256
HALO_FFN = 8
HALO_ATTN_BF16 = 16
HALO_POOL = 32


def _rmsnorm(x, g):
    return x * lax.rsqrt(jnp.mean(x * x, axis=-1, keepdims=True) + NORM_EPS) * g


def _nt_dot(a, b):
    return lax.dot_general(a, b, (((1,), (1,)), ((), ())), preferred_element_type=F32)


def _dot(a, b):
    return jnp.dot(a, b, preferred_element_type=F32)


def _inproj_body(x_ref, g_ref, w_ref, cos_ref, sin_ref,
                 q_ref, qsw_ref, ks_ref, kw_ref, kc_ref, vc_ref, vs_ref, vw_ref, gate_ref, *, tm):
    i = pl.program_id(1)
    h = _rmsnorm(x_ref[...], g_ref[...]).astype(BF16)
    res = _dot(h, w_ref[...])
    cos = cos_ref[...]
    sin = sin_ref[...]
    lane = lax.broadcasted_iota(jnp.int32, (tm, LANES), 1)
    first_half = (lane & (HEAD_DIM - 1)) < (HEAD_DIM // 2)

    def rope(xc):
        sw = jnp.where(first_half, pltpu.roll(xc, LANES - HEAD_DIM // 2, 1),
                       pltpu.roll(xc, HEAD_DIM // 2, 1))
        return xc * cos + sw * sin

    scale = HEAD_DIM ** -0.5 * LOG2_E
    n_pairs = N_HEADS // 2
    for p in range(n_pairs):
        qc = rope(res[:, p * LANES:(p + 1) * LANES]) * scale
        q_ref[:, p * LANES:(p + 1) * LANES] = qc.astype(BF16)
        qsw_ref[:, p * LANES:(p + 1) * LANES] = pltpu.roll(qc, HEAD_DIM, 1).astype(BF16)
    off = n_pairs * LANES
    row_t = i * tm + lax.broadcasted_iota(jnp.int32, (tm, LANES), 0)
    onehot = jnp.where(lane == (row_t >> 6), 1.0, 0.0).astype(BF16)
    for g in range(N_GROUPS):
        ks_ref[g, :, 0:LANES] = onehot
        ks_ref[g, :, LANES:2 * LANES] = rope(res[:, off + g * LANES: off + (g + 1) * LANES]).astype(BF16)
    off += N_GROUPS * LANES
    for g in range(N_GROUPS):
        kw_ref[g] = rope(res[:, off + g * LANES: off + (g + 1) * LANES]).astype(BF16)
    off += N_GROUPS * LANES
    for g in range(N_GROUPS):
        kc_ref[g] = res[:, off + g * HEAD_DIM:off + (g + 1) * HEAD_DIM]
        vc_ref[g] = res[:, off + LANES + g * HEAD_DIM:off + LANES + (g + 1) * HEAD_DIM]
    ones = jnp.ones((tm, LANES), BF16)
    vs_ref[:, 0:LANES] = res[:, off + 2 * LANES:off + 3 * LANES].astype(BF16)
    vs_ref[:, LANES:2 * LANES] = ones
    vw_ref[:, 0:LANES] = res[:, off + 3 * LANES:off + 4 * LANES].astype(BF16)
    vw_ref[:, LANES:2 * LANES] = ones
    off += 4 * LANES
    for g in range(N_GROUPS):
        z = res[:, off + g * LANES: off + (g + 1) * LANES]
        gate_ref[g] = 1.0 / (1.0 + jnp.exp(-z))


def _in_proj(x, g, wp, cos4, sin4):
    B, T, D = x.shape
    tm = TM_PROJ
    ncols = wp.shape[1]
    grid = (B, T // tm)
    tok = lambda last: pl.BlockSpec((None, tm, last), lambda b, i: (b, i, 0))
    grp = lambda last: pl.BlockSpec((None, N_GROUPS, tm, last), lambda b, i: (b, 0, i, 0))
    out_shape = (
        jax.ShapeDtypeStruct((B, T, N_HEADS * HEAD_DIM), BF16),
        jax.ShapeDtypeStruct((B, T, N_HEADS * HEAD_DIM), BF16),
        jax.ShapeDtypeStruct((B, N_GROUPS, T, 2 * LANES), BF16),
        jax.ShapeDtypeStruct((B, N_GROUPS, T, LANES), BF16),
        jax.ShapeDtypeStruct((B, N_GROUPS, T, HEAD_DIM), F32),
        jax.ShapeDtypeStruct((B, N_GROUPS, T, HEAD_DIM), F32),
        jax.ShapeDtypeStruct((B, T, 2 * LANES), BF16),
        jax.ShapeDtypeStruct((B, T, 2 * LANES), BF16),
        jax.ShapeDtypeStruct((B, N_GROUPS, T, LANES), F32),
    )
    return pl.pallas_call(
        functools.partial(_inproj_body, tm=tm),
        grid=grid,
        in_specs=[tok(D),
                  pl.BlockSpec((1, D), lambda b, i: (0, 0)),
                  pl.BlockSpec((D, ncols), lambda b, i: (0, 0)),
                  pl.BlockSpec((tm, LANES), lambda b, i: (i, 0)),
                  pl.BlockSpec((tm, LANES), lambda b, i: (i, 0))],
        out_specs=(tok(N_HEADS * HEAD_DIM), tok(N_HEADS * HEAD_DIM), grp(2 * LANES), grp(LANES),
                   grp(HEAD_DIM), grp(HEAD_DIM), tok(2 * LANES), tok(2 * LANES), grp(LANES)),
        out_shape=out_shape,
        compiler_params=pltpu.CompilerParams(dimension_semantics=("parallel", "parallel"),
                                             vmem_limit_bytes=VMEM_LIMIT),
        name="in_proj",
    )(x, g, wp, cos4, sin4)


def _gelu_tanh(x):
    return 0.5 * x * (1.0 + jnp.tanh(np.sqrt(2.0 / np.pi).astype(np.float32) * (x + 0.044715 * (x * x * x))))


def _compress_body(kc_ref, vc_ref,
                   kw1_ref, kpos_ref, kb1_ref, kw2_ref, kb2_ref,
                   vw1_ref, vpos_ref, vb1_ref, vw2a_ref, vw2b_ref, vb2_ref,
                   cos_ref, sin_ref, kco_ref, vco_ref, *, n_rows, half):
    row = lax.broadcasted_iota(jnp.int32, (n_rows, LANES), 0)
    keep = row < (n_rows - 1)

    def hidden_act(c_ref, w1_ref, pos_ref, b1_ref):
        posb = _dot(pos_ref[...], w1_ref[...])[0:1, :] + b1_ref[...]
        outs = []
        for g in range(N_GROUPS):
            a = jnp.zeros((n_rows, w1_ref.shape[1]), F32)
            b = jnp.zeros((n_rows, w1_ref.shape[1]), F32)
            for t in range(CMP_STRIDE):
                c = c_ref[g, pl.ds(t, n_rows, stride=CMP_STRIDE), :].astype(BF16)
                a = a + _dot(c, w1_ref[t * HEAD_DIM:(t + 1) * HEAD_DIM, :])
                b = b + _dot(c, w1_ref[half + t * HEAD_DIM:half + (t + 1) * HEAD_DIM, :])
            pre = a + pltpu.roll(b, n_rows - 1, 0) + posb
            outs.append(_gelu_tanh(pre).astype(BF16))
        return outs

    hk = hidden_act(kc_ref, kw1_ref, kpos_ref, kb1_ref)
    cos = cos_ref[...]
    sin = sin_ref[...]
    for g in range(N_GROUPS):
        r = _dot(hk[g], kw2_ref[...]) + kb2_ref[...]
        kco_ref[g] = jnp.where(keep, r * cos + pltpu.roll(r, HEAD_DIM, 1) * sin, 0.0).astype(BF16)
    hv = hidden_act(vc_ref, vw1_ref, vpos_ref, vb1_ref)
    v = _dot(hv[0], vw2a_ref[...]) + _dot(hv[1], vw2b_ref[...]) + vb2_ref[...]
    vco_ref[...] = jnp.where(keep, v, 0.0).astype(BF16)


def _compress(kc2, vc2, kparams, vparams, ccos, csin):
    B, _, T, dh = kc2.shape
    n_rows = T // CMP_STRIDE
    full = lambda a: pl.BlockSpec(a.shape, lambda b: (0,) * a.ndim)
    blk = pl.BlockSpec((None, N_GROUPS, T, dh), lambda b: (b, 0, 0, 0), pipeline_mode=pl.Buffered(1))
    weights = list(kparams) + list(vparams) + [ccos, csin]
    return pl.pallas_call(
        functools.partial(_compress_body, n_rows=n_rows, half=CMP_STRIDE * dh),
        grid=(B,),
        in_specs=[blk, blk] + [full(a) for a in weights],
        out_specs=(pl.BlockSpec((None, N_GROUPS, n_rows, LANES), lambda b: (b, 0, 0, 0)),
                   pl.BlockSpec((None, n_rows, LANES), lambda b: (b, 0, 0))),
        out_shape=(jax.ShapeDtypeStruct((B, N_GROUPS, n_rows, LANES), BF16),
                   jax.ShapeDtypeStruct((B, n_rows, LANES), BF16)),
        compiler_params=pltpu.CompilerParams(dimension_semantics=("parallel",),
                                             vmem_limit_bytes=VMEM_LIMIT),
        name="compress",
    )(kc2, vc2, *weights)


def _softmax_rows(s):
    m = jnp.max(s, axis=1, keepdims=True)
    e = jnp.exp2(s - m)
    return e, jnp.sum(e, axis=1, keepdims=True)


def _attn_body(q_ref, qsw_ref, ks_ref, kw_ref, vs_ref, vw_ref, kc_ref, vc_ref, gate_ref, ov_ref,
               o_ref, qaug_ref, s_ref, s2_ref, m_ref, l_ref, acc_ref, oacc_ref,
               *, tq, tk, n_cmp_pad, n_blk):
    hg = HEADS_PER_GROUP
    g = pl.program_id(1)
    i = pl.program_id(2)
    s0 = i * tq
    t_col = s0 + lax.broadcasted_iota(jnp.int32, (tq, 1), 0)

    rc = SEL_ROW_CHUNK
    n_chunks = hg * tq // rc
    reps = rc // tq

    def qpair(h):
        src = q_ref if h % 2 == 0 else qsw_ref
        return src[:, (h // 2) * LANES:(h // 2 + 1) * LANES]

    gates = gate_ref[...]

    def gate(h, branch):
        c = N_BRANCH * h + branch
        return gates[:, c:c + 1]

    n_lc = tk // LANES

    def scores(kt, rows):
        start = pl.multiple_of(kt * tk, tk)
        return _nt_dot(qaug_ref[rows, :], ks_ref[pl.ds(start, tk), :])

    s_refs = (s_ref, s2_ref)

    def softmax_pv(kt, causal, slot):
        start = pl.multiple_of(kt * tk, tk)
        if causal:
            kpos = start + lax.broadcasted_iota(jnp.int32, (1, tk), 1)
            bias_d = jnp.where(kpos <= t_col, 0.0, NEG_INF)
        for p in range(n_chunks):
            rows = slice(p * rc, (p + 1) * rc)
            if not causal:
                s_refs[1 - slot][rows, :] = scores(kt + 1, rows)
            s = s_refs[slot][rows, :]
            if causal:
                s = jnp.concatenate([s[r * tq:(r + 1) * tq] + bias_d for r in range(reps)], axis=0)
            chunks = [s[:, c * LANES:(c + 1) * LANES] for c in range(n_lc)]
            mx = functools.reduce(jnp.maximum, chunks)
            m_old = m_ref[rows, :]
            m_new = jnp.maximum(m_old, jnp.max(mx, axis=1, keepdims=True))
            alpha = jnp.exp2(m_old - m_new)
            p_bf = jnp.concatenate([jnp.exp2(c - m_new) for c in chunks], axis=1).astype(BF16)
            pv = _dot(p_bf, vs_ref[pl.ds(start, tk), :])
            acc_ref[rows, :] = alpha * acc_ref[rows, :] + pv[:, :LANES]
            l_ref[rows, :] = alpha * l_ref[rows, :] + pv[:, LANES:]
            m_ref[rows, :] = m_new

    def pre_loop(nk):
        j_row = lax.broadcasted_iota(jnp.int32, (1, nk), 1)
        bias_c = jnp.where(j_row * CMP_STRIDE + (CMP_BLOCK - 1) <= t_col, 0.0, NEG_INF)
        vis = jnp.where(t_col >= CMP_BLOCK - 1, 1.0, 0.0)
        q_all = jnp.concatenate([qpair(h) for h in range(hg)], axis=0)
        sc = _nt_dot(q_all, kc_ref[0:nk, :]).reshape(hg, tq, nk) + bias_c[None]
        e = jnp.exp2(sc - jnp.max(sc, axis=2, keepdims=True))
        r_c = vis[None] / jnp.sum(e, axis=2, keepdims=True)
        rhs_c = jnp.concatenate([vc_ref[0:nk, :], ov_ref[0:nk, :]], axis=1)
        pvc = _dot(e.reshape(hg * tq, nk).astype(BF16), rhs_c)
        imp = jnp.zeros((tq, LANES), F32)
        for h in range(hg):
            rows = slice(h * tq, (h + 1) * tq)
            oacc_ref[rows, :] = (gate(h, 0) * r_c[h]) * pvc[rows, :LANES]
            imp = imp + r_c[h] * pvc[rows, LANES:]

        n_win = WINDOW + tq
        kstart = pl.multiple_of(jnp.maximum(s0 - WINDOW, 0), tq)
        kpos_w = kstart + lax.broadcasted_iota(jnp.int32, (1, n_win), 1)
        bias_w = jnp.where((kpos_w <= t_col) & (kpos_w > t_col - WINDOW), 0.0, NEG_INF)
        sw = _nt_dot(q_all, kw_ref[pl.ds(kstart, n_win), :]).reshape(hg, tq, n_win) + bias_w[None]
        ew = jnp.exp2(sw - jnp.max(sw, axis=2, keepdims=True))
        ow = _dot(ew.reshape(hg * tq, n_win).astype(BF16), vw_ref[pl.ds(kstart, n_win), :])
        for h in range(hg):
            rows = slice(h * tq, (h + 1) * tq)
            oacc_ref[rows, :] += (gate(h, 2) / ow[rows, LANES:]) * ow[rows, :LANES]

        blk = lax.broadcasted_iota(jnp.int32, (tq, LANES), 1)
        cur = t_col >> 6
        forced = (blk == 0) | (blk == cur) | (blk == cur - 1)
        valid = blk <= cur
        score = jnp.where(valid & jnp.logical_not(forced), imp, NEG_INF)
        st = score.T
        blk_t = lax.broadcasted_iota(jnp.int32, (LANES, tq), 0).astype(F32)
        sel_t = jnp.zeros((LANES, tq), F32)
        for _ in range(SEL_TOPK - N_FORCED):
            mx = jnp.max(st, axis=0, keepdims=True)
            idx = jnp.min(jnp.where(st == mx, blk_t, float(LANES)), axis=0, keepdims=True)
            hit = blk_t == idx
            sel_t = jnp.where(hit, 1.0, sel_t)
            st = jnp.where(hit, -jnp.inf, st)
        sel = (sel_t.T > 0.5) | forced
        bias_s = jnp.where(sel & valid, 0.0, NEG_INF).astype(BF16)

        for h in range(hg):
            qaug_ref[h * tq:(h + 1) * tq, 0:LANES] = bias_s
            qaug_ref[h * tq:(h + 1) * tq, LANES:2 * LANES] = qpair(h)
        m_ref[...] = jnp.full(m_ref.shape, NEG_INF, F32)
        l_ref[...] = jnp.zeros(l_ref.shape, F32)
        acc_ref[...] = jnp.zeros(acc_ref.shape, F32)
        s_ref[...] = scores(0, slice(None))

    n_vis = (s0 + tq) // CMP_STRIDE - 1
    step = n_cmp_pad // CMP_VARIANTS
    for v in range(CMP_VARIANTS):
        lo, hi = v * step, (v + 1) * step

        @pl.when((n_vis > lo) & (n_vis <= hi) if v else n_vis <= hi)
        def _():
            pre_loop(hi)

    k_diag = s0 // tk

    def tile_pair(j, carry):
        softmax_pv(2 * j, False, 0)
        softmax_pv(2 * j + 1, False, 1)
        return carry

    lax.fori_loop(0, k_diag // 2, tile_pair, 0)

    @pl.when(k_diag % 2 == 1)
    def _():
        softmax_pv(k_diag - 1, False, 0)
        softmax_pv(k_diag, True, 1)

    @pl.when(k_diag % 2 == 0)
    def _():
        softmax_pv(k_diag, True, 0)

    outs = []
    for h in range(hg):
        rows = slice(h * tq, (h + 1) * tq)
        outs.append(oacc_ref[rows, :] + (gate(h, 1) / l_ref[rows, :]) * acc_ref[rows, :])
    lane = lax.broadcasted_iota(jnp.int32, (tq, LANES), 1)
    low = lane < HEAD_DIM
    is_g0 = g == 0
    for p in range(hg // 2):
        a = outs[2 * p]
        b = outs[2 * p + 1]
        xa = jnp.where(is_g0, a, pltpu.roll(a, HEAD_DIM, 1))
        xb = jnp.where(is_g0, pltpu.roll(b, HEAD_DIM, 1), b)
        o_ref[:, p * LANES:(p + 1) * LANES] = jnp.where(low, xa, xb).astype(BF16)


def _nsa_attn(q, qsw, ks, kw, vs, vw, kco, vco, gates, overlap):
    B, T, _ = q.shape
    tq, tk = TQ, TK_SEL
    n_cmp_pad = kco.shape[2]
    n_blk = T // SEL_BLOCK
    gw = HEADS_PER_GROUP * HEAD_DIM
    rows = HEADS_PER_GROUP * tq
    grid = (B, N_GROUPS, T // tq)
    once = pl.Buffered(1)
    return pl.pallas_call(
        functools.partial(_attn_body, tq=tq, tk=tk, n_cmp_pad=n_cmp_pad, n_blk=n_blk),
        grid=grid,
        in_specs=[
            pl.BlockSpec((None, tq, gw), lambda b, g, i: (b, i, g)),
            pl.BlockSpec((None, tq, gw), lambda b, g, i: (b, i, g)),
            pl.BlockSpec((None, None, T, 2 * LANES), lambda b, g, i: (b, g, 0, 0), pipeline_mode=once),
            pl.BlockSpec((None, None, T, LANES), lambda b, g, i: (b, g, 0, 0), pipeline_mode=once),
            pl.BlockSpec((None, T, 2 * LANES), lambda b, g, i: (b, 0, 0), pipeline_mode=once),
            pl.BlockSpec((None, T, 2 * LANES), lambda b, g, i: (b, 0, 0), pipeline_mode=once),
            pl.BlockSpec((None, None, n_cmp_pad, LANES), lambda b, g, i: (b, g, 0, 0)),
            pl.BlockSpec((None, n_cmp_pad, LANES), lambda b, g, i: (b, 0, 0)),
            pl.BlockSpec((None, None, tq, LANES), lambda b, g, i: (b, g, i, 0)),
            pl.BlockSpec(overlap.shape, lambda b, g, i: (0, 0)),
        ],
        out_specs=pl.BlockSpec((None, tq, gw), lambda b, g, i: (b, i, g)),
        out_shape=jax.ShapeDtypeStruct((B, T, N_HEADS * HEAD_DIM), BF16),
        scratch_shapes=[pltpu.VMEM((rows, 2 * LANES), BF16),
                        pltpu.VMEM((rows, tk), F32),
                        pltpu.VMEM((rows, tk), F32),
                        pltpu.VMEM((rows, LANES), F32),
                        pltpu.VMEM((rows, LANES), F32),
                        pltpu.VMEM((rows, LANES), F32),
                        pltpu.VMEM((rows, LANES), F32)],
        compiler_params=pltpu.CompilerParams(
            dimension_semantics=("parallel", "parallel", "arbitrary"),
            vmem_limit_bytes=VMEM_LIMIT),
        name="nsa_attn",
    )(q, qsw, ks, kw, vs, vw, kco, vco, gates, overlap)


def _mixer_out_proj(x_ref, xh_ref, a_ref, ah_ref, wo_ref, xin_ref, *, tm):
    halo = HALO_FFN
    lead = ah_ref.shape[0]
    a_ext = jnp.concatenate([ah_ref[...], a_ref[...]], axis=0)
    y = _dot(a_ext, wo_ref[...])
    xin_ref[0:halo, :] = xh_ref[...] + y[lead - halo:lead]
    xin_ref[halo:halo + tm, :] = x_ref[...] + y[lead:lead + tm]


def _mixer_pool(x_ref, xh_ref, gm_ref, pw_ref, pb_ref, ps_ref, xin_ref, *, tm, seq_tile):
    halo = HALO_FFN
    lead = xh_ref.shape[0]
    n_ext = lead + tm
    xe = jnp.concatenate([xh_ref[...], x_ref[...]], axis=0)
    row = lax.broadcasted_iota(jnp.int32, (n_ext, 1), 0)
    before_seq = (seq_tile == 0) & (row < lead)
    hm = jnp.where(before_seq, 0.0, _rmsnorm(xe, gm_ref[...]))
    t_seq = seq_tile * tm + row - lead
    gdim = xe.shape[1] // len(POOL_WINDOWS)
    ys = []
    for gi, w in enumerate(POOL_WINDOWS):
        cols = slice(gi * gdim, (gi + 1) * gdim)
        hg = hm[:, cols]
        s = hg
        shift = 1
        while shift < w:
            s = s + pltpu.roll(s, shift, 0)
            shift *= 2
        cnt = jnp.clip(t_seq + 1, 1, w).astype(F32)
        ys.append(_dot((s / cnt - hg).astype(BF16), pw_ref[gi]))
    y = (jnp.concatenate(ys, axis=1) + pb_ref[...]) * ps_ref[...]
    xin_ref[...] = (xe + y)[lead - halo:n_ext]


def _ffn_body(*refs, tm, tf, dff, tiles_per_seq, final_norm, mixer):
    n_mix = {"out_proj": 5, "pool": 6}[mixer]
    mix_refs, refs = refs[:n_mix], refs[n_mix:]
    (g_ref, wup_ref, cw_ref, cb_ref, wd_ref, gf_ref, o_ref,
     h_ref, xin_ref, a_ref, ug0_ref, uv0_ref, ug1_ref, uv1_ref) = refs
    u_refs = ((ug0_ref, uv0_ref), (ug1_ref, uv1_ref))
    i = pl.program_id(0)
    halo = HALO_FFN
    n = halo + tm
    seq_tile = i % tiles_per_seq
    if mixer == "out_proj":
        _mixer_out_proj(*mix_refs, xin_ref, tm=tm)
    else:
        _mixer_pool(*mix_refs, xin_ref, tm=tm, seq_tile=seq_tile)
    hn = _rmsnorm(xin_ref[...], g_ref[...])
    row = lax.broadcasted_iota(jnp.int32, (n, 1), 0)
    no_context = (seq_tile == 0) & (row < halo)
    h_ref[...] = jnp.where(no_context, 0.0, hn).astype(BF16)

    nj = dff // tf
    rb = FFN_ROW_BLOCK
    n_rb = tm // rb

    def up_proj(j, half, b):
        col = half * dff + j * tf
        r0 = 0 if b == 0 else halo + b * rb
        r1 = halo + (b + 1) * rb
        u_refs[j % 2][half][r0:r1, :] = _dot(h_ref[r0:r1, :], wup_ref[:, col:col + tf])

    def conv(j, half, b):
        col = half * dff + j * tf
        u = u_refs[j % 2][half][b * rb:halo + (b + 1) * rb, :]
        u1 = pltpu.roll(u, 1, 0)
        u2 = pltpu.roll(u, 2, 0)
        cw = cw_ref[:, col:col + tf]
        c = cw[0:1, :] * u2 + cw[1:2, :] * u1 + cw[2:3, :] * u + cb_ref[:, col:col + tf]
        return c[halo:halo + rb]

    for half in range(2):
        for b in range(n_rb):
            up_proj(0, half, b)
    for j in range(nj):
        for b in range(n_rb):
            rows = slice(b * rb, (b + 1) * rb)
            if j + 1 < nj:
                up_proj(j + 1, 0, b)
            cg = conv(j, 0, b)
            if j + 1 < nj:
                up_proj(j + 1, 1, b)
            cv = conv(j, 1, b)
            a = (cg * (1.0 / (1.0 + jnp.exp(-cg)))) * cv
            a_ref[rows, j * tf:(j + 1) * tf] = a.astype(BF16)

    y = xin_ref[halo:n, :] + _dot(a_ref[...], wd_ref[...])
    if final_norm:
        y = _rmsnorm(y, gf_ref[...])
    o_ref[...] = y


def _mixer_ffn(x2, seq_len, mixer, mixer_args, g, w_up, conv_w, conv_b, w_down, g_final, final_norm):
    N, D = x2.shape
    dff = w_down.shape[0]
    tm, tf = TM_FFN, TF_FFN
    resident = lambda a: pl.BlockSpec(a.shape, lambda i: (0,) * a.ndim, pipeline_mode=pl.Buffered(1))
    tile = pl.BlockSpec((tm, D), lambda i: (i, 0))

    def left_context(rows):
        return pl.BlockSpec((rows, D), lambda i: (jnp.maximum(i * (tm // rows) - 1, 0), 0))

    if mixer == "out_proj":
        attn, w_out = mixer_args
        mix_in = [x2, x2, attn, attn, w_out]
        mix_specs = [tile, left_context(HALO_FFN), tile, left_context(HALO_ATTN_BF16), resident(w_out)]
    else:
        mix_in = [x2, x2] + list(mixer_args)
        mix_specs = [tile, left_context(HALO_POOL)] + [resident(a) for a in mixer_args]
    weights = [g, w_up, conv_w, conv_b, w_down, g_final]
    return pl.pallas_call(
        functools.partial(_ffn_body, tm=tm, tf=tf, dff=dff, tiles_per_seq=seq_len // tm,
                          final_norm=final_norm, mixer=mixer),
        grid=(N // tm,),
        in_specs=mix_specs + [resident(a) for a in weights],
        out_specs=tile,
        out_shape=jax.ShapeDtypeStruct((N, D), F32),
        scratch_shapes=[pltpu.VMEM((HALO_FFN + tm, D), BF16),
                        pltpu.VMEM((HALO_FFN + tm, D), F32),
                        pltpu.VMEM((tm, dff), BF16),
                        ] + [pltpu.VMEM((HALO_FFN + tm, tf), F32)] * 4,
        compiler_params=pltpu.CompilerParams(dimension_semantics=("parallel",),
                                             vmem_limit_bytes=VMEM_LIMIT),
        name="mixer_ffn_" + mixer,
    )(*mix_in, *weights)


def _pad_cols(w, width):
    return jnp.pad(w, ((0, 0), (0, width - w.shape[1])))


def _inproj_weight(w_in):
    D = w_in.shape[0]
    kv = N_GROUPS * HEAD_DIM
    sizes = [N_HEADS * HEAD_DIM] + [kv] * 6 + [N_BRANCH * N_HEADS]
    offs = np.concatenate([[0], np.cumsum(sizes)])
    q, k_c, v_c, k_s, v_s, k_w, v_w, gt = [w_in[:, offs[n]:offs[n + 1]] for n in range(8)]
    per_group = lambda w, n: [_pad_cols(w[:, g * n:(g + 1) * n], LANES) for g in range(N_GROUPS)]
    cols = ([q] + per_group(k_s, HEAD_DIM) + per_group(k_w, HEAD_DIM) + [k_c, v_c, v_s, v_w]
            + per_group(gt, N_BRANCH * HEADS_PER_GROUP))
    return jnp.concatenate(cols, axis=1).astype(BF16)


def _rope_tables(pos):
    inv = 1.0 / (ROPE_THETA ** (jnp.arange(0, HEAD_DIM, 2, dtype=F32) / HEAD_DIM))
    ang = pos.astype(F32)[:, None] * inv[None, :]
    return jnp.cos(ang), jnp.sin(ang)


def _rot_half_cols(w):
    half = HEAD_DIM // 2
    return jnp.concatenate([-w[..., half:], w[..., :half]], axis=-1)


def _overlap_matrix(n_cmp_pad, n_blk):
    j = np.arange(n_cmp_pad)[:, None]
    s = np.arange(LANES)[None, :]
    lo = np.maximum(j * CMP_STRIDE, s * SEL_BLOCK)
    hi = np.minimum(j * CMP_STRIDE + CMP_BLOCK, (s + 1) * SEL_BLOCK)
    return jnp.asarray(np.clip(hi - lo, 0, None) / CMP_BLOCK, dtype=BF16)


def kernel(x, norm_mix_0, nsa_w_in, cmp_k_pos, cmp_k_w1, cmp_k_b1, cmp_k_w2, cmp_k_b2, cmp_v_pos, cmp_v_w1, cmp_v_b1, cmp_v_w2, cmp_v_b2, nsa_w_out, norm_ffn_0, ffn_up_0, ffn_conv_w_0, ffn_conv_b_0, ffn_down_0, norm_mix_1, pool_w, pool_b, pool_scale, norm_ffn_1, ffn_up_1, ffn_conv_w_1, ffn_conv_b_1, ffn_down_1, norm_final):
    B, T, D = x.shape
    assert D == N_HEADS * HEAD_DIM and SEL_BLOCK == 64
    assert T % TM_FFN == 0 and T % TK_SEL == 0 and TK_SEL % TQ == 0 and T >= WINDOW + TQ
    n_cmp_pad = T // CMP_STRIDE
    n_blk = T // SEL_BLOCK
    assert n_blk <= LANES
    row = lambda v: v.reshape(1, -1)

    cos, sin = _rope_tables(jnp.arange(T))
    cos4 = jnp.tile(cos, (1, 4))
    sin4 = jnp.tile(jnp.concatenate([-sin, sin], axis=1), (1, 2))
    q, qsw, ks, kw, kc, vc, vs, vw, gates = _in_proj(
        x, row(norm_mix_0), _inproj_weight(nsa_w_in), cos4, sin4)

    ccos, csin = _rope_tables(jnp.arange(n_cmp_pad) * CMP_STRIDE + (CMP_BLOCK - 1))
    zeros64 = jnp.zeros((n_cmp_pad, HEAD_DIM), F32)
    ccos2 = jnp.concatenate([ccos, ccos, zeros64], axis=1)
    csin2 = jnp.concatenate([csin, csin, zeros64], axis=1)
    pos_rows = lambda p: jnp.broadcast_to(p.reshape(1, -1), (SUBLANES, p.size)).astype(BF16)
    zero_w2 = jnp.zeros_like(cmp_v_w2)
    kparams = (cmp_k_w1.astype(BF16), pos_rows(cmp_k_pos), row(cmp_k_b1),
               jnp.concatenate([cmp_k_w2, _rot_half_cols(cmp_k_w2)], axis=1).astype(BF16),
               row(jnp.concatenate([cmp_k_b2, _rot_half_cols(cmp_k_b2)])))
    vparams = (cmp_v_w1.astype(BF16), pos_rows(cmp_v_pos), row(cmp_v_b1),
               jnp.concatenate([cmp_v_w2, zero_w2], axis=1).astype(BF16),
               jnp.concatenate([zero_w2, cmp_v_w2], axis=1).astype(BF16),
               row(jnp.concatenate([cmp_v_b2, cmp_v_b2])))
    kco, vco = _compress(kc, vc, kparams, vparams, ccos2, csin2)

    o = _nsa_attn(q, qsw, ks, kw, vs, vw, kco, vco, gates, _overlap_matrix(n_cmp_pad, n_blk))

    x2 = _mixer_ffn(x.reshape(B * T, D), T, "out_proj",
                    (o.reshape(B * T, D), nsa_w_out.astype(BF16)),
                    row(norm_ffn_0), ffn_up_0.astype(BF16), ffn_conv_w_0, row(ffn_conv_b_0),
                    ffn_down_0.astype(BF16), row(norm_final), False)
    x2 = _mixer_ffn(x2, T, "pool",
                    (row(norm_mix_1), pool_w.astype(BF16), row(pool_b.reshape(-1)), row(pool_scale)),
                    row(norm_ffn_1), ffn_up_1.astype(BF16), ffn_conv_w_1, row(ffn_conv_b_1),
                    ffn_down_1.astype(BF16), row(norm_final), True)
    return x2.reshape(B, T, D)
```

```python
import functools

import numpy as np
import jax
import jax.numpy as jnp
from jax import lax
from jax.experimental import pallas as pl
from jax.experimental.pallas import tpu as pltpu

F32 = jnp.float32
BF16 = jnp.bfloat16

N_HEADS = 16
HEAD_DIM = 64
N_GROUPS = 2
HEADS_PER_GROUP = N_HEADS // N_GROUPS
N_BRANCH = 3
CMP_STRIDE = 16
CMP_BLOCK = 32
SEL_BLOCK = 64
SEL_TOPK = 16
WINDOW = 512
ROPE_THETA = 10000.0
POOL_WINDOWS = (2, 4, 8, 16)
CONV_WIDTH = 3
NORM_EPS = 1e-6
NEG_INF = -1e30
FORCE_BONUS = 1e4
N_FORCED = 3
LOG2_E = 1.4426950408889634

LANES = 128
SUBLANES = 8
VMEM_LIMIT = 56 * 1024 * 1024

TM_PROJ = 512
TQ = 256
TK_SEL = 512
CMP_VARIANTS = 2
SEL_ROW_CHUNK = 512
TM_FFN = 512
TF_FFN = 256
FFN_ROW_BLOCK = 256
HALO_FFN = 8
HALO_ATTN_BF16 = 16
HALO_POOL = 32


def _rmsnorm(x, g):
    return x * lax.rsqrt(jnp.mean(x * x, axis=-1, keepdims=True) + NORM_EPS) * g


def _nt_dot(a, b):
    return lax.dot_general(a, b, (((1,), (1,)), ((), ())), preferred_element_type=F32)


def _dot(a, b):
    return jnp.dot(a, b, preferred_element_type=F32)


def _inproj_body(x_ref, g_ref, w_ref, cos_ref, sin_ref,
                 q_ref, qsw_ref, ks_ref, kw_ref, kc_ref, vc_ref, vs_ref, vw_ref, gate_ref, *, tm):
    i = pl.program_id(1)
    h = _rmsnorm(x_ref[...], g_ref[...]).astype(BF16)
    res = _dot(h, w_ref[...])
    cos = cos_ref[...]
    sin = sin_ref[...]
    lane = lax.broadcasted_iota(jnp.int32, (tm, LANES), 1)
    first_half = (lane & (HEAD_DIM - 1)) < (HEAD_DIM // 2)

    def rope(xc):
        sw = jnp.where(first_half, pltpu.roll(xc, LANES - HEAD_DIM // 2, 1),
                       pltpu.roll(xc, HEAD_DIM // 2, 1))
        return xc * cos + sw * sin

    scale = HEAD_DIM ** -0.5 * LOG2_E
    n_pairs = N_HEADS // 2
    for p in range(n_pairs):
        qc = rope(res[:, p * LANES:(p + 1) * LANES]) * scale
        q_ref[:, p * LANES:(p + 1) * LANES] = qc.astype(BF16)
        qsw_ref[:, p * LANES:(p + 1) * LANES] = pltpu.roll(qc, HEAD_DIM, 1).astype(BF16)
    off = n_pairs * LANES
    row_t = i * tm + lax.broadcasted_iota(jnp.int32, (tm, LANES), 0)
    onehot = jnp.where(lane == (row_t >> 6), 1.0, 0.0).astype(BF16)
    for g in range(N_GROUPS):
        ks_ref[g, :, 0:LANES] = onehot
        ks_ref[g, :, LANES:2 * LANES] = rope(res[:, off + g * LANES: off + (g + 1) * LANES]).astype(BF16)
    off += N_GROUPS * LANES
    for g in range(N_GROUPS):
        kw_ref[g] = rope(res[:, off + g * LANES: off + (g + 1) * LANES]).astype(BF16)
    off += N_GROUPS * LANES
    for g in range(N_GROUPS):
        kc_ref[g] = res[:, off + g * HEAD_DIM:off + (g + 1) * HEAD_DIM]
        vc_ref[g] = res[:, off + LANES + g * HEAD_DIM:off + LANES + (g + 1) * HEAD_DIM]
    ones = jnp.ones((tm, LANES), BF16)
    vs_ref[:, 0:LANES] = res[:, off + 2 * LANES:off + 3 * LANES].astype(BF16)
    vs_ref[:, LANES:2 * LANES] = ones
    vw_ref[:, 0:LANES] = res[:, off + 3 * LANES:off + 4 * LANES].astype(BF16)
    vw_ref[:, LANES:2 * LANES] = ones
    off += 4 * LANES
    for g in range(N_GROUPS):
        z = res[:, off + g * LANES: off + (g + 1) * LANES]
        gate_ref[g] = 1.0 / (1.0 + jnp.exp(-z))


def _in_proj(x, g, wp, cos4, sin4):
    B, T, D = x.shape
    tm = TM_PROJ
    ncols = wp.shape[1]
    grid = (B, T // tm)
    tok = lambda last: pl.BlockSpec((None, tm, last), lambda b, i: (b, i, 0))
    grp = lambda last: pl.BlockSpec((None, N_GROUPS, tm, last), lambda b, i: (b, 0, i, 0))
    out_shape = (
        jax.ShapeDtypeStruct((B, T, N_HEADS * HEAD_DIM), BF16),
        jax.ShapeDtypeStruct((B, T, N_HEADS * HEAD_DIM), BF16),
        jax.ShapeDtypeStruct((B, N_GROUPS, T, 2 * LANES), BF16),
        jax.ShapeDtypeStruct((B, N_GROUPS, T, LANES), BF16),
        jax.ShapeDtypeStruct((B, N_GROUPS, T, HEAD_DIM), F32),
        jax.ShapeDtypeStruct((B, N_GROUPS, T, HEAD_DIM), F32),
        jax.ShapeDtypeStruct((B, T, 2 * LANES), BF16),
        jax.ShapeDtypeStruct((B, T, 2 * LANES), BF16),
        jax.ShapeDtypeStruct((B, N_GROUPS, T, LANES), F32),
    )
    return pl.pallas_call(
        functools.partial(_inproj_body, tm=tm),
        grid=grid,
        in_specs=[tok(D),
                  pl.BlockSpec((1, D), lambda b, i: (0, 0)),
                  pl.BlockSpec((D, ncols), lambda b, i: (0, 0)),
                  pl.BlockSpec((tm, LANES), lambda b, i: (i, 0)),
                  pl.BlockSpec((tm, LANES), lambda b, i: (i, 0))],
        out_specs=(tok(N_HEADS * HEAD_DIM), tok(N_HEADS * HEAD_DIM), grp(2 * LANES), grp(LANES),
                   grp(HEAD_DIM), grp(HEAD_DIM), tok(2 * LANES), tok(2 * LANES), grp(LANES)),
        out_shape=out_shape,
        compiler_params=pltpu.CompilerParams(dimension_semantics=("parallel", "parallel"),
                                             vmem_limit_bytes=VMEM_LIMIT),
        name="in_proj",
    )(x, g, wp, cos4, sin4)


def _gelu_tanh(x):
    return 0.5 * x * (1.0 + jnp.tanh(np.sqrt(2.0 / np.pi).astype(np.float32) * (x + 0.044715 * (x * x * x))))


def _compress_body(kc_ref, vc_ref,
                   kw1_ref, kpos_ref, kb1_ref, kw2_ref, kb2_ref,
                   vw1_ref, vpos_ref, vb1_ref, vw2a_ref, vw2b_ref, vb2_ref,
                   cos_ref, sin_ref, kco_ref, vco_ref, *, n_rows, half):
    row = lax.broadcasted_iota(jnp.int32, (n_rows, LANES), 0)
    keep = row < (n_rows - 1)

    def hidden_act(c_ref, w1_ref, pos_ref, b1_ref):
        posb = _dot(pos_ref[...], w1_ref[...])[0:1, :] + b1_ref[...]
        outs = []
        for g in range(N_GROUPS):
            a = jnp.zeros((n_rows, w1_ref.shape[1]), F32)
            b = jnp.zeros((n_rows, w1_ref.shape[1]), F32)
            for t in range(CMP_STRIDE):
                c = c_ref[g, pl.ds(t, n_rows, stride=CMP_STRIDE), :].astype(BF16)
                a = a + _dot(c, w1_ref[t * HEAD_DIM:(t + 1) * HEAD_DIM, :])
                b = b + _dot(c, w1_ref[half + t * HEAD_DIM:half + (t + 1) * HEAD_DIM, :])
            pre = a + pltpu.roll(b, n_rows - 1, 0) + posb
            outs.append(_gelu_tanh(pre).astype(BF16))
        return outs

    hk = hidden_act(kc_ref, kw1_ref, kpos_ref, kb1_ref)
    cos = cos_ref[...]
    sin = sin_ref[...]
    for g in range(N_GROUPS):
        r = _dot(hk[g], kw2_ref[...]) + kb2_ref[...]
        kco_ref[g] = jnp.where(keep, r * cos + pltpu.roll(r, HEAD_DIM, 1) * sin, 0.0).astype(BF16)
    hv = hidden_act(vc_ref, vw1_ref, vpos_ref, vb1_ref)
    v = _dot(hv[0], vw2a_ref[...]) + _dot(hv[1], vw2b_ref[...]) + vb2_ref[...]
    vco_ref[...] = jnp.where(keep, v, 0.0).astype(BF16)


def _compress(kc2, vc2, kparams, vparams, ccos, csin):
    B, _, T, dh = kc2.shape
    n_rows = T // CMP_STRIDE
    full = lambda a: pl.BlockSpec(a.shape, lambda b: (0,) * a.ndim)
    blk = pl.BlockSpec((None, N_GROUPS, T, dh), lambda b: (b, 0, 0, 0), pipeline_mode=pl.Buffered(1))
    weights = list(kparams) + list(vparams) + [ccos, csin]
    return pl.pallas_call(
        functools.partial(_compress_body, n_rows=n_rows, half=CMP_STRIDE * dh),
        grid=(B,),
        in_specs=[blk, blk] + [full(a) for a in weights],
        out_specs=(pl.BlockSpec((None, N_GROUPS, n_rows, LANES), lambda b: (b, 0, 0, 0)),
                   pl.BlockSpec((None, n_rows, LANES), lambda b: (b, 0, 0))),
        out_shape=(jax.ShapeDtypeStruct((B, N_GROUPS, n_rows, LANES), BF16),
                   jax.ShapeDtypeStruct((B, n_rows, LANES), BF16)),
        compiler_params=pltpu.CompilerParams(dimension_semantics=("parallel",),
                                             vmem_limit_bytes=VMEM_LIMIT),
        name="compress",
    )(kc2, vc2, *weights)


def _softmax_rows(s):
    m = jnp.max(s, axis=1, keepdims=True)
    e = jnp.exp2(s - m)
    return e, jnp.sum(e, axis=1, keepdims=True)


def _attn_body(q_ref, qsw_ref, ks_ref, kw_ref, vs_ref, vw_ref, kc_ref, vc_ref, gate_ref, ov_ref,
               o_ref, qaug_ref, s_ref, m_ref, l_ref, acc_ref, oacc_ref, *, tq, tk, n_cmp_pad, n_blk):
    hg = HEADS_PER_GROUP
    g = pl.program_id(1)
    i = pl.program_id(2)
    s0 = i * tq
    t_col = s0 + lax.broadcasted_iota(jnp.int32, (tq, 1), 0)

    rc = SEL_ROW_CHUNK
    n_chunks = hg * tq // rc
    reps = rc // tq

    def qpair(h):
        src = q_ref if h % 2 == 0 else qsw_ref
        return src[:, (h // 2) * LANES:(h // 2 + 1) * LANES]

    gates = gate_ref[...]

    def gate(h, branch):
        c = N_BRANCH * h + branch
        return gates[:, c:c + 1]

    n_lc = tk // LANES

    def scores(kt, rows):
        start = pl.multiple_of(kt * tk, tk)
        return _nt_dot(qaug_ref[rows, :], ks_ref[pl.ds(start, tk), :])

    def softmax_pv(kt, causal):
        start = pl.multiple_of(kt * tk, tk)
        if causal:
            kpos = start + lax.broadcasted_iota(jnp.int32, (1, tk), 1)
            bias_d = jnp.where(kpos <= t_col, 0.0, NEG_INF)
        for p in range(n_chunks):
            rows = slice(p * rc, (p + 1) * rc)
            s = s_ref[rows, :]
            if causal:
                s = jnp.concatenate([s[r * tq:(r + 1) * tq] + bias_d for r in range(reps)], axis=0)
            chunks = [s[:, c * LANES:(c + 1) * LANES] for c in range(n_lc)]
            mx = functools.reduce(jnp.maximum, chunks)
            m_old = m_ref[rows, :]
            m_new = jnp.maximum(m_old, jnp.max(mx, axis=1, keepdims=True))
            alpha = jnp.exp2(m_old - m_new)
            p_bf = jnp.concatenate([jnp.exp2(c - m_new) for c in chunks], axis=1).astype(BF16)
            pv = _dot(p_bf, vs_ref[pl.ds(start, tk), :])
            acc_ref[rows, :] = alpha * acc_ref[rows, :] + pv[:, :LANES]
            l_ref[rows, :] = alpha * l_ref[rows, :] + pv[:, LANES:]
            m_ref[rows, :] = m_new
            if not causal:
                s_ref[rows, :] = scores(kt + 1, rows)

    def pre_loop(nk):
        j_row = lax.broadcasted_iota(jnp.int32, (1, nk), 1)
        bias_c = jnp.where(j_row * CMP_STRIDE + (CMP_BLOCK - 1) <= t_col, 0.0, NEG_INF)
        vis = jnp.where(t_col >= CMP_BLOCK - 1, 1.0, 0.0)
        q_all = jnp.concatenate([qpair(h) for h in range(hg)], axis=0)
        sc = _nt_dot(q_all, kc_ref[0:nk, :]).reshape(hg, tq, nk) + bias_c[None]
        e = jnp.exp2(sc - jnp.max(sc, axis=2, keepdims=True))
        r_c = vis[None] / jnp.sum(e, axis=2, keepdims=True)
        rhs_c = jnp.concatenate([vc_ref[0:nk, :], ov_ref[0:nk, :]], axis=1)
        pvc = _dot(e.reshape(hg * tq, nk).astype(BF16), rhs_c)
        imp = jnp.zeros((tq, LANES), F32)
        for h in range(hg):
            rows = slice(h * tq, (h + 1) * tq)
            oacc_ref[rows, :] = (gate(h, 0) * r_c[h]) * pvc[rows, :LANES]
            imp = imp + r_c[h] * pvc[rows, LANES:]

        n_win = WINDOW + tq
        kstart = pl.multiple_of(jnp.maximum(s0 - WINDOW, 0), tq)
        kpos_w = kstart + lax.broadcasted_iota(jnp.int32, (1, n_win), 1)
        bias_w = jnp.where((kpos_w <= t_col) & (kpos_w > t_col - WINDOW), 0.0, NEG_INF)
        sw = _nt_dot(q_all, kw_ref[pl.ds(kstart, n_win), :]).reshape(hg, tq, n_win) + bias_w[None]
        ew = jnp.exp2(sw - jnp.max(sw, axis=2, keepdims=True))
        ow = _dot(ew.reshape(hg * tq, n_win).astype(BF16), vw_ref[pl.ds(kstart, n_win), :])
        for h in range(hg):
            rows = slice(h * tq, (h + 1) * tq)
            oacc_ref[rows, :] += (gate(h, 2) / ow[rows, LANES:]) * ow[rows, :LANES]

        blk = lax.broadcasted_iota(jnp.int32, (tq, LANES), 1)
        cur = t_col >> 6
        forced = (blk == 0) | (blk == cur) | (blk == cur - 1)
        valid = blk <= cur
        score = jnp.where(valid & jnp.logical_not(forced), imp, NEG_INF)
        st = score.T
        blk_t = lax.broadcasted_iota(jnp.int32, (LANES, tq), 0).astype(F32)
        sel_t = jnp.zeros((LANES, tq), F32)
        for _ in range(SEL_TOPK - N_FORCED):
            mx = jnp.max(st, axis=0, keepdims=True)
            idx = jnp.min(jnp.where(st == mx, blk_t, float(LANES)), axis=0, keepdims=True)
            hit = blk_t == idx
            sel_t = jnp.where(hit, 1.0, sel_t)
            st = jnp.where(hit, -jnp.inf, st)
        sel = (sel_t.T > 0.5) | forced
        bias_s = jnp.where(sel & valid, 0.0, NEG_INF).astype(BF16)

        for h in range(hg):
            qaug_ref[h * tq:(h + 1) * tq, 0:LANES] = bias_s
            qaug_ref[h * tq:(h + 1) * tq, LANES:2 * LANES] = qpair(h)
        m_ref[...] = jnp.full(m_ref.shape, NEG_INF, F32)
        l_ref[...] = jnp.zeros(l_ref.shape, F32)
        acc_ref[...] = jnp.zeros(acc_ref.shape, F32)
        s_ref[...] = scores(0, slice(None))

    n_vis = (s0 + tq) // CMP_STRIDE - 1
    step = n_cmp_pad // CMP_VARIANTS
    for v in range(CMP_VARIANTS):
        lo, hi = v * step, (v + 1) * step

        @pl.when((n_vis > lo) & (n_vis <= hi) if v else n_vis <= hi)
        def _():
            pre_loop(hi)

    k_diag = s0 // tk

    def tile_pair(j, carry):
        softmax_pv(2 * j, False)
        softmax_pv(2 * j + 1, False)
        return carry

    lax.fori_loop(0, k_diag // 2, tile_pair, 0)

    @pl.when(k_diag % 2 == 1)
    def _():
        softmax_pv(k_diag - 1, False)

    softmax_pv(k_diag, True)

    outs = []
    for h in range(hg):
        rows = slice(h * tq, (h + 1) * tq)
        outs.append(oacc_ref[rows, :] + (gate(h, 1) / l_ref[rows, :]) * acc_ref[rows, :])
    lane = lax.broadcasted_iota(jnp.int32, (tq, LANES), 1)
    low = lane < HEAD_DIM
    is_g0 = g == 0
    for p in range(hg // 2):
        a = outs[2 * p]
        b = outs[2 * p + 1]
        xa = jnp.where(is_g0, a, pltpu.roll(a, HEAD_DIM, 1))
        xb = jnp.where(is_g0, pltpu.roll(b, HEAD_DIM, 1), b)
        o_ref[:, p * LANES:(p + 1) * LANES] = jnp.where(low, xa, xb).astype(BF16)


def _nsa_attn(q, qsw, ks, kw, vs, vw, kco, vco, gates, overlap):
    B, T, _ = q.shape
    tq, tk = TQ, TK_SEL
    n_cmp_pad = kco.shape[2]
    n_blk = T // SEL_BLOCK
    gw = HEADS_PER_GROUP * HEAD_DIM
    rows = HEADS_PER_GROUP * tq
    grid = (B, N_GROUPS, T // tq)
    once = pl.Buffered(1)
    return pl.pallas_call(
        functools.partial(_attn_body, tq=tq, tk=tk, n_cmp_pad=n_cmp_pad, n_blk=n_blk),
        grid=grid,
        in_specs=[
            pl.BlockSpec((None, tq, gw), lambda b, g, i: (b, i, g)),
            pl.BlockSpec((None, tq, gw), lambda b, g, i: (b, i, g)),
            pl.BlockSpec((None, None, T, 2 * LANES), lambda b, g, i: (b, g, 0, 0), pipeline_mode=once),
            pl.BlockSpec((None, None, T, LANES), lambda b, g, i: (b, g, 0, 0), pipeline_mode=once),
            pl.BlockSpec((None, T, 2 * LANES), lambda b, g, i: (b, 0, 0), pipeline_mode=once),
            pl.BlockSpec((None, T, 2 * LANES), lambda b, g, i: (b, 0, 0), pipeline_mode=once),
            pl.BlockSpec((None, None, n_cmp_pad, LANES), lambda b, g, i: (b, g, 0, 0)),
            pl.BlockSpec((None, n_cmp_pad, LANES), lambda b, g, i: (b, 0, 0)),
            pl.BlockSpec((None, None, tq, LANES), lambda b, g, i: (b, g, i, 0)),
            pl.BlockSpec(overlap.shape, lambda b, g, i: (0, 0)),
        ],
        out_specs=pl.BlockSpec((None, tq, gw), lambda b, g, i: (b, i, g)),
        out_shape=jax.ShapeDtypeStruct((B, T, N_HEADS * HEAD_DIM), BF16),
        scratch_shapes=[pltpu.VMEM((rows, 2 * LANES), BF16),
                        pltpu.VMEM((rows, tk), F32),
                        pltpu.VMEM((rows, LANES), F32),
                        pltpu.VMEM((rows, LANES), F32),
                        pltpu.VMEM((rows, LANES), F32),
                        pltpu.VMEM((rows, LANES), F32)],
        compiler_params=pltpu.CompilerParams(
            dimension_semantics=("parallel", "parallel", "arbitrary"),
            vmem_limit_bytes=VMEM_LIMIT),
        name="nsa_attn",
    )(q, qsw, ks, kw, vs, vw, kco, vco, gates, overlap)


def _mixer_out_proj(x_ref, xh_ref, a_ref, ah_ref, wo_ref, xin_ref, *, tm):
    halo = HALO_FFN
    lead = ah_ref.shape[0]
    a_ext = jnp.concatenate([ah_ref[...], a_ref[...]], axis=0)
    y = _dot(a_ext, wo_ref[...])
    xin_ref[0:halo, :] = xh_ref[...] + y[lead - halo:lead]
    xin_ref[halo:halo + tm, :] = x_ref[...] + y[lead:lead + tm]


def _mixer_pool(x_ref, xh_ref, gm_ref, pw_ref, pb_ref, ps_ref, xin_ref, *, tm, seq_tile):
    halo = HALO_FFN
    lead = xh_ref.shape[0]
    n_ext = lead + tm
    xe = jnp.concatenate([xh_ref[...], x_ref[...]], axis=0)
    row = lax.broadcasted_iota(jnp.int32, (n_ext, 1), 0)
    before_seq = (seq_tile == 0) & (row < lead)
    hm = jnp.where(before_seq, 0.0, _rmsnorm(xe, gm_ref[...]))
    t_seq = seq_tile * tm + row - lead
    gdim = xe.shape[1] // len(POOL_WINDOWS)
    ys = []
    for gi, w in enumerate(POOL_WINDOWS):
        cols = slice(gi * gdim, (gi + 1) * gdim)
        hg = hm[:, cols]
        s = hg
        shift = 1
        while shift < w:
            s = s + pltpu.roll(s, shift, 0)
            shift *= 2
        cnt = jnp.clip(t_seq + 1, 1, w).astype(F32)
        ys.append(_dot((s / cnt - hg).astype(BF16), pw_ref[gi]))
    y = (jnp.concatenate(ys, axis=1) + pb_ref[...]) * ps_ref[...]
    xin_ref[...] = (xe + y)[lead - halo:n_ext]


def _ffn_body(*refs, tm, tf, dff, tiles_per_seq, final_norm, mixer):
    n_mix = {"out_proj": 5, "pool": 6}[mixer]
    mix_refs, refs = refs[:n_mix], refs[n_mix:]
    (g_ref, wup_ref, cw_ref, cb_ref, wd_ref, gf_ref, o_ref,
     h_ref, xin_ref, a_ref, ug0_ref, uv0_ref, ug1_ref, uv1_ref) = refs
    u_refs = ((ug0_ref, uv0_ref), (ug1_ref, uv1_ref))
    i = pl.program_id(0)
    halo = HALO_FFN
    n = halo + tm
    seq_tile = i % tiles_per_seq
    if mixer == "out_proj":
        _mixer_out_proj(*mix_refs, xin_ref, tm=tm)
    else:
        _mixer_pool(*mix_refs, xin_ref, tm=tm, seq_tile=seq_tile)
    hn = _rmsnorm(xin_ref[...], g_ref[...])
    row = lax.broadcasted_iota(jnp.int32, (n, 1), 0)
    no_context = (seq_tile == 0) & (row < halo)
    h_ref[...] = jnp.where(no_context, 0.0, hn).astype(BF16)

    nj = dff // tf
    rb = FFN_ROW_BLOCK
    n_rb = tm // rb

    def up_proj(j, half, b):
        col = half * dff + j * tf
        r0 = 0 if b == 0 else halo + b * rb
        r1 = halo + (b + 1) * rb
        u_refs[j % 2][half][r0:r1, :] = _dot(h_ref[r0:r1, :], wup_ref[:, col:col + tf])

    def conv(j, half, b):
        col = half * dff + j * tf
        u = u_refs[j % 2][half][b * rb:halo + (b + 1) * rb, :]
        u1 = pltpu.roll(u, 1, 0)
        u2 = pltpu.roll(u, 2, 0)
        cw = cw_ref[:, col:col + tf]
        c = cw[0:1, :] * u2 + cw[1:2, :] * u1 + cw[2:3, :] * u + cb_ref[:, col:col + tf]
        return c[halo:halo + rb]

    for half in range(2):
        for b in range(n_rb):
            up_proj(0, half, b)
    for j in range(nj):
        for b in range(n_rb):
            rows = slice(b * rb, (b + 1) * rb)
            if j + 1 < nj:
                up_proj(j + 1, 0, b)
            cg = conv(j, 0, b)
            if j + 1 < nj:
                up_proj(j + 1, 1, b)
            cv = conv(j, 1, b)
            a = (cg * (1.0 / (1.0 + jnp.exp(-cg)))) * cv
            a_ref[rows, j * tf:(j + 1) * tf] = a.astype(BF16)

    y = xin_ref[halo:n, :] + _dot(a_ref[...], wd_ref[...])
    if final_norm:
        y = _rmsnorm(y, gf_ref[...])
    o_ref[...] = y


def _mixer_ffn(x2, seq_len, mixer, mixer_args, g, w_up, conv_w, conv_b, w_down, g_final, final_norm):
    N, D = x2.shape
    dff = w_down.shape[0]
    tm, tf = TM_FFN, TF_FFN
    resident = lambda a: pl.BlockSpec(a.shape, lambda i: (0,) * a.ndim, pipeline_mode=pl.Buffered(1))
    tile = pl.BlockSpec((tm, D), lambda i: (i, 0))

    def left_context(rows):
        return pl.BlockSpec((rows, D), lambda i: (jnp.maximum(i * (tm // rows) - 1, 0), 0))

    if mixer == "out_proj":
        attn, w_out = mixer_args
        mix_in = [x2, x2, attn, attn, w_out]
        mix_specs = [tile, left_context(HALO_FFN), tile, left_context(HALO_ATTN_BF16), resident(w_out)]
    else:
        mix_in = [x2, x2] + list(mixer_args)
        mix_specs = [tile, left_context(HALO_POOL)] + [resident(a) for a in mixer_args]
    weights = [g, w_up, conv_w, conv_b, w_down, g_final]
    return pl.pallas_call(
        functools.partial(_ffn_body, tm=tm, tf=tf, dff=dff, tiles_per_seq=seq_len // tm,
                          final_norm=final_norm, mixer=mixer),
        grid=(N // tm,),
        in_specs=mix_specs + [resident(a) for a in weights],
        out_specs=tile,
        out_shape=jax.ShapeDtypeStruct((N, D), F32),
        scratch_shapes=[pltpu.VMEM((HALO_FFN + tm, D), BF16),
                        pltpu.VMEM((HALO_FFN + tm, D), F32),
                        pltpu.VMEM((tm, dff), BF16),
                        ] + [pltpu.VMEM((HALO_FFN + tm, tf), F32)] * 4,
        compiler_params=pltpu.CompilerParams(dimension_semantics=("parallel",),
                                             vmem_limit_bytes=VMEM_LIMIT),
        name="mixer_ffn_" + mixer,
    )(*mix_in, *weights)


def _pad_cols(w, width):
    return jnp.pad(w, ((0, 0), (0, width - w.shape[1])))


def _inproj_weight(w_in):
    D = w_in.shape[0]
    kv = N_GROUPS * HEAD_DIM
    sizes = [N_HEADS * HEAD_DIM] + [kv] * 6 + [N_BRANCH * N_HEADS]
    offs = np.concatenate([[0], np.cumsum(sizes)])
    q, k_c, v_c, k_s, v_s, k_w, v_w, gt = [w_in[:, offs[n]:offs[n + 1]] for n in range(8)]
    per_group = lambda w, n: [_pad_cols(w[:, g * n:(g + 1) * n], LANES) for g in range(N_GROUPS)]
    cols = ([q] + per_group(k_s, HEAD_DIM) + per_group(k_w, HEAD_DIM) + [k_c, v_c, v_s, v_w]
            + per_group(gt, N_BRANCH * HEADS_PER_GROUP))
    return jnp.concatenate(cols, axis=1).astype(BF16)


def _rope_tables(pos):
    inv = 1.0 / (ROPE_THETA ** (jnp.arange(0, HEAD_DIM, 2, dtype=F32) / HEAD_DIM))
    ang = pos.astype(F32)[:, None] * inv[None, :]
    return jnp.cos(ang), jnp.sin(ang)


def _rot_half_cols(w):
    half = HEAD_DIM // 2
    return jnp.concatenate([-w[..., half:], w[..., :half]], axis=-1)


def _overlap_matrix(n_cmp_pad, n_blk):
    j = np.arange(n_cmp_pad)[:, None]
    s = np.arange(LANES)[None, :]
    lo = np.maximum(j * CMP_STRIDE, s * SEL_BLOCK)
    hi = np.minimum(j * CMP_STRIDE + CMP_BLOCK, (s + 1) * SEL_BLOCK)
    return jnp.asarray(np.clip(hi - lo, 0, None) / CMP_BLOCK, dtype=BF16)


def kernel(x, norm_mix_0, nsa_w_in, cmp_k_pos, cmp_k_w1, cmp_k_b1, cmp_k_w2, cmp_k_b2, cmp_v_pos, cmp_v_w1, cmp_v_b1, cmp_v_w2, cmp_v_b2, nsa_w_out, norm_ffn_0, ffn_up_0, ffn_conv_w_0, ffn_conv_b_0, ffn_down_0, norm_mix_1, pool_w, pool_b, pool_scale, norm_ffn_1, ffn_up_1, ffn_conv_w_1, ffn_conv_b_1, ffn_down_1, norm_final):
    B, T, D = x.shape
    assert D == N_HEADS * HEAD_DIM and SEL_BLOCK == 64
    assert T % TM_FFN == 0 and T % TK_SEL == 0 and TK_SEL % TQ == 0 and T >= WINDOW + TQ
    n_cmp_pad = T // CMP_STRIDE
    n_blk = T // SEL_BLOCK
    assert n_blk <= LANES
    row = lambda v: v.reshape(1, -1)

    cos, sin = _rope_tables(jnp.arange(T))
    cos4 = jnp.tile(cos, (1, 4))
    sin4 = jnp.tile(jnp.concatenate([-sin, sin], axis=1), (1, 2))
    q, qsw, ks, kw, kc, vc, vs, vw, gates = _in_proj(
        x, row(norm_mix_0), _inproj_weight(nsa_w_in), cos4, sin4)

    ccos, csin = _rope_tables(jnp.arange(n_cmp_pad) * CMP_STRIDE + (CMP_BLOCK - 1))
    zeros64 = jnp.zeros((n_cmp_pad, HEAD_DIM), F32)
    ccos2 = jnp.concatenate([ccos, ccos, zeros64], axis=1)
    csin2 = jnp.concatenate([csin, csin, zeros64], axis=1)
    pos_rows = lambda p: jnp.broadcast_to(p.reshape(1, -1), (SUBLANES, p.size)).astype(BF16)
    zero_w2 = jnp.zeros_like(cmp_v_w2)
    kparams = (cmp_k_w1.astype(BF16), pos_rows(cmp_k_pos), row(cmp_k_b1),
               jnp.concatenate([cmp_k_w2, _rot_half_cols(cmp_k_w2)], axis=1).astype(BF16),
               row(jnp.concatenate([cmp_k_b2, _rot_half_cols(cmp_k_b2)])))
    vparams = (cmp_v_w1.astype(BF16), pos_rows(cmp_v_pos), row(cmp_v_b1),
               jnp.concatenate([cmp_v_w2, zero_w2], axis=1).astype(BF16),
               jnp.concatenate([zero_w2, cmp_v_w2], axis=1).astype(BF16),
               row(jnp.concatenate([cmp_v_b2, cmp_v_b2])))
    kco, vco = _compress(kc, vc, kparams, vparams, ccos2, csin2)

    o = _nsa_attn(q, qsw, ks, kw, vs, vw, kco, vco, gates, _overlap_matrix(n_cmp_pad, n_blk))

    x2 = _mixer_ffn(x.reshape(B * T, D), T, "out_proj",
                    (o.reshape(B * T, D), nsa_w_out.astype(BF16)),
                    row(norm_ffn_0), ffn_up_0.astype(BF16), ffn_conv_w_0, row(ffn_conv_b_0),
                    ffn_down_0.astype(BF16), row(norm_final), False)
    x2 = _mixer_ffn(x2, T, "pool",
                    (row(norm_mix_1), pool_w.astype(BF16), row(pool_b.reshape(-1)), row(pool_scale)),
                    row(norm_ffn_1), ffn_up_1.astype(BF16), ffn_conv_w_1, row(ffn_conv_b_1),
                    ffn_down_1.astype(BF16), row(norm_final), True)
    return x2.reshape(B, T, D)
```

```python
import functools

import numpy as np
import jax
import jax.numpy as jnp
from jax import lax
from jax.experimental import pallas as pl
from jax.experimental.pallas import tpu as pltpu

F32 = jnp.float32
BF16 = jnp.bfloat16

N_HEADS = 16
HEAD_DIM = 64
N_GROUPS = 2
HEADS_PER_GROUP = N_HEADS // N_GROUPS
N_BRANCH = 3
CMP_STRIDE = 16
CMP_BLOCK = 32
SEL_BLOCK = 64
SEL_TOPK = 16
WINDOW = 512
ROPE_THETA = 10000.0
POOL_WINDOWS = (2, 4, 8, 16)
CONV_WIDTH = 3
NORM_EPS = 1e-6
NEG_INF = -1e30
FORCE_BONUS = 1e4
N_FORCED = 3
LOG2_E = 1.4426950408889634

LANES = 128
SUBLANES = 8
VMEM_LIMIT = 56 * 1024 * 1024

TM_PROJ = 512
TQ = 256
TK_SEL = 512
CMP_VARIANTS = 2
SEL_ROW_CHUNK = 512
TM_FFN = 512
TF_FFN = 256
FFN_ROW_BLOCK = 128
HALO_FFN = 8
HALO_ATTN_BF16 = 16
HALO_POOL = 32


def _rmsnorm(x, g):
    return x * lax.rsqrt(jnp.mean(x * x, axis=-1, keepdims=True) + NORM_EPS) * g


def _nt_dot(a, b):
    return lax.dot_general(a, b, (((1,), (1,)), ((), ())), preferred_element_type=F32)


def _dot(a, b):
    return jnp.dot(a, b, preferred_element_type=F32)


def _inproj_body(x_ref, g_ref, w_ref, cos_ref, sin_ref,
                 q_ref, qsw_ref, ks_ref, kw_ref, kc_ref, vc_ref, vs_ref, vw_ref, gate_ref, *, tm):
    i = pl.program_id(1)
    h = _rmsnorm(x_ref[...], g_ref[...]).astype(BF16)
    res = _dot(h, w_ref[...])
    cos = cos_ref[...]
    sin = sin_ref[...]
    lane = lax.broadcasted_iota(jnp.int32, (tm, LANES), 1)
    first_half = (lane & (HEAD_DIM - 1)) < (HEAD_DIM // 2)

    def rope(xc):
        sw = jnp.where(first_half, pltpu.roll(xc, LANES - HEAD_DIM // 2, 1),
                       pltpu.roll(xc, HEAD_DIM // 2, 1))
        return xc * cos + sw * sin

    scale = HEAD_DIM ** -0.5 * LOG2_E
    n_pairs = N_HEADS // 2
    for p in range(n_pairs):
        qc = rope(res[:, p * LANES:(p + 1) * LANES]) * scale
        q_ref[:, p * LANES:(p + 1) * LANES] = qc.astype(BF16)
        qsw_ref[:, p * LANES:(p + 1) * LANES] = pltpu.roll(qc, HEAD_DIM, 1).astype(BF16)
    off = n_pairs * LANES
    row_t = i * tm + lax.broadcasted_iota(jnp.int32, (tm, LANES), 0)
    onehot = jnp.where(lane == (row_t >> 6), 1.0, 0.0).astype(BF16)
    for g in range(N_GROUPS):
        ks_ref[g, :, 0:LANES] = onehot
        ks_ref[g, :, LANES:2 * LANES] = rope(res[:, off + g * LANES: off + (g + 1) * LANES]).astype(BF16)
    off += N_GROUPS * LANES
    for g in range(N_GROUPS):
        kw_ref[g] = rope(res[:, off + g * LANES: off + (g + 1) * LANES]).astype(BF16)
    off += N_GROUPS * LANES
    for g in range(N_GROUPS):
        kc_ref[g] = res[:, off + g * HEAD_DIM:off + (g + 1) * HEAD_DIM]
        vc_ref[g] = res[:, off + LANES + g * HEAD_DIM:off + LANES + (g + 1) * HEAD_DIM]
    ones = jnp.ones((tm, LANES), BF16)
    vs_ref[:, 0:LANES] = res[:, off + 2 * LANES:off + 3 * LANES].astype(BF16)
    vs_ref[:, LANES:2 * LANES] = ones
    vw_ref[:, 0:LANES] = res[:, off + 3 * LANES:off + 4 * LANES].astype(BF16)
    vw_ref[:, LANES:2 * LANES] = ones
    off += 4 * LANES
    for g in range(N_GROUPS):
        z = res[:, off + g * LANES: off + (g + 1) * LANES]
        gate_ref[g] = 1.0 / (1.0 + jnp.exp(-z))


def _in_proj(x, g, wp, cos4, sin4):
    B, T, D = x.shape
    tm = TM_PROJ
    ncols = wp.shape[1]
    grid = (B, T // tm)
    tok = lambda last: pl.BlockSpec((None, tm, last), lambda b, i: (b, i, 0))
    grp = lambda last: pl.BlockSpec((None, N_GROUPS, tm, last), lambda b, i: (b, 0, i, 0))
    out_shape = (
        jax.ShapeDtypeStruct((B, T, N_HEADS * HEAD_DIM), BF16),
        jax.ShapeDtypeStruct((B, T, N_HEADS * HEAD_DIM), BF16),
        jax.ShapeDtypeStruct((B, N_GROUPS, T, 2 * LANES), BF16),
        jax.ShapeDtypeStruct((B, N_GROUPS, T, LANES), BF16),
        jax.ShapeDtypeStruct((B, N_GROUPS, T, HEAD_DIM), F32),
        jax.ShapeDtypeStruct((B, N_GROUPS, T, HEAD_DIM), F32),
        jax.ShapeDtypeStruct((B, T, 2 * LANES), BF16),
        jax.ShapeDtypeStruct((B, T, 2 * LANES), BF16),
        jax.ShapeDtypeStruct((B, N_GROUPS, T, LANES), F32),
    )
    return pl.pallas_call(
        functools.partial(_inproj_body, tm=tm),
        grid=grid,
        in_specs=[tok(D),
                  pl.BlockSpec((1, D), lambda b, i: (0, 0)),
                  pl.BlockSpec((D, ncols), lambda b, i: (0, 0)),
                  pl.BlockSpec((tm, LANES), lambda b, i: (i, 0)),
                  pl.BlockSpec((tm, LANES), lambda b, i: (i, 0))],
        out_specs=(tok(N_HEADS * HEAD_DIM), tok(N_HEADS * HEAD_DIM), grp(2 * LANES), grp(LANES),
                   grp(HEAD_DIM), grp(HEAD_DIM), tok(2 * LANES), tok(2 * LANES), grp(LANES)),
        out_shape=out_shape,
        compiler_params=pltpu.CompilerParams(dimension_semantics=("parallel", "parallel"),
                                             vmem_limit_bytes=VMEM_LIMIT),
        name="in_proj",
    )(x, g, wp, cos4, sin4)


def _gelu_tanh(x):
    return 0.5 * x * (1.0 + jnp.tanh(np.sqrt(2.0 / np.pi).astype(np.float32) * (x + 0.044715 * (x * x * x))))


def _compress_body(kc_ref, vc_ref,
                   kw1_ref, kpos_ref, kb1_ref, kw2_ref, kb2_ref,
                   vw1_ref, vpos_ref, vb1_ref, vw2a_ref, vw2b_ref, vb2_ref,
                   cos_ref, sin_ref, kco_ref, vco_ref, *, n_rows, half):
    row = lax.broadcasted_iota(jnp.int32, (n_rows, LANES), 0)
    keep = row < (n_rows - 1)

    def hidden_act(c_ref, w1_ref, pos_ref, b1_ref):
        posb = _dot(pos_ref[...], w1_ref[...])[0:1, :] + b1_ref[...]
        outs = []
        for g in range(N_GROUPS):
            a = jnp.zeros((n_rows, w1_ref.shape[1]), F32)
            b = jnp.zeros((n_rows, w1_ref.shape[1]), F32)
            for t in range(CMP_STRIDE):
                c = c_ref[g, pl.ds(t, n_rows, stride=CMP_STRIDE), :].astype(BF16)
                a = a + _dot(c, w1_ref[t * HEAD_DIM:(t + 1) * HEAD_DIM, :])
                b = b + _dot(c, w1_ref[half + t * HEAD_DIM:half + (t + 1) * HEAD_DIM, :])
            pre = a + pltpu.roll(b, n_rows - 1, 0) + posb
            outs.append(_gelu_tanh(pre).astype(BF16))
        return outs

    hk = hidden_act(kc_ref, kw1_ref, kpos_ref, kb1_ref)
    cos = cos_ref[...]
    sin = sin_ref[...]
    for g in range(N_GROUPS):
        r = _dot(hk[g], kw2_ref[...]) + kb2_ref[...]
        kco_ref[g] = jnp.where(keep, r * cos + pltpu.roll(r, HEAD_DIM, 1) * sin, 0.0).astype(BF16)
    hv = hidden_act(vc_ref, vw1_ref, vpos_ref, vb1_ref)
    v = _dot(hv[0], vw2a_ref[...]) + _dot(hv[1], vw2b_ref[...]) + vb2_ref[...]
    vco_ref[...] = jnp.where(keep, v, 0.0).astype(BF16)


def _compress(kc2, vc2, kparams, vparams, ccos, csin):
    B, _, T, dh = kc2.shape
    n_rows = T // CMP_STRIDE
    full = lambda a: pl.BlockSpec(a.shape, lambda b: (0,) * a.ndim)
    blk = pl.BlockSpec((None, N_GROUPS, T, dh), lambda b: (b, 0, 0, 0), pipeline_mode=pl.Buffered(1))
    weights = list(kparams) + list(vparams) + [ccos, csin]
    return pl.pallas_call(
        functools.partial(_compress_body, n_rows=n_rows, half=CMP_STRIDE * dh),
        grid=(B,),
        in_specs=[blk, blk] + [full(a) for a in weights],
        out_specs=(pl.BlockSpec((None, N_GROUPS, n_rows, LANES), lambda b: (b, 0, 0, 0)),
                   pl.BlockSpec((None, n_rows, LANES), lambda b: (b, 0, 0))),
        out_shape=(jax.ShapeDtypeStruct((B, N_GROUPS, n_rows, LANES), BF16),
                   jax.ShapeDtypeStruct((B, n_rows, LANES), BF16)),
        compiler_params=pltpu.CompilerParams(dimension_semantics=("parallel",),
                                             vmem_limit_bytes=VMEM_LIMIT),
        name="compress",
    )(kc2, vc2, *weights)


def _softmax_rows(s):
    m = jnp.max(s, axis=1, keepdims=True)
    e = jnp.exp2(s - m)
    return e, jnp.sum(e, axis=1, keepdims=True)


def _attn_body(q_ref, qsw_ref, ks_ref, kw_ref, vs_ref, vw_ref, kc_ref, vc_ref, gate_ref, ov_ref,
               o_ref, qaug_ref, s_ref, m_ref, l_ref, acc_ref, oacc_ref, *, tq, tk, n_cmp_pad):
    hg = HEADS_PER_GROUP
    g = pl.program_id(1)
    i = pl.program_id(2)
    s0 = i * tq
    t_col = s0 + lax.broadcasted_iota(jnp.int32, (tq, 1), 0)

    rc = SEL_ROW_CHUNK
    n_chunks = hg * tq // rc
    reps = rc // tq

    def qpair(h):
        src = q_ref if h % 2 == 0 else qsw_ref
        return src[:, (h // 2) * LANES:(h // 2 + 1) * LANES]

    gates = gate_ref[...]

    def gate(h, branch):
        c = N_BRANCH * h + branch
        return gates[:, c:c + 1]

    n_lc = tk // LANES

    def scores(kt, rows):
        start = pl.multiple_of(kt * tk, tk)
        return _nt_dot(qaug_ref[rows, :], ks_ref[pl.ds(start, tk), :])

    def softmax_pv(kt, causal):
        start = pl.multiple_of(kt * tk, tk)
        if causal:
            kpos = start + lax.broadcasted_iota(jnp.int32, (1, tk), 1)
            bias_d = jnp.where(kpos <= t_col, 0.0, NEG_INF)
        for p in range(n_chunks):
            rows = slice(p * rc, (p + 1) * rc)
            s = s_ref[rows, :]
            if causal:
                s = jnp.concatenate([s[r * tq:(r + 1) * tq] + bias_d for r in range(reps)], axis=0)
            chunks = [s[:, c * LANES:(c + 1) * LANES] for c in range(n_lc)]
            mx = functools.reduce(jnp.maximum, chunks)
            m_old = m_ref[rows, :]
            m_new = jnp.maximum(m_old, jnp.max(mx, axis=1, keepdims=True))
            alpha = jnp.exp2(m_old - m_new)
            p_bf = jnp.concatenate([jnp.exp2(c - m_new) for c in chunks], axis=1).astype(BF16)
            pv = _dot(p_bf, vs_ref[pl.ds(start, tk), :])
            acc_ref[rows, :] = alpha * acc_ref[rows, :] + pv[:, :LANES]
            l_ref[rows, :] = alpha * l_ref[rows, :] + pv[:, LANES:]
            m_ref[rows, :] = m_new
            if not causal:
                s_ref[rows, :] = scores(kt + 1, rows)

    def pre_loop(nk):
        j_row = lax.broadcasted_iota(jnp.int32, (1, nk), 1)
        bias_c = jnp.where(j_row * CMP_STRIDE + (CMP_BLOCK - 1) <= t_col, 0.0, NEG_INF)
        vis = jnp.where(t_col >= CMP_BLOCK - 1, 1.0, 0.0)
        q_all = jnp.concatenate([qpair(h) for h in range(hg)], axis=0)
        sc = _nt_dot(q_all, kc_ref[0:nk, :]).reshape(hg, tq, nk) + bias_c[None]
        e = jnp.exp2(sc - jnp.max(sc, axis=2, keepdims=True))
        r_c = vis[None] / jnp.sum(e, axis=2, keepdims=True)
        rhs_c = jnp.concatenate([vc_ref[0:nk, :], ov_ref[0:nk, :]], axis=1)
        pvc = _dot(e.reshape(hg * tq, nk).astype(BF16), rhs_c)
        imp = jnp.zeros((tq, LANES), F32)
        for h in range(hg):
            rows = slice(h * tq, (h + 1) * tq)
            oacc_ref[rows, :] = (gate(h, 0) * r_c[h]) * pvc[rows, :LANES]
            imp = imp + r_c[h] * pvc[rows, LANES:]

        n_win = WINDOW + tq
        kstart = pl.multiple_of(jnp.maximum(s0 - WINDOW, 0), tq)
        kpos_w = kstart + lax.broadcasted_iota(jnp.int32, (1, n_win), 1)
        bias_w = jnp.where((kpos_w <= t_col) & (kpos_w > t_col - WINDOW), 0.0, NEG_INF)
        sw = _nt_dot(q_all, kw_ref[pl.ds(kstart, n_win), :]).reshape(hg, tq, n_win) + bias_w[None]
        ew = jnp.exp2(sw - jnp.max(sw, axis=2, keepdims=True))
        ow = _dot(ew.reshape(hg * tq, n_win).astype(BF16), vw_ref[pl.ds(kstart, n_win), :])
        for h in range(hg):
            rows = slice(h * tq, (h + 1) * tq)
            oacc_ref[rows, :] += (gate(h, 2) / ow[rows, LANES:]) * ow[rows, :LANES]

        blk = lax.broadcasted_iota(jnp.int32, (tq, LANES), 1)
        cur = t_col >> 6
        forced = (blk == 0) | (blk == cur) | (blk == cur - 1)
        valid = blk <= cur
        score = jnp.where(valid & jnp.logical_not(forced), imp, NEG_INF)
        st = score.T
        blk_t = lax.broadcasted_iota(jnp.int32, (LANES, tq), 0).astype(F32)
        sel_t = jnp.zeros((LANES, tq), F32)
        for _ in range(SEL_TOPK - N_FORCED):
            mx = jnp.max(st, axis=0, keepdims=True)
            idx = jnp.min(jnp.where(st == mx, blk_t, float(LANES)), axis=0, keepdims=True)
            hit = blk_t == idx
            sel_t = jnp.where(hit, 1.0, sel_t)
            st = jnp.where(hit, -jnp.inf, st)
        sel = (sel_t.T > 0.5) | forced
        bias_s = jnp.where(sel & valid, 0.0, NEG_INF).astype(BF16)

        for h in range(hg):
            qaug_ref[h * tq:(h + 1) * tq, 0:LANES] = bias_s
            qaug_ref[h * tq:(h + 1) * tq, LANES:2 * LANES] = qpair(h)
        m_ref[...] = jnp.full(m_ref.shape, NEG_INF, F32)
        l_ref[...] = jnp.zeros(l_ref.shape, F32)
        acc_ref[...] = jnp.zeros(acc_ref.shape, F32)
        s_ref[...] = scores(0, slice(None))

    n_vis = (s0 + tq) // CMP_STRIDE - 1
    step = n_cmp_pad // CMP_VARIANTS
    for v in range(CMP_VARIANTS):
        lo, hi = v * step, (v + 1) * step

        @pl.when((n_vis > lo) & (n_vis <= hi) if v else n_vis <= hi)
        def _():
            pre_loop(hi)

    k_diag = s0 // tk

    def tile_pair(j, carry):
        softmax_pv(2 * j, False)
        softmax_pv(2 * j + 1, False)
        return carry

    lax.fori_loop(0, k_diag // 2, tile_pair, 0)

    @pl.when(k_diag % 2 == 1)
    def _():
        softmax_pv(k_diag - 1, False)

    softmax_pv(k_diag, True)

    outs = []
    for h in range(hg):
        rows = slice(h * tq, (h + 1) * tq)
        outs.append(oacc_ref[rows, :] + (gate(h, 1) / l_ref[rows, :]) * acc_ref[rows, :])
    lane = lax.broadcasted_iota(jnp.int32, (tq, LANES), 1)
    low = lane < HEAD_DIM
    is_g0 = g == 0
    for p in range(hg // 2):
        a = outs[2 * p]
        b = outs[2 * p + 1]
        xa = jnp.where(is_g0, a, pltpu.roll(a, HEAD_DIM, 1))
        xb = jnp.where(is_g0, pltpu.roll(b, HEAD_DIM, 1), b)
        o_ref[:, p * LANES:(p + 1) * LANES] = jnp.where(low, xa, xb).astype(BF16)


def _nsa_attn(q, qsw, ks, kw, vs, vw, kco, vco, gates, overlap):
    B, T, _ = q.shape
    tq, tk = TQ, TK_SEL
    n_cmp_pad = kco.shape[2]
    gw = HEADS_PER_GROUP * HEAD_DIM
    rows = HEADS_PER_GROUP * tq
    grid = (B, N_GROUPS, T // tq)
    once = pl.Buffered(1)
    return pl.pallas_call(
        functools.partial(_attn_body, tq=tq, tk=tk, n_cmp_pad=n_cmp_pad),
        grid=grid,
        in_specs=[
            pl.BlockSpec((None, tq, gw), lambda b, g, i: (b, i, g)),
            pl.BlockSpec((None, tq, gw), lambda b, g, i: (b, i, g)),
            pl.BlockSpec((None, None, T, 2 * LANES), lambda b, g, i: (b, g, 0, 0), pipeline_mode=once),
            pl.BlockSpec((None, None, T, LANES), lambda b, g, i: (b, g, 0, 0), pipeline_mode=once),
            pl.BlockSpec((None, T, 2 * LANES), lambda b, g, i: (b, 0, 0), pipeline_mode=once),
            pl.BlockSpec((None, T, 2 * LANES), lambda b, g, i: (b, 0, 0), pipeline_mode=once),
            pl.BlockSpec((None, None, n_cmp_pad, LANES), lambda b, g, i: (b, g, 0, 0)),
            pl.BlockSpec((None, n_cmp_pad, LANES), lambda b, g, i: (b, 0, 0)),
            pl.BlockSpec((None, None, tq, LANES), lambda b, g, i: (b, g, i, 0)),
            pl.BlockSpec(overlap.shape, lambda b, g, i: (0, 0)),
        ],
        out_specs=pl.BlockSpec((None, tq, gw), lambda b, g, i: (b, i, g)),
        out_shape=jax.ShapeDtypeStruct((B, T, N_HEADS * HEAD_DIM), BF16),
        scratch_shapes=[pltpu.VMEM((rows, 2 * LANES), BF16),
                        pltpu.VMEM((rows, tk), F32),
                        pltpu.VMEM((rows, LANES), F32),
                        pltpu.VMEM((rows, LANES), F32),
                        pltpu.VMEM((rows, LANES), F32),
                        pltpu.VMEM((rows, LANES), F32)],
        compiler_params=pltpu.CompilerParams(
            dimension_semantics=("parallel", "parallel", "arbitrary"),
            vmem_limit_bytes=VMEM_LIMIT),
        name="nsa_attn",
    )(q, qsw, ks, kw, vs, vw, kco, vco, gates, overlap)


def _mixer_out_proj(x_ref, xh_ref, a_ref, ah_ref, wo_ref, xin_ref, *, tm):
    halo = HALO_FFN
    lead = ah_ref.shape[0]
    a_ext = jnp.concatenate([ah_ref[...], a_ref[...]], axis=0)
    y = _dot(a_ext, wo_ref[...])
    xin_ref[0:halo, :] = xh_ref[...] + y[lead - halo:lead]
    xin_ref[halo:halo + tm, :] = x_ref[...] + y[lead:lead + tm]


def _mixer_pool(x_ref, xh_ref, gm_ref, pw_ref, pb_ref, ps_ref, xin_ref, *, tm, seq_tile):
    halo = HALO_FFN
    lead = xh_ref.shape[0]
    n_ext = lead + tm
    xe = jnp.concatenate([xh_ref[...], x_ref[...]], axis=0)
    row = lax.broadcasted_iota(jnp.int32, (n_ext, 1), 0)
    before_seq = (seq_tile == 0) & (row < lead)
    hm = jnp.where(before_seq, 0.0, _rmsnorm(xe, gm_ref[...]))
    t_seq = seq_tile * tm + row - lead
    gdim = xe.shape[1] // len(POOL_WINDOWS)
    ys = []
    for gi, w in enumerate(POOL_WINDOWS):
        cols = slice(gi * gdim, (gi + 1) * gdim)
        hg = hm[:, cols]
        s = hg
        shift = 1
        while shift < w:
            s = s + pltpu.roll(s, shift, 0)
            shift *= 2
        cnt = jnp.clip(t_seq + 1, 1, w).astype(F32)
        ys.append(_dot((s / cnt - hg).astype(BF16), pw_ref[gi]))
    y = (jnp.concatenate(ys, axis=1) + pb_ref[...]) * ps_ref[...]
    xin_ref[...] = (xe + y)[lead - halo:n_ext]


def _ffn_body(*refs, tm, tf, dff, tiles_per_seq, final_norm, mixer):
    n_mix = {"out_proj": 5, "pool": 6}[mixer]
    mix_refs, refs = refs[:n_mix], refs[n_mix:]
    (g_ref, wup_ref, cw_ref, cb_ref, wd_ref, gf_ref, o_ref,
     h_ref, xin_ref, a_ref, ug0_ref, uv0_ref, ug1_ref, uv1_ref) = refs
    u_refs = ((ug0_ref, uv0_ref), (ug1_ref, uv1_ref))
    i = pl.program_id(0)
    halo = HALO_FFN
    n = halo + tm
    seq_tile = i % tiles_per_seq
    if mixer == "out_proj":
        _mixer_out_proj(*mix_refs, xin_ref, tm=tm)
    else:
        _mixer_pool(*mix_refs, xin_ref, tm=tm, seq_tile=seq_tile)
    hn = _rmsnorm(xin_ref[...], g_ref[...])
    row = lax.broadcasted_iota(jnp.int32, (n, 1), 0)
    no_context = (seq_tile == 0) & (row < halo)
    h_ref[...] = jnp.where(no_context, 0.0, hn).astype(BF16)

    nj = dff // tf
    rb = FFN_ROW_BLOCK
    n_rb = tm // rb

    def up_proj(j, half, b):
        col = half * dff + j * tf
        r0 = 0 if b == 0 else halo + b * rb
        r1 = halo + (b + 1) * rb
        u_refs[j % 2][half][r0:r1, :] = _dot(h_ref[r0:r1, :], wup_ref[:, col:col + tf])

    def conv(j, half, b):
        col = half * dff + j * tf
        u = u_refs[j % 2][half][b * rb:halo + (b + 1) * rb, :]
        u1 = pltpu.roll(u, 1, 0)
        u2 = pltpu.roll(u, 2, 0)
        cw = cw_ref[:, col:col + tf]
        c = cw[0:1, :] * u2 + cw[1:2, :] * u1 + cw[2:3, :] * u + cb_ref[:, col:col + tf]
        return c[halo:halo + rb]

    for half in range(2):
        for b in range(n_rb):
            up_proj(0, half, b)
    for j in range(nj):
        for b in range(n_rb):
            rows = slice(b * rb, (b + 1) * rb)
            if j + 1 < nj:
                up_proj(j + 1, 0, b)
            cg = conv(j, 0, b)
            if j + 1 < nj:
                up_proj(j + 1, 1, b)
            cv = conv(j, 1, b)
            a = (cg * (1.0 / (1.0 + jnp.exp(-cg)))) * cv
            a_ref[rows, j * tf:(j + 1) * tf] = a.astype(BF16)

    y = xin_ref[halo:n, :] + _dot(a_ref[...], wd_ref[...])
    if final_norm:
        y = _rmsnorm(y, gf_ref[...])
    o_ref[...] = y


def _mixer_ffn(x2, seq_len, mixer, mixer_args, g, w_up, conv_w, conv_b, w_down, g_final, final_norm):
    N, D = x2.shape
    dff = w_down.shape[0]
    tm, tf = TM_FFN, TF_FFN
    resident = lambda a: pl.BlockSpec(a.shape, lambda i: (0,) * a.ndim, pipeline_mode=pl.Buffered(1))
    tile = pl.BlockSpec((tm, D), lambda i: (i, 0))

    def left_context(rows):
        return pl.BlockSpec((rows, D), lambda i: (jnp.maximum(i * (tm // rows) - 1, 0), 0))

    if mixer == "out_proj":
        attn, w_out = mixer_args
        mix_in = [x2, x2, attn, attn, w_out]
        mix_specs = [tile, left_context(HALO_FFN), tile, left_context(HALO_ATTN_BF16), resident(w_out)]
    else:
        mix_in = [x2, x2] + list(mixer_args)
        mix_specs = [tile, left_context(HALO_POOL)] + [resident(a) for a in mixer_args]
    weights = [g, w_up, conv_w, conv_b, w_down, g_final]
    return pl.pallas_call(
        functools.partial(_ffn_body, tm=tm, tf=tf, dff=dff, tiles_per_seq=seq_len // tm,
                          final_norm=final_norm, mixer=mixer),
        grid=(N // tm,),
        in_specs=mix_specs + [resident(a) for a in weights],
        out_specs=tile,
        out_shape=jax.ShapeDtypeStruct((N, D), F32),
        scratch_shapes=[pltpu.VMEM((HALO_FFN + tm, D), BF16),
                        pltpu.VMEM((HALO_FFN + tm, D), F32),
                        pltpu.VMEM((tm, dff), BF16),
                        ] + [pltpu.VMEM((HALO_FFN + tm, tf), F32)] * 4,
        compiler_params=pltpu.CompilerParams(dimension_semantics=("parallel",),
                                             vmem_limit_bytes=VMEM_LIMIT),
        name="mixer_ffn_" + mixer,
    )(*mix_in, *weights)


def _pad_cols(w, width):
    return jnp.pad(w, ((0, 0), (0, width - w.shape[1])))


def _inproj_weight(w_in):
    D = w_in.shape[0]
    kv = N_GROUPS * HEAD_DIM
    sizes = [N_HEADS * HEAD_DIM] + [kv] * 6 + [N_BRANCH * N_HEADS]
    offs = np.concatenate([[0], np.cumsum(sizes)])
    q, k_c, v_c, k_s, v_s, k_w, v_w, gt = [w_in[:, offs[n]:offs[n + 1]] for n in range(8)]
    per_group = lambda w, n: [_pad_cols(w[:, g * n:(g + 1) * n], LANES) for g in range(N_GROUPS)]
    cols = ([q] + per_group(k_s, HEAD_DIM) + per_group(k_w, HEAD_DIM) + [k_c, v_c, v_s, v_w]
            + per_group(gt, N_BRANCH * HEADS_PER_GROUP))
    return jnp.concatenate(cols, axis=1).astype(BF16)


def _rope_tables(pos):
    inv = 1.0 / (ROPE_THETA ** (jnp.arange(0, HEAD_DIM, 2, dtype=F32) / HEAD_DIM))
    ang = pos.astype(F32)[:, None] * inv[None, :]
    return jnp.cos(ang), jnp.sin(ang)


def _rot_half_cols(w):
    half = HEAD_DIM // 2
    return jnp.concatenate([-w[..., half:], w[..., :half]], axis=-1)


def _overlap_matrix(n_cmp_pad):
    j = np.arange(n_cmp_pad)[:, None]
    s = np.arange(LANES)[None, :]
    lo = np.maximum(j * CMP_STRIDE, s * SEL_BLOCK)
    hi = np.minimum(j * CMP_STRIDE + CMP_BLOCK, (s + 1) * SEL_BLOCK)
    return jnp.asarray(np.clip(hi - lo, 0, None) / CMP_BLOCK, dtype=BF16)


def kernel(x, norm_mix_0, nsa_w_in, cmp_k_pos, cmp_k_w1, cmp_k_b1, cmp_k_w2, cmp_k_b2, cmp_v_pos, cmp_v_w1, cmp_v_b1, cmp_v_w2, cmp_v_b2, nsa_w_out, norm_ffn_0, ffn_up_0, ffn_conv_w_0, ffn_conv_b_0, ffn_down_0, norm_mix_1, pool_w, pool_b, pool_scale, norm_ffn_1, ffn_up_1, ffn_conv_w_1, ffn_conv_b_1, ffn_down_1, norm_final):
    B, T, D = x.shape
    assert D == N_HEADS * HEAD_DIM and SEL_BLOCK == 64
    assert T % TM_FFN == 0 and T % TK_SEL == 0 and TK_SEL % TQ == 0 and T >= WINDOW + TQ
    n_cmp_pad = T // CMP_STRIDE
    assert T // SEL_BLOCK <= LANES
    assert HEADS_PER_GROUP < FORCE_BONUS
    row = lambda v: v.reshape(1, -1)

    cos, sin = _rope_tables(jnp.arange(T))
    cos4 = jnp.tile(cos, (1, 4))
    sin4 = jnp.tile(jnp.concatenate([-sin, sin], axis=1), (1, 2))
    q, qsw, ks, kw, kc, vc, vs, vw, gates = _in_proj(
        x, row(norm_mix_0), _inproj_weight(nsa_w_in), cos4, sin4)

    ccos, csin = _rope_tables(jnp.arange(n_cmp_pad) * CMP_STRIDE + (CMP_BLOCK - 1))
    zeros64 = jnp.zeros((n_cmp_pad, HEAD_DIM), F32)
    ccos2 = jnp.concatenate([ccos, ccos, zeros64], axis=1)
    csin2 = jnp.concatenate([csin, csin, zeros64], axis=1)
    pos_rows = lambda p: jnp.broadcast_to(p.reshape(1, -1), (SUBLANES, p.size)).astype(BF16)
    zero_w2 = jnp.zeros_like(cmp_v_w2)
    kparams = (cmp_k_w1.astype(BF16), pos_rows(cmp_k_pos), row(cmp_k_b1),
               jnp.concatenate([cmp_k_w2, _rot_half_cols(cmp_k_w2)], axis=1).astype(BF16),
               row(jnp.concatenate([cmp_k_b2, _rot_half_cols(cmp_k_b2)])))
    vparams = (cmp_v_w1.astype(BF16), pos_rows(cmp_v_pos), row(cmp_v_b1),
               jnp.concatenate([cmp_v_w2, zero_w2], axis=1).astype(BF16),
               jnp.concatenate([zero_w2, cmp_v_w2], axis=1).astype(BF16),
               row(jnp.concatenate([cmp_v_b2, cmp_v_b2])))
    kco, vco = _compress(kc, vc, kparams, vparams, ccos2, csin2)

    o = _nsa_attn(q, qsw, ks, kw, vs, vw, kco, vco, gates, _overlap_matrix(n_cmp_pad))

    x2 = _mixer_ffn(x.reshape(B * T, D), T, "out_proj",
                    (o.reshape(B * T, D), nsa_w_out.astype(BF16)),
                    row(norm_ffn_0), ffn_up_0.astype(BF16), ffn_conv_w_0, row(ffn_conv_b_0),
                    ffn_down_0.astype(BF16), row(norm_final), False)
    x2 = _mixer_ffn(x2, T, "pool",
                    (row(norm_mix_1), pool_w.astype(BF16), row(pool_b.reshape(-1)), row(pool_scale)),
                    row(norm_ffn_1), ffn_up_1.astype(BF16), ffn_conv_w_1, row(ffn_conv_b_1),
                    ffn_down_1.astype(BF16), row(norm_final), True)
    return x2.reshape(B, T, D)
```

```python
import functools

import numpy as np
import jax
import jax.numpy as jnp
from jax import lax
from jax.experimental import pallas as pl
from jax.experimental.pallas import tpu as pltpu

F32 = jnp.float32
BF16 = jnp.bfloat16

N_HEADS = 16
HEAD_DIM = 64
N_GROUPS = 2
HEADS_PER_GROUP = N_HEADS // N_GROUPS
N_BRANCH = 3
CMP_STRIDE = 16
CMP_BLOCK = 32
SEL_BLOCK = 64
SEL_TOPK = 16
WINDOW = 512
ROPE_THETA = 10000.0
POOL_WINDOWS = (2, 4, 8, 16)
CONV_WIDTH = 3
NORM_EPS = 1e-6
NEG_INF = -1e30
FORCE_BONUS = 1e4
N_FORCED = 3
LOG2_E = 1.4426950408889634

LANES = 128
SUBLANES = 8
VMEM_LIMIT = 56 * 1024 * 1024

TM_PROJ = 512
TQ = 256
TK_SEL = 512
CMP_VARIANTS = 2
SEL_ROW_CHUNK = 1024
TM_FFN = 512
TF_FFN = 256
FFN_ROW_BLOCK = 128
HALO_FFN = 8
HALO_ATTN_BF16 = 16
HALO_POOL = 32


def _rmsnorm(x, g):
    return x * lax.rsqrt(jnp.mean(x * x, axis=-1, keepdims=True) + NORM_EPS) * g


def _nt_dot(a, b):
    return lax.dot_general(a, b, (((1,), (1,)), ((), ())), preferred_element_type=F32)


def _dot(a, b):
    return jnp.dot(a, b, preferred_element_type=F32)


def _inproj_body(x_ref, g_ref, w_ref, cos_ref, sin_ref,
                 q_ref, qsw_ref, ks_ref, kw_ref, kc_ref, vc_ref, vs_ref, vw_ref, gate_ref, *, tm):
    i = pl.program_id(1)
    h = _rmsnorm(x_ref[...], g_ref[...]).astype(BF16)
    res = _dot(h, w_ref[...])
    cos = cos_ref[...]
    sin = sin_ref[...]
    lane = lax.broadcasted_iota(jnp.int32, (tm, LANES), 1)
    first_half = (lane & (HEAD_DIM - 1)) < (HEAD_DIM // 2)

    def rope(xc):
        sw = jnp.where(first_half, pltpu.roll(xc, LANES - HEAD_DIM // 2, 1),
                       pltpu.roll(xc, HEAD_DIM // 2, 1))
        return xc * cos + sw * sin

    scale = HEAD_DIM ** -0.5 * LOG2_E
    n_pairs = N_HEADS // 2
    for p in range(n_pairs):
        qc = rope(res[:, p * LANES:(p + 1) * LANES]) * scale
        q_ref[:, p * LANES:(p + 1) * LANES] = qc.astype(BF16)
        qsw_ref[:, p * LANES:(p + 1) * LANES] = pltpu.roll(qc, HEAD_DIM, 1).astype(BF16)
    off = n_pairs * LANES
    row_t = i * tm + lax.broadcasted_iota(jnp.int32, (tm, LANES), 0)
    onehot = jnp.where(lane == (row_t >> 6), 1.0, 0.0).astype(BF16)
    for g in range(N_GROUPS):
        ks_ref[g, :, 0:LANES] = onehot
        ks_ref[g, :, LANES:2 * LANES] = rope(res[:, off + g * LANES: off + (g + 1) * LANES]).astype(BF16)
    off += N_GROUPS * LANES
    for g in range(N_GROUPS):
        kw_ref[g] = rope(res[:, off + g * LANES: off + (g + 1) * LANES]).astype(BF16)
    off += N_GROUPS * LANES
    for g in range(N_GROUPS):
        kc_ref[g] = res[:, off + g * HEAD_DIM:off + (g + 1) * HEAD_DIM]
        vc_ref[g] = res[:, off + LANES + g * HEAD_DIM:off + LANES + (g + 1) * HEAD_DIM]
    ones = jnp.ones((tm, LANES), BF16)
    vs_ref[:, 0:LANES] = res[:, off + 2 * LANES:off + 3 * LANES].astype(BF16)
    vs_ref[:, LANES:2 * LANES] = ones
    vw_ref[:, 0:LANES] = res[:, off + 3 * LANES:off + 4 * LANES].astype(BF16)
    vw_ref[:, LANES:2 * LANES] = ones
    off += 4 * LANES
    for g in range(N_GROUPS):
        z = res[:, off + g * LANES: off + (g + 1) * LANES]
        gate_ref[g] = 1.0 / (1.0 + jnp.exp(-z))


def _in_proj(x, g, wp, cos4, sin4):
    B, T, D = x.shape
    tm = TM_PROJ
    ncols = wp.shape[1]
    grid = (B, T // tm)
    tok = lambda last: pl.BlockSpec((None, tm, last), lambda b, i: (b, i, 0))
    grp = lambda last: pl.BlockSpec((None, N_GROUPS, tm, last), lambda b, i: (b, 0, i, 0))
    out_shape = (
        jax.ShapeDtypeStruct((B, T, N_HEADS * HEAD_DIM), BF16),
        jax.ShapeDtypeStruct((B, T, N_HEADS * HEAD_DIM), BF16),
        jax.ShapeDtypeStruct((B, N_GROUPS, T, 2 * LANES), BF16),
        jax.ShapeDtypeStruct((B, N_GROUPS, T, LANES), BF16),
        jax.ShapeDtypeStruct((B, N_GROUPS, T, HEAD_DIM), F32),
        jax.ShapeDtypeStruct((B, N_GROUPS, T, HEAD_DIM), F32),
        jax.ShapeDtypeStruct((B, T, 2 * LANES), BF16),
        jax.ShapeDtypeStruct((B, T, 2 * LANES), BF16),
        jax.ShapeDtypeStruct((B, N_GROUPS, T, LANES), F32),
    )
    return pl.pallas_call(
        functools.partial(_inproj_body, tm=tm),
        grid=grid,
        in_specs=[tok(D),
                  pl.BlockSpec((1, D), lambda b, i: (0, 0)),
                  pl.BlockSpec((D, ncols), lambda b, i: (0, 0)),
                  pl.BlockSpec((tm, LANES), lambda b, i: (i, 0)),
                  pl.BlockSpec((tm, LANES), lambda b, i: (i, 0))],
        out_specs=(tok(N_HEADS * HEAD_DIM), tok(N_HEADS * HEAD_DIM), grp(2 * LANES), grp(LANES),
                   grp(HEAD_DIM), grp(HEAD_DIM), tok(2 * LANES), tok(2 * LANES), grp(LANES)),
        out_shape=out_shape,
        compiler_params=pltpu.CompilerParams(dimension_semantics=("parallel", "parallel"),
                                             vmem_limit_bytes=VMEM_LIMIT),
        name="in_proj",
    )(x, g, wp, cos4, sin4)


def _gelu_tanh(x):
    return 0.5 * x * (1.0 + jnp.tanh(np.sqrt(2.0 / np.pi).astype(np.float32) * (x + 0.044715 * (x * x * x))))


def _compress_body(kc_ref, vc_ref,
                   kw1_ref, kpos_ref, kb1_ref, kw2_ref, kb2_ref,
                   vw1_ref, vpos_ref, vb1_ref, vw2a_ref, vw2b_ref, vb2_ref,
                   cos_ref, sin_ref, kco_ref, vco_ref, *, n_rows, half):
    row = lax.broadcasted_iota(jnp.int32, (n_rows, LANES), 0)
    keep = row < (n_rows - 1)

    def hidden_act(c_ref, w1_ref, pos_ref, b1_ref):
        posb = _dot(pos_ref[...], w1_ref[...])[0:1, :] + b1_ref[...]
        outs = []
        for g in range(N_GROUPS):
            a = jnp.zeros((n_rows, w1_ref.shape[1]), F32)
            b = jnp.zeros((n_rows, w1_ref.shape[1]), F32)
            for t in range(CMP_STRIDE):
                c = c_ref[g, pl.ds(t, n_rows, stride=CMP_STRIDE), :].astype(BF16)
                a = a + _dot(c, w1_ref[t * HEAD_DIM:(t + 1) * HEAD_DIM, :])
                b = b + _dot(c, w1_ref[half + t * HEAD_DIM:half + (t + 1) * HEAD_DIM, :])
            pre = a + pltpu.roll(b, n_rows - 1, 0) + posb
            outs.append(_gelu_tanh(pre).astype(BF16))
        return outs

    hk = hidden_act(kc_ref, kw1_ref, kpos_ref, kb1_ref)
    cos = cos_ref[...]
    sin = sin_ref[...]
    for g in range(N_GROUPS):
        r = _dot(hk[g], kw2_ref[...]) + kb2_ref[...]
        kco_ref[g] = jnp.where(keep, r * cos + pltpu.roll(r, HEAD_DIM, 1) * sin, 0.0).astype(BF16)
    hv = hidden_act(vc_ref, vw1_ref, vpos_ref, vb1_ref)
    v = _dot(hv[0], vw2a_ref[...]) + _dot(hv[1], vw2b_ref[...]) + vb2_ref[...]
    vco_ref[...] = jnp.where(keep, v, 0.0).astype(BF16)


def _compress(kc2, vc2, kparams, vparams, ccos, csin):
    B, _, T, dh = kc2.shape
    n_rows = T // CMP_STRIDE
    full = lambda a: pl.BlockSpec(a.shape, lambda b: (0,) * a.ndim)
    blk = pl.BlockSpec((None, N_GROUPS, T, dh), lambda b: (b, 0, 0, 0), pipeline_mode=pl.Buffered(1))
    weights = list(kparams) + list(vparams) + [ccos, csin]
    return pl.pallas_call(
        functools.partial(_compress_body, n_rows=n_rows, half=CMP_STRIDE * dh),
        grid=(B,),
        in_specs=[blk, blk] + [full(a) for a in weights],
        out_specs=(pl.BlockSpec((None, N_GROUPS, n_rows, LANES), lambda b: (b, 0, 0, 0)),
                   pl.BlockSpec((None, n_rows, LANES), lambda b: (b, 0, 0))),
        out_shape=(jax.ShapeDtypeStruct((B, N_GROUPS, n_rows, LANES), BF16),
                   jax.ShapeDtypeStruct((B, n_rows, LANES), BF16)),
        compiler_params=pltpu.CompilerParams(dimension_semantics=("parallel",),
                                             vmem_limit_bytes=VMEM_LIMIT),
        name="compress",
    )(kc2, vc2, *weights)


def _softmax_rows(s):
    m = jnp.max(s, axis=1, keepdims=True)
    e = jnp.exp2(s - m)
    return e, jnp.sum(e, axis=1, keepdims=True)


def _attn_body(q_ref, qsw_ref, ks_ref, kw_ref, vs_ref, vw_ref, kc_ref, vc_ref, gate_ref, ov_ref,
               o_ref, qaug_ref, s_ref, m_ref, l_ref, acc_ref, oacc_ref, *, tq, tk, n_cmp_pad):
    hg = HEADS_PER_GROUP
    g = pl.program_id(1)
    i = pl.program_id(2)
    s0 = i * tq
    t_col = s0 + lax.broadcasted_iota(jnp.int32, (tq, 1), 0)

    rc = SEL_ROW_CHUNK
    n_chunks = hg * tq // rc
    reps = rc // tq

    def qpair(h):
        src = q_ref if h % 2 == 0 else qsw_ref
        return src[:, (h // 2) * LANES:(h // 2 + 1) * LANES]

    gates = gate_ref[...]

    def gate(h, branch):
        c = N_BRANCH * h + branch
        return gates[:, c:c + 1]

    n_lc = tk // LANES

    def scores(kt, rows):
        start = pl.multiple_of(kt * tk, tk)
        return _nt_dot(qaug_ref[rows, :], ks_ref[pl.ds(start, tk), :])

    def softmax_pv(kt, causal):
        start = pl.multiple_of(kt * tk, tk)
        if causal:
            kpos = start + lax.broadcasted_iota(jnp.int32, (1, tk), 1)
            bias_d = jnp.where(kpos <= t_col, 0.0, NEG_INF)
        for p in range(n_chunks):
            rows = slice(p * rc, (p + 1) * rc)
            s = s_ref[rows, :]
            if causal:
                s = jnp.concatenate([s[r * tq:(r + 1) * tq] + bias_d for r in range(reps)], axis=0)
            chunks = [s[:, c * LANES:(c + 1) * LANES] for c in range(n_lc)]
            mx = functools.reduce(jnp.maximum, chunks)
            m_old = m_ref[rows, :]
            m_new = jnp.maximum(m_old, jnp.max(mx, axis=1, keepdims=True))
            alpha = jnp.exp2(m_old - m_new)
            p_bf = jnp.concatenate([jnp.exp2(c - m_new) for c in chunks], axis=1).astype(BF16)
            pv = _dot(p_bf, vs_ref[pl.ds(start, tk), :])
            acc_ref[rows, :] = alpha * acc_ref[rows, :] + pv[:, :LANES]
            l_ref[rows, :] = alpha * l_ref[rows, :] + pv[:, LANES:]
            m_ref[rows, :] = m_new
            if not causal:
                s_ref[rows, :] = scores(kt + 1, rows)

    def pre_loop(nk):
        j_row = lax.broadcasted_iota(jnp.int32, (1, nk), 1)
        bias_c = jnp.where(j_row * CMP_STRIDE + (CMP_BLOCK - 1) <= t_col, 0.0, NEG_INF)
        vis = jnp.where(t_col >= CMP_BLOCK - 1, 1.0, 0.0)
        q_all = jnp.concatenate([qpair(h) for h in range(hg)], axis=0)
        sc = _nt_dot(q_all, kc_ref[0:nk, :]).reshape(hg, tq, nk) + bias_c[None]
        e = jnp.exp2(sc - jnp.max(sc, axis=2, keepdims=True))
        r_c = vis[None] / jnp.sum(e, axis=2, keepdims=True)
        rhs_c = jnp.concatenate([vc_ref[0:nk, :], ov_ref[0:nk, :]], axis=1)
        pvc = _dot(e.reshape(hg * tq, nk).astype(BF16), rhs_c)
        imp = jnp.zeros((tq, LANES), F32)
        for h in range(hg):
            rows = slice(h * tq, (h + 1) * tq)
            oacc_ref[rows, :] = (gate(h, 0) * r_c[h]) * pvc[rows, :LANES]
            imp = imp + r_c[h] * pvc[rows, LANES:]

        n_win = WINDOW + tq
        kstart = pl.multiple_of(jnp.maximum(s0 - WINDOW, 0), tq)
        kpos_w = kstart + lax.broadcasted_iota(jnp.int32, (1, n_win), 1)
        bias_w = jnp.where((kpos_w <= t_col) & (kpos_w > t_col - WINDOW), 0.0, NEG_INF)
        sw = _nt_dot(q_all, kw_ref[pl.ds(kstart, n_win), :]).reshape(hg, tq, n_win) + bias_w[None]
        ew = jnp.exp2(sw - jnp.max(sw, axis=2, keepdims=True))
        ow = _dot(ew.reshape(hg * tq, n_win).astype(BF16), vw_ref[pl.ds(kstart, n_win), :])
        for h in range(hg):
            rows = slice(h * tq, (h + 1) * tq)
            oacc_ref[rows, :] += (gate(h, 2) / ow[rows, LANES:]) * ow[rows, :LANES]

        blk = lax.broadcasted_iota(jnp.int32, (tq, LANES), 1)
        cur = t_col >> 6
        forced = (blk == 0) | (blk == cur) | (blk == cur - 1)
        valid = blk <= cur
        score = jnp.where(valid & jnp.logical_not(forced), imp, NEG_INF)
        st = score.T
        blk_t = lax.broadcasted_iota(jnp.int32, (LANES, tq), 0).astype(F32)
        sel_t = jnp.zeros((LANES, tq), F32)
        for _ in range(SEL_TOPK - N_FORCED):
            mx = jnp.max(st, axis=0, keepdims=True)
            idx = jnp.min(jnp.where(st == mx, blk_t, float(LANES)), axis=0, keepdims=True)
            hit = blk_t == idx
            sel_t = jnp.where(hit, 1.0, sel_t)
            st = jnp.where(hit, -jnp.inf, st)
        sel = (sel_t.T > 0.5) | forced
        bias_s = jnp.where(sel & valid, 0.0, NEG_INF).astype(BF16)

        for h in range(hg):
            qaug_ref[h * tq:(h + 1) * tq, 0:LANES] = bias_s
            qaug_ref[h * tq:(h + 1) * tq, LANES:2 * LANES] = qpair(h)
        m_ref[...] = jnp.full(m_ref.shape, NEG_INF, F32)
        l_ref[...] = jnp.zeros(l_ref.shape, F32)
        acc_ref[...] = jnp.zeros(acc_ref.shape, F32)
        s_ref[...] = scores(0, slice(None))

    n_vis = (s0 + tq) // CMP_STRIDE - 1
    step = n_cmp_pad // CMP_VARIANTS
    for v in range(CMP_VARIANTS):
        lo, hi = v * step, (v + 1) * step

        @pl.when((n_vis > lo) & (n_vis <= hi) if v else n_vis <= hi)
        def _():
            pre_loop(hi)

    k_diag = s0 // tk

    def tile_pair(j, carry):
        softmax_pv(2 * j, False)
        softmax_pv(2 * j + 1, False)
        return carry

    lax.fori_loop(0, k_diag // 2, tile_pair, 0)

    @pl.when(k_diag % 2 == 1)
    def _():
        softmax_pv(k_diag - 1, False)

    softmax_pv(k_diag, True)

    outs = []
    for h in range(hg):
        rows = slice(h * tq, (h + 1) * tq)
        outs.append(oacc_ref[rows, :] + (gate(h, 1) / l_ref[rows, :]) * acc_ref[rows, :])
    lane = lax.broadcasted_iota(jnp.int32, (tq, LANES), 1)
    low = lane < HEAD_DIM
    is_g0 = g == 0
    for p in range(hg // 2):
        a = outs[2 * p]
        b = outs[2 * p + 1]
        xa = jnp.where(is_g0, a, pltpu.roll(a, HEAD_DIM, 1))
        xb = jnp.where(is_g0, pltpu.roll(b, HEAD_DIM, 1), b)
        o_ref[:, p * LANES:(p + 1) * LANES] = jnp.where(low, xa, xb).astype(BF16)


def _nsa_attn(q, qsw, ks, kw, vs, vw, kco, vco, gates, overlap):
    B, T, _ = q.shape
    tq, tk = TQ, TK_SEL
    n_cmp_pad = kco.shape[2]
    gw = HEADS_PER_GROUP * HEAD_DIM
    rows = HEADS_PER_GROUP * tq
    grid = (B, N_GROUPS, T // tq)
    once = pl.Buffered(1)
    return pl.pallas_call(
        functools.partial(_attn_body, tq=tq, tk=tk, n_cmp_pad=n_cmp_pad),
        grid=grid,
        in_specs=[
            pl.BlockSpec((None, tq, gw), lambda b, g, i: (b, i, g)),
            pl.BlockSpec((None, tq, gw), lambda b, g, i: (b, i, g)),
            pl.BlockSpec((None, None, T, 2 * LANES), lambda b, g, i: (b, g, 0, 0), pipeline_mode=once),
            pl.BlockSpec((None, None, T, LANES), lambda b, g, i: (b, g, 0, 0), pipeline_mode=once),
            pl.BlockSpec((None, T, 2 * LANES), lambda b, g, i: (b, 0, 0), pipeline_mode=once),
            pl.BlockSpec((None, T, 2 * LANES), lambda b, g, i: (b, 0, 0), pipeline_mode=once),
            pl.BlockSpec((None, None, n_cmp_pad, LANES), lambda b, g, i: (b, g, 0, 0)),
            pl.BlockSpec((None, n_cmp_pad, LANES), lambda b, g, i: (b, 0, 0)),
            pl.BlockSpec((None, None, tq, LANES), lambda b, g, i: (b, g, i, 0)),
            pl.BlockSpec(overlap.shape, lambda b, g, i: (0, 0)),
        ],
        out_specs=pl.BlockSpec((None, tq, gw), lambda b, g, i: (b, i, g)),
        out_shape=jax.ShapeDtypeStruct((B, T, N_HEADS * HEAD_DIM), BF16),
        scratch_shapes=[pltpu.VMEM((rows, 2 * LANES), BF16),
                        pltpu.VMEM((rows, tk), F32),
                        pltpu.VMEM((rows, LANES), F32),
                        pltpu.VMEM((rows, LANES), F32),
                        pltpu.VMEM((rows, LANES), F32),
                        pltpu.VMEM((rows, LANES), F32)],
        compiler_params=pltpu.CompilerParams(
            dimension_semantics=("parallel", "parallel", "arbitrary"),
            vmem_limit_bytes=VMEM_LIMIT),
        name="nsa_attn",
    )(q, qsw, ks, kw, vs, vw, kco, vco, gates, overlap)


def _mixer_out_proj(x_ref, xh_ref, a_ref, ah_ref, wo_ref, xin_ref, *, tm):
    halo = HALO_FFN
    lead = ah_ref.shape[0]
    a_ext = jnp.concatenate([ah_ref[...], a_ref[...]], axis=0)
    y = _dot(a_ext, wo_ref[...])
    xin_ref[0:halo, :] = xh_ref[...] + y[lead - halo:lead]
    xin_ref[halo:halo + tm, :] = x_ref[...] + y[lead:lead + tm]


def _mixer_pool(x_ref, xh_ref, gm_ref, pw_ref, pb_ref, ps_ref, xin_ref, *, tm, seq_tile):
    halo = HALO_FFN
    lead = xh_ref.shape[0]
    n_ext = lead + tm
    xe = jnp.concatenate([xh_ref[...], x_ref[...]], axis=0)
    row = lax.broadcasted_iota(jnp.int32, (n_ext, 1), 0)
    before_seq = (seq_tile == 0) & (row < lead)
    hm = jnp.where(before_seq, 0.0, _rmsnorm(xe, gm_ref[...]))
    t_seq = seq_tile * tm + row - lead
    gdim = xe.shape[1] // len(POOL_WINDOWS)
    ys = []
    for gi, w in enumerate(POOL_WINDOWS):
        cols = slice(gi * gdim, (gi + 1) * gdim)
        hg = hm[:, cols]
        s = hg
        shift = 1
        while shift < w:
            s = s + pltpu.roll(s, shift, 0)
            shift *= 2
        cnt = jnp.clip(t_seq + 1, 1, w).astype(F32)
        ys.append(_dot((s / cnt - hg).astype(BF16), pw_ref[gi]))
    y = (jnp.concatenate(ys, axis=1) + pb_ref[...]) * ps_ref[...]
    xin_ref[...] = (xe + y)[lead - halo:n_ext]


def _ffn_body(*refs, tm, tf, dff, tiles_per_seq, final_norm, mixer):
    n_mix = {"out_proj": 5, "pool": 6}[mixer]
    mix_refs, refs = refs[:n_mix], refs[n_mix:]
    (g_ref, wup_ref, cw_ref, cb_ref, wd_ref, gf_ref, o_ref,
     h_ref, xin_ref, a_ref, ug0_ref, uv0_ref, ug1_ref, uv1_ref) = refs
    u_refs = ((ug0_ref, uv0_ref), (ug1_ref, uv1_ref))
    i = pl.program_id(0)
    halo = HALO_FFN
    n = halo + tm
    seq_tile = i % tiles_per_seq
    if mixer == "out_proj":
        _mixer_out_proj(*mix_refs, xin_ref, tm=tm)
    else:
        _mixer_pool(*mix_refs, xin_ref, tm=tm, seq_tile=seq_tile)
    hn = _rmsnorm(xin_ref[...], g_ref[...])
    row = lax.broadcasted_iota(jnp.int32, (n, 1), 0)
    no_context = (seq_tile == 0) & (row < halo)
    h_ref[...] = jnp.where(no_context, 0.0, hn).astype(BF16)

    nj = dff // tf
    rb = FFN_ROW_BLOCK
    n_rb = tm // rb

    def up_proj(j, half, b):
        col = half * dff + j * tf
        r0 = 0 if b == 0 else halo + b * rb
        r1 = halo + (b + 1) * rb
        u_refs[j % 2][half][r0:r1, :] = _dot(h_ref[r0:r1, :], wup_ref[:, col:col + tf])

    def conv(j, half, b):
        col = half * dff + j * tf
        u = u_refs[j % 2][half][b * rb:halo + (b + 1) * rb, :]
        u1 = pltpu.roll(u, 1, 0)
        u2 = pltpu.roll(u, 2, 0)
        cw = cw_ref[:, col:col + tf]
        c = cw[0:1, :] * u2 + cw[1:2, :] * u1 + cw[2:3, :] * u + cb_ref[:, col:col + tf]
        return c[halo:halo + rb]

    for half in range(2):
        for b in range(n_rb):
            up_proj(0, half, b)
    for j in range(nj):
        for b in range(n_rb):
            rows = slice(b * rb, (b + 1) * rb)
            if j + 1 < nj:
                up_proj(j + 1, 0, b)
            cg = conv(j, 0, b)
            if j + 1 < nj:
                up_proj(j + 1, 1, b)
            cv = conv(j, 1, b)
            a = (cg * (1.0 / (1.0 + jnp.exp(-cg)))) * cv
            a_ref[rows, j * tf:(j + 1) * tf] = a.astype(BF16)

    y = xin_ref[halo:n, :] + _dot(a_ref[...], wd_ref[...])
    if final_norm:
        y = _rmsnorm(y, gf_ref[...])
    o_ref[...] = y


def _mixer_ffn(x2, seq_len, mixer, mixer_args, g, w_up, conv_w, conv_b, w_down, g_final, final_norm):
    N, D = x2.shape
    dff = w_down.shape[0]
    tm, tf = TM_FFN, TF_FFN
    resident = lambda a: pl.BlockSpec(a.shape, lambda i: (0,) * a.ndim, pipeline_mode=pl.Buffered(1))
    tile = pl.BlockSpec((tm, D), lambda i: (i, 0))

    def left_context(rows):
        return pl.BlockSpec((rows, D), lambda i: (jnp.maximum(i * (tm // rows) - 1, 0), 0))

    if mixer == "out_proj":
        attn, w_out = mixer_args
        mix_in = [x2, x2, attn, attn, w_out]
        mix_specs = [tile, left_context(HALO_FFN), tile, left_context(HALO_ATTN_BF16), resident(w_out)]
    else:
        mix_in = [x2, x2] + list(mixer_args)
        mix_specs = [tile, left_context(HALO_POOL)] + [resident(a) for a in mixer_args]
    weights = [g, w_up, conv_w, conv_b, w_down, g_final]
    return pl.pallas_call(
        functools.partial(_ffn_body, tm=tm, tf=tf, dff=dff, tiles_per_seq=seq_len // tm,
                          final_norm=final_norm, mixer=mixer),
        grid=(N // tm,),
        in_specs=mix_specs + [resident(a) for a in weights],
        out_specs=tile,
        out_shape=jax.ShapeDtypeStruct((N, D), F32),
        scratch_shapes=[pltpu.VMEM((HALO_FFN + tm, D), BF16),
                        pltpu.VMEM((HALO_FFN + tm, D), F32),
                        pltpu.VMEM((tm, dff), BF16),
                        ] + [pltpu.VMEM((HALO_FFN + tm, tf), F32)] * 4,
        compiler_params=pltpu.CompilerParams(dimension_semantics=("parallel",),
                                             vmem_limit_bytes=VMEM_LIMIT),
        name="mixer_ffn_" + mixer,
    )(*mix_in, *weights)


def _pad_cols(w, width):
    return jnp.pad(w, ((0, 0), (0, width - w.shape[1])))


def _inproj_weight(w_in):
    D = w_in.shape[0]
    kv = N_GROUPS * HEAD_DIM
    sizes = [N_HEADS * HEAD_DIM] + [kv] * 6 + [N_BRANCH * N_HEADS]
    offs = np.concatenate([[0], np.cumsum(sizes)])
    q, k_c, v_c, k_s, v_s, k_w, v_w, gt = [w_in[:, offs[n]:offs[n + 1]] for n in range(8)]
    per_group = lambda w, n: [_pad_cols(w[:, g * n:(g + 1) * n], LANES) for g in range(N_GROUPS)]
    cols = ([q] + per_group(k_s, HEAD_DIM) + per_group(k_w, HEAD_DIM) + [k_c, v_c, v_s, v_w]
            + per_group(gt, N_BRANCH * HEADS_PER_GROUP))
    return jnp.concatenate(cols, axis=1).astype(BF16)


def _rope_tables(pos):
    inv = 1.0 / (ROPE_THETA ** (jnp.arange(0, HEAD_DIM, 2, dtype=F32) / HEAD_DIM))
    ang = pos.astype(F32)[:, None] * inv[None, :]
    return jnp.cos(ang), jnp.sin(ang)


def _rot_half_cols(w):
    half = HEAD_DIM // 2
    return jnp.concatenate([-w[..., half:], w[..., :half]], axis=-1)


def _overlap_matrix(n_cmp_pad):
    j = np.arange(n_cmp_pad)[:, None]
    s = np.arange(LANES)[None, :]
    lo = np.maximum(j * CMP_STRIDE, s * SEL_BLOCK)
    hi = np.minimum(j * CMP_STRIDE + CMP_BLOCK, (s + 1) * SEL_BLOCK)
    return jnp.asarray(np.clip(hi - lo, 0, None) / CMP_BLOCK, dtype=BF16)


def kernel(x, norm_mix_0, nsa_w_in, cmp_k_pos, cmp_k_w1, cmp_k_b1, cmp_k_w2, cmp_k_b2, cmp_v_pos, cmp_v_w1, cmp_v_b1, cmp_v_w2, cmp_v_b2, nsa_w_out, norm_ffn_0, ffn_up_0, ffn_conv_w_0, ffn_conv_b_0, ffn_down_0, norm_mix_1, pool_w, pool_b, pool_scale, norm_ffn_1, ffn_up_1, ffn_conv_w_1, ffn_conv_b_1, ffn_down_1, norm_final):
    B, T, D = x.shape
    assert D == N_HEADS * HEAD_DIM and SEL_BLOCK == 64
    assert T % TM_FFN == 0 and T % TK_SEL == 0 and TK_SEL % TQ == 0 and T >= WINDOW + TQ
    n_cmp_pad = T // CMP_STRIDE
    assert T // SEL_BLOCK <= LANES
    assert HEADS_PER_GROUP < FORCE_BONUS
    row = lambda v: v.reshape(1, -1)

    cos, sin = _rope_tables(jnp.arange(T))
    cos4 = jnp.tile(cos, (1, 4))
    sin4 = jnp.tile(jnp.concatenate([-sin, sin], axis=1), (1, 2))
    q, qsw, ks, kw, kc, vc, vs, vw, gates = _in_proj(
        x, row(norm_mix_0), _inproj_weight(nsa_w_in), cos4, sin4)

    ccos, csin = _rope_tables(jnp.arange(n_cmp_pad) * CMP_STRIDE + (CMP_BLOCK - 1))
    zeros64 = jnp.zeros((n_cmp_pad, HEAD_DIM), F32)
    ccos2 = jnp.concatenate([ccos, ccos, zeros64], axis=1)
    csin2 = jnp.concatenate([csin, csin, zeros64], axis=1)
    pos_rows = lambda p: jnp.broadcast_to(p.reshape(1, -1), (SUBLANES, p.size)).astype(BF16)
    zero_w2 = jnp.zeros_like(cmp_v_w2)
    kparams = (cmp_k_w1.astype(BF16), pos_rows(cmp_k_pos), row(cmp_k_b1),
               jnp.concatenate([cmp_k_w2, _rot_half_cols(cmp_k_w2)], axis=1).astype(BF16),
               row(jnp.concatenate([cmp_k_b2, _rot_half_cols(cmp_k_b2)])))
    vparams = (cmp_v_w1.astype(BF16), pos_rows(cmp_v_pos), row(cmp_v_b1),
               jnp.concatenate([cmp_v_w2, zero_w2], axis=1).astype(BF16),
               jnp.concatenate([zero_w2, cmp_v_w2], axis=1).astype(BF16),
               row(jnp.concatenate([cmp_v_b2, cmp_v_b2])))
    kco, vco = _compress(kc, vc, kparams, vparams, ccos2, csin2)

    o = _nsa_attn(q, qsw, ks, kw, vs, vw, kco, vco, gates, _overlap_matrix(n_cmp_pad))

    x2 = _mixer_ffn(x.reshape(B * T, D), T, "out_proj",
                    (o.reshape(B * T, D), nsa_w_out.astype(BF16)),
                    row(norm_ffn_0), ffn_up_0.astype(BF16), ffn_conv_w_0, row(ffn_conv_b_0),
                    ffn_down_0.astype(BF16), row(norm_final), False)
    x2 = _mixer_ffn(x2, T, "pool",
                    (row(norm_mix_1), pool_w.astype(BF16), row(pool_b.reshape(-1)), row(pool_scale)),
                    row(norm_ffn_1), ffn_up_1.astype(BF16), ffn_conv_w_1, row(ffn_conv_b_1),
                    ffn_down_1.astype(BF16), row(norm_final), True)
    return x2.reshape(B, T, D)
```

```python
import functools

import numpy as np
import jax
import jax.numpy as jnp
from jax import lax
from jax.experimental import pallas as pl
from jax.experimental.pallas import tpu as pltpu

F32 = jnp.float32
BF16 = jnp.bfloat16

N_HEADS = 16
HEAD_DIM = 64
N_GROUPS = 2
HEADS_PER_GROUP = N_HEADS // N_GROUPS
N_BRANCH = 3
CMP_STRIDE = 16
CMP_BLOCK = 32
SEL_BLOCK = 64
SEL_TOPK = 16
WINDOW = 512
ROPE_THETA = 10000.0
POOL_WINDOWS = (2, 4, 8, 16)
CONV_WIDTH = 3
NORM_EPS = 1e-6
NEG_INF = -1e30
FORCE_BONUS = 1e4
N_FORCED = 3
LOG2_E = 1.4426950408889634

LANES = 128
SUBLANES = 8
VMEM_LIMIT = 56 * 1024 * 1024

TM_PROJ = 512
TQ = 256
TK_SEL = 512
CMP_VARIANTS = 2
SEL_ROW_CHUNK = 2048
TM_FFN = 512
TF_FFN = 256
FFN_ROW_BLOCK = 128
HALO_FFN = 8
HALO_ATTN_BF16 = 16
HALO_POOL = 32


def _rmsnorm(x, g):
    return x * lax.rsqrt(jnp.mean(x * x, axis=-1, keepdims=True) + NORM_EPS) * g


def _nt_dot(a, b):
    return lax.dot_general(a, b, (((1,), (1,)), ((), ())), preferred_element_type=F32)


def _dot(a, b):
    return jnp.dot(a, b, preferred_element_type=F32)


def _inproj_body(x_ref, g_ref, w_ref, cos_ref, sin_ref,
                 q_ref, qsw_ref, ks_ref, kw_ref, kc_ref, vc_ref, vs_ref, vw_ref, gate_ref, *, tm):
    i = pl.program_id(1)
    h = _rmsnorm(x_ref[...], g_ref[...]).astype(BF16)
    res = _dot(h, w_ref[...])
    cos = cos_ref[...]
    sin = sin_ref[...]
    lane = lax.broadcasted_iota(jnp.int32, (tm, LANES), 1)
    first_half = (lane & (HEAD_DIM - 1)) < (HEAD_DIM // 2)

    def rope(xc):
        sw = jnp.where(first_half, pltpu.roll(xc, LANES - HEAD_DIM // 2, 1),
                       pltpu.roll(xc, HEAD_DIM // 2, 1))
        return xc * cos + sw * sin

    scale = HEAD_DIM ** -0.5 * LOG2_E
    n_pairs = N_HEADS // 2
    for p in range(n_pairs):
        qc = rope(res[:, p * LANES:(p + 1) * LANES]) * scale
        q_ref[:, p * LANES:(p + 1) * LANES] = qc.astype(BF16)
        qsw_ref[:, p * LANES:(p + 1) * LANES] = pltpu.roll(qc, HEAD_DIM, 1).astype(BF16)
    off = n_pairs * LANES
    row_t = i * tm + lax.broadcasted_iota(jnp.int32, (tm, LANES), 0)
    onehot = jnp.where(lane == (row_t >> 6), 1.0, 0.0).astype(BF16)
    for g in range(N_GROUPS):
        ks_ref[g, :, 0:LANES] = onehot
        ks_ref[g, :, LANES:2 * LANES] = rope(res[:, off + g * LANES: off + (g + 1) * LANES]).astype(BF16)
    off += N_GROUPS * LANES
    for g in range(N_GROUPS):
        kw_ref[g] = rope(res[:, off + g * LANES: off + (g + 1) * LANES]).astype(BF16)
    off += N_GROUPS * LANES
    for g in range(N_GROUPS):
        kc_ref[g] = res[:, off + g * HEAD_DIM:off + (g + 1) * HEAD_DIM]
        vc_ref[g] = res[:, off + LANES + g * HEAD_DIM:off + LANES + (g + 1) * HEAD_DIM]
    ones = jnp.ones((tm, LANES), BF16)
    vs_ref[:, 0:LANES] = res[:, off + 2 * LANES:off + 3 * LANES].astype(BF16)
    vs_ref[:, LANES:2 * LANES] = ones
    vw_ref[:, 0:LANES] = res[:, off + 3 * LANES:off + 4 * LANES].astype(BF16)
    vw_ref[:, LANES:2 * LANES] = ones
    off += 4 * LANES
    for g in range(N_GROUPS):
        z = res[:, off + g * LANES: off + (g + 1) * LANES]
        gate_ref[g] = 1.0 / (1.0 + jnp.exp(-z))


def _in_proj(x, g, wp, cos4, sin4):
    B, T, D = x.shape
    tm = TM_PROJ
    ncols = wp.shape[1]
    grid = (B, T // tm)
    tok = lambda last: pl.BlockSpec((None, tm, last), lambda b, i: (b, i, 0))
    grp = lambda last: pl.BlockSpec((None, N_GROUPS, tm, last), lambda b, i: (b, 0, i, 0))
    out_shape = (
        jax.ShapeDtypeStruct((B, T, N_HEADS * HEAD_DIM), BF16),
        jax.ShapeDtypeStruct((B, T, N_HEADS * HEAD_DIM), BF16),
        jax.ShapeDtypeStruct((B, N_GROUPS, T, 2 * LANES), BF16),
        jax.ShapeDtypeStruct((B, N_GROUPS, T, LANES), BF16),
        jax.ShapeDtypeStruct((B, N_GROUPS, T, HEAD_DIM), F32),
        jax.ShapeDtypeStruct((B, N_GROUPS, T, HEAD_DIM), F32),
        jax.ShapeDtypeStruct((B, T, 2 * LANES), BF16),
        jax.ShapeDtypeStruct((B, T, 2 * LANES), BF16),
        jax.ShapeDtypeStruct((B, N_GROUPS, T, LANES), F32),
    )
    return pl.pallas_call(
        functools.partial(_inproj_body, tm=tm),
        grid=grid,
        in_specs=[tok(D),
                  pl.BlockSpec((1, D), lambda b, i: (0, 0)),
                  pl.BlockSpec((D, ncols), lambda b, i: (0, 0)),
                  pl.BlockSpec((tm, LANES), lambda b, i: (i, 0)),
                  pl.BlockSpec((tm, LANES), lambda b, i: (i, 0))],
        out_specs=(tok(N_HEADS * HEAD_DIM), tok(N_HEADS * HEAD_DIM), grp(2 * LANES), grp(LANES),
                   grp(HEAD_DIM), grp(HEAD_DIM), tok(2 * LANES), tok(2 * LANES), grp(LANES)),
        out_shape=out_shape,
        compiler_params=pltpu.CompilerParams(dimension_semantics=("parallel", "parallel"),
                                             vmem_limit_bytes=VMEM_LIMIT),
        name="in_proj",
    )(x, g, wp, cos4, sin4)


def _gelu_tanh(x):
    return 0.5 * x * (1.0 + jnp.tanh(np.sqrt(2.0 / np.pi).astype(np.float32) * (x + 0.044715 * (x * x * x))))


def _compress_body(kc_ref, vc_ref,
                   kw1_ref, kpos_ref, kb1_ref, kw2_ref, kb2_ref,
                   vw1_ref, vpos_ref, vb1_ref, vw2a_ref, vw2b_ref, vb2_ref,
                   cos_ref, sin_ref, kco_ref, vco_ref, *, n_rows, half):
    row = lax.broadcasted_iota(jnp.int32, (n_rows, LANES), 0)
    keep = row < (n_rows - 1)

    def hidden_act(c_ref, w1_ref, pos_ref, b1_ref):
        posb = _dot(pos_ref[...], w1_ref[...])[0:1, :] + b1_ref[...]
        outs = []
        for g in range(N_GROUPS):
            a = jnp.zeros((n_rows, w1_ref.shape[1]), F32)
            b = jnp.zeros((n_rows, w1_ref.shape[1]), F32)
            for t in range(CMP_STRIDE):
                c = c_ref[g, pl.ds(t, n_rows, stride=CMP_STRIDE), :].astype(BF16)
                a = a + _dot(c, w1_ref[t * HEAD_DIM:(t + 1) * HEAD_DIM, :])
                b = b + _dot(c, w1_ref[half + t * HEAD_DIM:half + (t + 1) * HEAD_DIM, :])
            pre = a + pltpu.roll(b, n_rows - 1, 0) + posb
            outs.append(_gelu_tanh(pre).astype(BF16))
        return outs

    hk = hidden_act(kc_ref, kw1_ref, kpos_ref, kb1_ref)
    cos = cos_ref[...]
    sin = sin_ref[...]
    for g in range(N_GROUPS):
        r = _dot(hk[g], kw2_ref[...]) + kb2_ref[...]
        kco_ref[g] = jnp.where(keep, r * cos + pltpu.roll(r, HEAD_DIM, 1) * sin, 0.0).astype(BF16)
    hv = hidden_act(vc_ref, vw1_ref, vpos_ref, vb1_ref)
    v = _dot(hv[0], vw2a_ref[...]) + _dot(hv[1], vw2b_ref[...]) + vb2_ref[...]
    vco_ref[...] = jnp.where(keep, v, 0.0).astype(BF16)


def _compress(kc2, vc2, kparams, vparams, ccos, csin):
    B, _, T, dh = kc2.shape
    n_rows = T // CMP_STRIDE
    full = lambda a: pl.BlockSpec(a.shape, lambda b: (0,) * a.ndim)
    blk = pl.BlockSpec((None, N_GROUPS, T, dh), lambda b: (b, 0, 0, 0), pipeline_mode=pl.Buffered(1))
    weights = list(kparams) + list(vparams) + [ccos, csin]
    return pl.pallas_call(
        functools.partial(_compress_body, n_rows=n_rows, half=CMP_STRIDE * dh),
        grid=(B,),
        in_specs=[blk, blk] + [full(a) for a in weights],
        out_specs=(pl.BlockSpec((None, N_GROUPS, n_rows, LANES), lambda b: (b, 0, 0, 0)),
                   pl.BlockSpec((None, n_rows, LANES), lambda b: (b, 0, 0))),
        out_shape=(jax.ShapeDtypeStruct((B, N_GROUPS, n_rows, LANES), BF16),
                   jax.ShapeDtypeStruct((B, n_rows, LANES), BF16)),
        compiler_params=pltpu.CompilerParams(dimension_semantics=("parallel",),
                                             vmem_limit_bytes=VMEM_LIMIT),
        name="compress",
    )(kc2, vc2, *weights)


def _softmax_rows(s):
    m = jnp.max(s, axis=1, keepdims=True)
    e = jnp.exp2(s - m)
    return e, jnp.sum(e, axis=1, keepdims=True)


def _attn_body(q_ref, qsw_ref, ks_ref, kw_ref, vs_ref, vw_ref, kc_ref, vc_ref, gate_ref, ov_ref,
               o_ref, qaug_ref, s_ref, m_ref, l_ref, acc_ref, oacc_ref, *, tq, tk, n_cmp_pad):
    hg = HEADS_PER_GROUP
    g = pl.program_id(1)
    i = pl.program_id(2)
    s0 = i * tq
    t_col = s0 + lax.broadcasted_iota(jnp.int32, (tq, 1), 0)

    rc = SEL_ROW_CHUNK
    n_chunks = hg * tq // rc
    reps = rc // tq

    def qpair(h):
        src = q_ref if h % 2 == 0 else qsw_ref
        return src[:, (h // 2) * LANES:(h // 2 + 1) * LANES]

    gates = gate_ref[...]

    def gate(h, branch):
        c = N_BRANCH * h + branch
        return gates[:, c:c + 1]

    n_lc = tk // LANES

    def scores(kt, rows):
        start = pl.multiple_of(kt * tk, tk)
        return _nt_dot(qaug_ref[rows, :], ks_ref[pl.ds(start, tk), :])

    def softmax_pv(kt, causal):
        start = pl.multiple_of(kt * tk, tk)
        if causal:
            kpos = start + lax.broadcasted_iota(jnp.int32, (1, tk), 1)
            bias_d = jnp.where(kpos <= t_col, 0.0, NEG_INF)
        for p in range(n_chunks):
            rows = slice(p * rc, (p + 1) * rc)
            s = s_ref[rows, :]
            if causal:
                s = jnp.concatenate([s[r * tq:(r + 1) * tq] + bias_d for r in range(reps)], axis=0)
            chunks = [s[:, c * LANES:(c + 1) * LANES] for c in range(n_lc)]
            mx = functools.reduce(jnp.maximum, chunks)
            m_old = m_ref[rows, :]
            m_new = jnp.maximum(m_old, jnp.max(mx, axis=1, keepdims=True))
            alpha = jnp.exp2(m_old - m_new)
            p_bf = jnp.concatenate([jnp.exp2(c - m_new) for c in chunks], axis=1).astype(BF16)
            pv = _dot(p_bf, vs_ref[pl.ds(start, tk), :])
            acc_ref[rows, :] = alpha * acc_ref[rows, :] + pv[:, :LANES]
            l_ref[rows, :] = alpha * l_ref[rows, :] + pv[:, LANES:]
            m_ref[rows, :] = m_new
            if not causal:
                s_ref[rows, :] = scores(kt + 1, rows)

    def pre_loop(nk):
        j_row = lax.broadcasted_iota(jnp.int32, (1, nk), 1)
        bias_c = jnp.where(j_row * CMP_STRIDE + (CMP_BLOCK - 1) <= t_col, 0.0, NEG_INF)
        vis = jnp.where(t_col >= CMP_BLOCK - 1, 1.0, 0.0)
        q_all = jnp.concatenate([qpair(h) for h in range(hg)], axis=0)
        sc = _nt_dot(q_all, kc_ref[0:nk, :]).reshape(hg, tq, nk) + bias_c[None]
        e = jnp.exp2(sc - jnp.max(sc, axis=2, keepdims=True))
        r_c = vis[None] / jnp.sum(e, axis=2, keepdims=True)
        rhs_c = jnp.concatenate([vc_ref[0:nk, :], ov_ref[0:nk, :]], axis=1)
        pvc = _dot(e.reshape(hg * tq, nk).astype(BF16), rhs_c)
        imp = jnp.zeros((tq, LANES), F32)
        for h in range(hg):
            rows = slice(h * tq, (h + 1) * tq)
            oacc_ref[rows, :] = (gate(h, 0) * r_c[h]) * pvc[rows, :LANES]
            imp = imp + r_c[h] * pvc[rows, LANES:]

        n_win = WINDOW + tq
        kstart = pl.multiple_of(jnp.maximum(s0 - WINDOW, 0), tq)
        kpos_w = kstart + lax.broadcasted_iota(jnp.int32, (1, n_win), 1)
        bias_w = jnp.where((kpos_w <= t_col) & (kpos_w > t_col - WINDOW), 0.0, NEG_INF)
        sw = _nt_dot(q_all, kw_ref[pl.ds(kstart, n_win), :]).reshape(hg, tq, n_win) + bias_w[None]
        ew = jnp.exp2(sw - jnp.max(sw, axis=2, keepdims=True))
        ow = _dot(ew.reshape(hg * tq, n_win).astype(BF16), vw_ref[pl.ds(kstart, n_win), :])
        for h in range(hg):
            rows = slice(h * tq, (h + 1) * tq)
            oacc_ref[rows, :] += (gate(h, 2) / ow[rows, LANES:]) * ow[rows, :LANES]

        blk = lax.broadcasted_iota(jnp.int32, (tq, LANES), 1)
        cur = t_col >> 6
        forced = (blk == 0) | (blk == cur) | (blk == cur - 1)
        valid = blk <= cur
        score = jnp.where(valid & jnp.logical_not(forced), imp, NEG_INF)
        st = score.T
        blk_t = lax.broadcasted_iota(jnp.int32, (LANES, tq), 0).astype(F32)
        sel_t = jnp.zeros((LANES, tq), F32)
        for _ in range(SEL_TOPK - N_FORCED):
            mx = jnp.max(st, axis=0, keepdims=True)
            idx = jnp.min(jnp.where(st == mx, blk_t, float(LANES)), axis=0, keepdims=True)
            hit = blk_t == idx
            sel_t = jnp.where(hit, 1.0, sel_t)
            st = jnp.where(hit, -jnp.inf, st)
        sel = (sel_t.T > 0.5) | forced
        bias_s = jnp.where(sel & valid, 0.0, NEG_INF).astype(BF16)

        for h in range(hg):
            qaug_ref[h * tq:(h + 1) * tq, 0:LANES] = bias_s
            qaug_ref[h * tq:(h + 1) * tq, LANES:2 * LANES] = qpair(h)
        m_ref[...] = jnp.full(m_ref.shape, NEG_INF, F32)
        l_ref[...] = jnp.zeros(l_ref.shape, F32)
        acc_ref[...] = jnp.zeros(acc_ref.shape, F32)
        s_ref[...] = scores(0, slice(None))

    n_vis = (s0 + tq) // CMP_STRIDE - 1
    step = n_cmp_pad // CMP_VARIANTS
    for v in range(CMP_VARIANTS):
        lo, hi = v * step, (v + 1) * step

        @pl.when((n_vis > lo) & (n_vis <= hi) if v else n_vis <= hi)
        def _():
            pre_loop(hi)

    k_diag = s0 // tk

    def tile_pair(j, carry):
        softmax_pv(2 * j, False)
        softmax_pv(2 * j + 1, False)
        return carry

    lax.fori_loop(0, k_diag // 2, tile_pair, 0)

    @pl.when(k_diag % 2 == 1)
    def _():
        softmax_pv(k_diag - 1, False)

    softmax_pv(k_diag, True)

    outs = []
    for h in range(hg):
        rows = slice(h * tq, (h + 1) * tq)
        outs.append(oacc_ref[rows, :] + (gate(h, 1) / l_ref[rows, :]) * acc_ref[rows, :])
    lane = lax.broadcasted_iota(jnp.int32, (tq, LANES), 1)
    low = lane < HEAD_DIM
    is_g0 = g == 0
    for p in range(hg // 2):
        a = outs[2 * p]
        b = outs[2 * p + 1]
        xa = jnp.where(is_g0, a, pltpu.roll(a, HEAD_DIM, 1))
        xb = jnp.where(is_g0, pltpu.roll(b, HEAD_DIM, 1), b)
        o_ref[:, p * LANES:(p + 1) * LANES] = jnp.where(low, xa, xb).astype(BF16)


def _nsa_attn(q, qsw, ks, kw, vs, vw, kco, vco, gates, overlap):
    B, T, _ = q.shape
    tq, tk = TQ, TK_SEL
    n_cmp_pad = kco.shape[2]
    gw = HEADS_PER_GROUP * HEAD_DIM
    rows = HEADS_PER_GROUP * tq
    grid = (B, N_GROUPS, T // tq)
    once = pl.Buffered(1)
    return pl.pallas_call(
        functools.partial(_attn_body, tq=tq, tk=tk, n_cmp_pad=n_cmp_pad),
        grid=grid,
        in_specs=[
            pl.BlockSpec((None, tq, gw), lambda b, g, i: (b, i, g)),
            pl.BlockSpec((None, tq, gw), lambda b, g, i: (b, i, g)),
            pl.BlockSpec((None, None, T, 2 * LANES), lambda b, g, i: (b, g, 0, 0), pipeline_mode=once),
            pl.BlockSpec((None, None, T, LANES), lambda b, g, i: (b, g, 0, 0), pipeline_mode=once),
            pl.BlockSpec((None, T, 2 * LANES), lambda b, g, i: (b, 0, 0), pipeline_mode=once),
            pl.BlockSpec((None, T, 2 * LANES), lambda b, g, i: (b, 0, 0), pipeline_mode=once),
            pl.BlockSpec((None, None, n_cmp_pad, LANES), lambda b, g, i: (b, g, 0, 0)),
            pl.BlockSpec((None, n_cmp_pad, LANES), lambda b, g, i: (b, 0, 0)),
            pl.BlockSpec((None, None, tq, LANES), lambda b, g, i: (b, g, i, 0)),
            pl.BlockSpec(overlap.shape, lambda b, g, i: (0, 0)),
        ],
        out_specs=pl.BlockSpec((None, tq, gw), lambda b, g, i: (b, i, g)),
        out_shape=jax.ShapeDtypeStruct((B, T, N_HEADS * HEAD_DIM), BF16),
        scratch_shapes=[pltpu.VMEM((rows, 2 * LANES), BF16),
                        pltpu.VMEM((rows, tk), F32),
                        pltpu.VMEM((rows, LANES), F32),
                        pltpu.VMEM((rows, LANES), F32),
                        pltpu.VMEM((rows, LANES), F32),
                        pltpu.VMEM((rows, LANES), F32)],
        compiler_params=pltpu.CompilerParams(
            dimension_semantics=("parallel", "parallel", "arbitrary"),
            vmem_limit_bytes=VMEM_LIMIT),
        name="nsa_attn",
    )(q, qsw, ks, kw, vs, vw, kco, vco, gates, overlap)


def _mixer_out_proj(x_ref, xh_ref, a_ref, ah_ref, wo_ref, xin_ref, *, tm):
    halo = HALO_FFN
    lead = ah_ref.shape[0]
    a_ext = jnp.concatenate([ah_ref[...], a_ref[...]], axis=0)
    y = _dot(a_ext, wo_ref[...])
    xin_ref[0:halo, :] = xh_ref[...] + y[lead - halo:lead]
    xin_ref[halo:halo + tm, :] = x_ref[...] + y[lead:lead + tm]


def _mixer_pool(x_ref, xh_ref, gm_ref, pw_ref, pb_ref, ps_ref, xin_ref, *, tm, seq_tile):
    halo = HALO_FFN
    lead = xh_ref.shape[0]
    n_ext = lead + tm
    xe = jnp.concatenate([xh_ref[...], x_ref[...]], axis=0)
    row = lax.broadcasted_iota(jnp.int32, (n_ext, 1), 0)
    before_seq = (seq_tile == 0) & (row < lead)
    hm = jnp.where(before_seq, 0.0, _rmsnorm(xe, gm_ref[...]))
    t_seq = seq_tile * tm + row - lead
    gdim = xe.shape[1] // len(POOL_WINDOWS)
    ys = []
    for gi, w in enumerate(POOL_WINDOWS):
        cols = slice(gi * gdim, (gi + 1) * gdim)
        hg = hm[:, cols]
        s = hg
        shift = 1
        while shift < w:
            s = s + pltpu.roll(s, shift, 0)
            shift *= 2
        cnt = jnp.clip(t_seq + 1, 1, w).astype(F32)
        ys.append(_dot((s / cnt - hg).astype(BF16), pw_ref[gi]))
    y = (jnp.concatenate(ys, axis=1) + pb_ref[...]) * ps_ref[...]
    xin_ref[...] = (xe + y)[lead - halo:n_ext]


def _ffn_body(*refs, tm, tf, dff, tiles_per_seq, final_norm, mixer):
    n_mix = {"out_proj": 5, "pool": 6}[mixer]
    mix_refs, refs = refs[:n_mix], refs[n_mix:]
    (g_ref, wup_ref, cw_ref, cb_ref, wd_ref, gf_ref, o_ref,
     h_ref, xin_ref, a_ref, ug0_ref, uv0_ref, ug1_ref, uv1_ref) = refs
    u_refs = ((ug0_ref, uv0_ref), (ug1_ref, uv1_ref))
    i = pl.program_id(0)
    halo = HALO_FFN
    n = halo + tm
    seq_tile = i % tiles_per_seq
    if mixer == "out_proj":
        _mixer_out_proj(*mix_refs, xin_ref, tm=tm)
    else:
        _mixer_pool(*mix_refs, xin_ref, tm=tm, seq_tile=seq_tile)
    hn = _rmsnorm(xin_ref[...], g_ref[...])
    row = lax.broadcasted_iota(jnp.int32, (n, 1), 0)
    no_context = (seq_tile == 0) & (row < halo)
    h_ref[...] = jnp.where(no_context, 0.0, hn).astype(BF16)

    nj = dff // tf
    rb = FFN_ROW_BLOCK
    n_rb = tm // rb

    def up_proj(j, half, b):
        col = half * dff + j * tf
        r0 = 0 if b == 0 else halo + b * rb
        r1 = halo + (b + 1) * rb
        u_refs[j % 2][half][r0:r1, :] = _dot(h_ref[r0:r1, :], wup_ref[:, col:col + tf])

    def conv(j, half, b):
        col = half * dff + j * tf
        u = u_refs[j % 2][half][b * rb:halo + (b + 1) * rb, :]
        u1 = pltpu.roll(u, 1, 0)
        u2 = pltpu.roll(u, 2, 0)
        cw = cw_ref[:, col:col + tf]
        c = cw[0:1, :] * u2 + cw[1:2, :] * u1 + cw[2:3, :] * u + cb_ref[:, col:col + tf]
        return c[halo:halo + rb]

    for half in range(2):
        for b in range(n_rb):
            up_proj(0, half, b)
    for j in range(nj):
        for b in range(n_rb):
            rows = slice(b * rb, (b + 1) * rb)
            if j + 1 < nj:
                up_proj(j + 1, 0, b)
            cg = conv(j, 0, b)
            if j + 1 < nj:
                up_proj(j + 1, 1, b)
            cv = conv(j, 1, b)
            a = (cg * (1.0 / (1.0 + jnp.exp(-cg)))) * cv
            a_ref[rows, j * tf:(j + 1) * tf] = a.astype(BF16)

    y = xin_ref[halo:n, :] + _dot(a_ref[...], wd_ref[...])
    if final_norm:
        y = _rmsnorm(y, gf_ref[...])
    o_ref[...] = y


def _mixer_ffn(x2, seq_len, mixer, mixer_args, g, w_up, conv_w, conv_b, w_down, g_final, final_norm):
    N, D = x2.shape
    dff = w_down.shape[0]
    tm, tf = TM_FFN, TF_FFN
    resident = lambda a: pl.BlockSpec(a.shape, lambda i: (0,) * a.ndim, pipeline_mode=pl.Buffered(1))
    tile = pl.BlockSpec((tm, D), lambda i: (i, 0))

    def left_context(rows):
        return pl.BlockSpec((rows, D), lambda i: (jnp.maximum(i * (tm // rows) - 1, 0), 0))

    if mixer == "out_proj":
        attn, w_out = mixer_args
        mix_in = [x2, x2, attn, attn, w_out]
        mix_specs = [tile, left_context(HALO_FFN), tile, left_context(HALO_ATTN_BF16), resident(w_out)]
    else:
        mix_in = [x2, x2] + list(mixer_args)
        mix_specs = [tile, left_context(HALO_POOL)] + [resident(a) for a in mixer_args]
    weights = [g, w_up, conv_w, conv_b, w_down, g_final]
    return pl.pallas_call(
        functools.partial(_ffn_body, tm=tm, tf=tf, dff=dff, tiles_per_seq=seq_len // tm,
                          final_norm=final_norm, mixer=mixer),
        grid=(N // tm,),
        in_specs=mix_specs + [resident(a) for a in weights],
        out_specs=tile,
        out_shape=jax.ShapeDtypeStruct((N, D), F32),
        scratch_shapes=[pltpu.VMEM((HALO_FFN + tm, D), BF16),
                        pltpu.VMEM((HALO_FFN + tm, D), F32),
                        pltpu.VMEM((tm, dff), BF16),
                        ] + [pltpu.VMEM((HALO_FFN + tm, tf), F32)] * 4,
        compiler_params=pltpu.CompilerParams(dimension_semantics=("parallel",),
                                             vmem_limit_bytes=VMEM_LIMIT),
        name="mixer_ffn_" + mixer,
    )(*mix_in, *weights)


def _pad_cols(w, width):
    return jnp.pad(w, ((0, 0), (0, width - w.shape[1])))


def _inproj_weight(w_in):
    D = w_in.shape[0]
    kv = N_GROUPS * HEAD_DIM
    sizes = [N_HEADS * HEAD_DIM] + [kv] * 6 + [N_BRANCH * N_HEADS]
    offs = np.concatenate([[0], np.cumsum(sizes)])
    q, k_c, v_c, k_s, v_s, k_w, v_w, gt = [w_in[:, offs[n]:offs[n + 1]] for n in range(8)]
    per_group = lambda w, n: [_pad_cols(w[:, g * n:(g + 1) * n], LANES) for g in range(N_GROUPS)]
    cols = ([q] + per_group(k_s, HEAD_DIM) + per_group(k_w, HEAD_DIM) + [k_c, v_c, v_s, v_w]
            + per_group(gt, N_BRANCH * HEADS_PER_GROUP))
    return jnp.concatenate(cols, axis=1).astype(BF16)


def _rope_tables(pos):
    inv = 1.0 / (ROPE_THETA ** (jnp.arange(0, HEAD_DIM, 2, dtype=F32) / HEAD_DIM))
    ang = pos.astype(F32)[:, None] * inv[None, :]
    return jnp.cos(ang), jnp.sin(ang)


def _rot_half_cols(w):
    half = HEAD_DIM // 2
    return jnp.concatenate([-w[..., half:], w[..., :half]], axis=-1)


def _overlap_matrix(n_cmp_pad):
    j = np.arange(n_cmp_pad)[:, None]
    s = np.arange(LANES)[None, :]
    lo = np.maximum(j * CMP_STRIDE, s * SEL_BLOCK)
    hi = np.minimum(j * CMP_STRIDE + CMP_BLOCK, (s + 1) * SEL_BLOCK)
    return jnp.asarray(np.clip(hi - lo, 0, None) / CMP_BLOCK, dtype=BF16)


def kernel(x, norm_mix_0, nsa_w_in, cmp_k_pos, cmp_k_w1, cmp_k_b1, cmp_k_w2, cmp_k_b2, cmp_v_pos, cmp_v_w1, cmp_v_b1, cmp_v_w2, cmp_v_b2, nsa_w_out, norm_ffn_0, ffn_up_0, ffn_conv_w_0, ffn_conv_b_0, ffn_down_0, norm_mix_1, pool_w, pool_b, pool_scale, norm_ffn_1, ffn_up_1, ffn_conv_w_1, ffn_conv_b_1, ffn_down_1, norm_final):
    B, T, D = x.shape
    assert D == N_HEADS * HEAD_DIM and SEL_BLOCK == 64
    assert T % TM_FFN == 0 and T % TK_SEL == 0 and TK_SEL % TQ == 0 and T >= WINDOW + TQ
    n_cmp_pad = T // CMP_STRIDE
    assert T // SEL_BLOCK <= LANES
    assert HEADS_PER_GROUP < FORCE_BONUS
    row = lambda v: v.reshape(1, -1)

    cos, sin = _rope_tables(jnp.arange(T))
    cos4 = jnp.tile(cos, (1, 4))
    sin4 = jnp.tile(jnp.concatenate([-sin, sin], axis=1), (1, 2))
    q, qsw, ks, kw, kc, vc, vs, vw, gates = _in_proj(
        x, row(norm_mix_0), _inproj_weight(nsa_w_in), cos4, sin4)

    ccos, csin = _rope_tables(jnp.arange(n_cmp_pad) * CMP_STRIDE + (CMP_BLOCK - 1))
    zeros64 = jnp.zeros((n_cmp_pad, HEAD_DIM), F32)
    ccos2 = jnp.concatenate([ccos, ccos, zeros64], axis=1)
    csin2 = jnp.concatenate([csin, csin, zeros64], axis=1)
    pos_rows = lambda p: jnp.broadcast_to(p.reshape(1, -1), (SUBLANES, p.size)).astype(BF16)
    zero_w2 = jnp.zeros_like(cmp_v_w2)
    kparams = (cmp_k_w1.astype(BF16), pos_rows(cmp_k_pos), row(cmp_k_b1),
               jnp.concatenate([cmp_k_w2, _rot_half_cols(cmp_k_w2)], axis=1).astype(BF16),
               row(jnp.concatenate([cmp_k_b2, _rot_half_cols(cmp_k_b2)])))
    vparams = (cmp_v_w1.astype(BF16), pos_rows(cmp_v_pos), row(cmp_v_b1),
               jnp.concatenate([cmp_v_w2, zero_w2], axis=1).astype(BF16),
               jnp.concatenate([zero_w2, cmp_v_w2], axis=1).astype(BF16),
               row(jnp.concatenate([cmp_v_b2, cmp_v_b2])))
    kco, vco = _compress(kc, vc, kparams, vparams, ccos2, csin2)

    o = _nsa_attn(q, qsw, ks, kw, vs, vw, kco, vco, gates, _overlap_matrix(n_cmp_pad))

    x2 = _mixer_ffn(x.reshape(B * T, D), T, "out_proj",
                    (o.reshape(B * T, D), nsa_w_out.astype(BF16)),
                    row(norm_ffn_0), ffn_up_0.astype(BF16), ffn_conv_w_0, row(ffn_conv_b_0),
                    ffn_down_0.astype(BF16), row(norm_final), False)
    x2 = _mixer_ffn(x2, T, "pool",
                    (row(norm_mix_1), pool_w.astype(BF16), row(pool_b.reshape(-1)), row(pool_scale)),
                    row(norm_ffn_1), ffn_up_1.astype(BF16), ffn_conv_w_1, row(ffn_conv_b_1),
                    ffn_down_1.astype(BF16), row(norm_final), True)
    return x2.reshape(B, T, D)
```
